```python
import math
import jax, jax.numpy as jnp
from jax import lax
import numpy as np

D_MODEL = 2048
BATCH = 4
SEQ = 2048
DEPTH = 1
DEC_BATCH = 8
DEC_SEQ = 4
PAST_LEN = 16384
PAGE_SIZE = 128

HEAD_DIM = 128
N_HEADS_A = 8
N_HEADS_B = 8
N_KV_B = 2
HPG = N_HEADS_B // N_KV_B
D_A = N_HEADS_A * HEAD_DIM
D_B = N_HEADS_B * HEAD_DIM
D_KV_B = N_KV_B * HEAD_DIM
ROPE_DIM = HEAD_DIM // 4
ROPE_THETA = 500000.0
SB_QBLOCK = 128
CMP_LEN = 32
CMP_STRIDE = 16
SLC_BLOCK = 64
N_SELECT = 16
WINDOW = 512
NSA_QBLOCK = 32
WIN_QBLOCK = 128
FORCE_SCORE = 1.0e4
PEER_HEADS = 8
PEER_NKEYS = 128
PEER_N = PEER_NKEYS * PEER_NKEYS
PEER_QDIM = 256
PEER_TOPK = 16
PEER_TBLOCK = 128
ALPHA = (2 * DEPTH) ** 0.25
BETA = (8 * DEPTH) ** -0.25
LN_EPS = 1e-5
IN_SIZES = (D_A, D_A, D_A, D_B, D_KV_B, D_KV_B, D_KV_B, D_KV_B, D_KV_B, D_KV_B, 3 * N_HEADS_B)
VALUE_PARTS = (2, 5, 7, 9)
IN_COLS = 3 * D_A + D_B + 6 * D_KV_B + 3 * N_HEADS_B

kernel_name = "sb_nsa_peer_hybrid_step"


def layer_norm(x, g, b):
    xf = x.astype(jnp.float32)
    mu = jnp.mean(xf, axis=-1, keepdims=True)
    var = jnp.mean(jnp.square(xf - mu), axis=-1, keepdims=True)
    return ((xf - mu) * lax.rsqrt(var + LN_EPS) * g + b).astype(x.dtype)


def partial_rope(x, pos):
    half = ROPE_DIM // 2
    inv = ROPE_THETA ** (-jnp.arange(half, dtype=jnp.float32) / half)
    ang = pos.astype(jnp.float32)[:, None] * inv[None, :]
    cos, sin = jnp.cos(ang)[:, None, :], jnp.sin(ang)[:, None, :]
    xr = x[..., :ROPE_DIM].astype(jnp.float32)
    x1, x2 = xr[..., :half], xr[..., half:]
    rot = jnp.concatenate([x1 * cos - x2 * sin, x2 * cos + x1 * sin], axis=-1)
    return jnp.concatenate([rot.astype(x.dtype), x[..., ROPE_DIM:]], axis=-1)


def masked_softmax(s, mask):
    s = jnp.where(mask, s.astype(jnp.float32), -1e30)
    m = jnp.max(s, axis=-1, keepdims=True)
    e = jnp.where(mask, jnp.exp(s - m), 0.0)
    return e / jnp.maximum(jnp.sum(e, axis=-1, keepdims=True), 1e-30)


def project(x, w_in, pos):
    B, T = x.shape[:2]
    offsets = np.cumsum(IN_SIZES)[:-1].tolist()
    qa, ka, va, qb, kc, vc, ks, vs, kw, vw, g = jnp.split(x @ w_in, offsets, axis=-1)
    ha = lambda t: t.reshape(B, T, N_HEADS_A, HEAD_DIM)
    hb = lambda t: t.reshape(B, T, N_KV_B, HEAD_DIM)
    qb = partial_rope(qb.reshape(B, T, N_HEADS_B, HEAD_DIM), pos).reshape(B, T, N_KV_B, HPG, HEAD_DIM)
    kc, ks, kw = partial_rope(hb(kc), pos), partial_rope(hb(ks), pos), partial_rope(hb(kw), pos)
    gates = jax.nn.sigmoid(g).reshape(B, T, N_KV_B, HPG, 3)
    return ha(qa), ha(ka), ha(va), qb, kc, hb(vc), ks, hb(vs), kw, hb(vw), gates


def stick_breaking(q, k, v, q_pos, k_pos):
    z = jnp.einsum('bqhd,bkhd->bhqk', q, k).astype(jnp.float32) / math.sqrt(HEAD_DIM)
    mask = k_pos[None, :] < q_pos[:, None]
    log_beta = jax.nn.log_sigmoid(z)
    log_rest = jnp.where(mask, jax.nn.log_sigmoid(-z), 0.0)
    nxt = jnp.concatenate([log_rest[..., 1:], jnp.zeros_like(log_rest[..., :1])], axis=-1)
    after = lax.cumsum(nxt, axis=3, reverse=True)
    w = jnp.where(mask, jnp.exp(log_beta + after), 0.0)
    return jnp.einsum('bhqk,bkhd->bqhd', w.astype(v.dtype), v)


def sb_prompt(q, k, v, pos):
    B, T = q.shape[:2]
    nb = T // SB_QBLOCK
    qb = jnp.moveaxis(q.reshape(B, nb, SB_QBLOCK, N_HEADS_A, HEAD_DIM), 1, 0)
    out = lax.map(lambda a: stick_breaking(a[0], k, v, a[1], pos), (qb, pos.reshape(nb, SB_QBLOCK)))
    return jnp.moveaxis(out, 0, 1).reshape(B, T, N_HEADS_A, HEAD_DIM)


def compress(rows, pe, w1, w2):
    L = rows.shape[1]
    n_cmp = (L - CMP_LEN) // CMP_STRIDE + 1
    idx = np.arange(n_cmp)[:, None] * CMP_STRIDE + np.arange(CMP_LEN)[None, :]
    blk = rows[:, idx] + pe[None, None, :, None, :]
    h = jax.nn.gelu(jnp.einsum('bnlgd,lde->bnge', blk, w1))
    return jnp.einsum('bnge,ef->bngf', h, w2)


def cmp_to_slc(n_cmp, n_sel):
    i = np.arange(n_cmp)[:, None]
    j = np.arange(n_sel)[None, :]
    lo = np.maximum(i * CMP_STRIDE, j * SLC_BLOCK)
    hi = np.minimum(i * CMP_STRIDE + CMP_LEN, (j + 1) * SLC_BLOCK)
    return jnp.asarray((np.maximum(hi - lo, 0) / CMP_STRIDE).astype(np.float32))


def to_sel_blocks(rows):
    B, L, G, d = rows.shape
    n_sel = -(-L // SLC_BLOCK)
    rows = jnp.pad(rows, ((0, 0), (0, n_sel * SLC_BLOCK - L), (0, 0), (0, 0)))
    return rows.reshape(B, n_sel, SLC_BLOCK, G, d).transpose(0, 3, 1, 2, 4)


def nsa_cmp_slc(q, q_pos, kcmp, vcmp, ks_blk, vs_blk):
    B, Tq = q.shape[:2]
    n_cmp, n_sel = kcmp.shape[1], ks_blk.shape[2]
    scale = 1.0 / math.sqrt(HEAD_DIM)
    cmp_end = jnp.arange(n_cmp, dtype=jnp.int32) * CMP_STRIDE + (CMP_LEN - 1)
    cmask = (cmp_end[None, :] <= q_pos[:, None])[None, :, None, None, :]
    s = jnp.einsum('btghd,bngd->btghn', q, kcmp).astype(jnp.float32) * scale
    p = masked_softmax(s, cmask)
    o_cmp = jnp.einsum('btghn,bngd->btghd', p.astype(vcmp.dtype), vcmp)
    imp = jnp.einsum('btghn,nj->btgj', p, cmp_to_slc(n_cmp, n_sel))
    j = jnp.arange(n_sel, dtype=jnp.int32)[None, :]
    qblk = (q_pos // SLC_BLOCK)[:, None]
    forced = ((j == 0) | (j == qblk) | (j == qblk - 1))[None, :, None, :]
    valid = (j * SLC_BLOCK <= q_pos[:, None])[None, :, None, :]
    imp = jnp.where(forced, FORCE_SCORE, jnp.where(valid, imp, -FORCE_SCORE))
    n_top = min(N_SELECT, n_sel)
    _, sel = lax.top_k(imp, n_top)
    b_i = jnp.arange(B)[:, None, None, None]
    g_i = jnp.arange(N_KV_B)[None, None, :, None]
    ks = ks_blk[b_i, g_i, sel]
    vs = vs_blk[b_i, g_i, sel]
    kpos = sel[..., None] * SLC_BLOCK + jnp.arange(SLC_BLOCK, dtype=jnp.int32)
    smask = (kpos <= q_pos[None, :, None, None, None]).reshape(B, Tq, N_KV_B, 1, n_top * SLC_BLOCK)
    s2 = jnp.einsum('btghd,btgnld->btghnl', q, ks).astype(jnp.float32) * scale
    p2 = masked_softmax(s2.reshape(B, Tq, N_KV_B, HPG, n_top * SLC_BLOCK), smask)
    o_slc = jnp.einsum('btghk,btgkd->btghd', p2.astype(vs.dtype),
                       vs.reshape(B, Tq, N_KV_B, n_top * SLC_BLOCK, HEAD_DIM))
    return o_cmp, o_slc


def band_attend(q, q_pos, k, v, k_pos):
    s = jnp.einsum('bnqghd,bnkgd->bnghqk', q, k).astype(jnp.float32) / math.sqrt(HEAD_DIM)
    dlt = q_pos[:, :, None] - k_pos[:, None, :]
    mask = (dlt >= 0) & (dlt < WINDOW) & (k_pos[:, None, :] >= 0)
    p = masked_softmax(s, mask[None, :, None, None])
    return jnp.einsum('bnghqk,bnkgd->bnqghd', p.astype(v.dtype), v)


def window_prompt(q, kw, vw):
    B, T = q.shape[:2]
    nb, nw = T // WIN_QBLOCK, WINDOW // WIN_QBLOCK
    pad = ((0, 0), (WINDOW, 0), (0, 0), (0, 0))
    kpb = jnp.pad(kw, pad).reshape(B, nb + nw, WIN_QBLOCK, N_KV_B, HEAD_DIM)
    vpb = jnp.pad(vw, pad).reshape(B, nb + nw, WIN_QBLOCK, N_KV_B, HEAD_DIM)
    k_slab = jnp.concatenate([kpb[:, i:i + nb] for i in range(nw + 1)], axis=2)
    v_slab = jnp.concatenate([vpb[:, i:i + nb] for i in range(nw + 1)], axis=2)
    q_pos = jnp.arange(T, dtype=jnp.int32).reshape(nb, WIN_QBLOCK)
    k_pos = (jnp.arange(nb, dtype=jnp.int32) * WIN_QBLOCK)[:, None] - WINDOW + jnp.arange(WINDOW + WIN_QBLOCK, dtype=jnp.int32)[None, :]
    o = band_attend(q.reshape(B, nb, WIN_QBLOCK, N_KV_B, HPG, HEAD_DIM), q_pos, k_slab, v_slab, k_pos)
    return o.reshape(B, T, N_KV_B, HPG, HEAD_DIM)


def nsa_merge(o_c, o_s, o_w, gates):
    return gates[..., 0:1] * o_c + gates[..., 1:2] * o_s + gates[..., 2:3] * o_w


def mixers_prompt(x, w_in, cmp_pe_k, cmp_w1_k, cmp_w2_k, cmp_pe_v, cmp_w1_v, cmp_w2_v):
    B, T, _ = x.shape
    pos = jnp.arange(T, dtype=jnp.int32)
    qa, ka, va, qb, kc, vc, ks, vs, kw, vw, gates = project(x, w_in, pos)
    o_a = sb_prompt(qa, ka, va, pos)
    kcmp = compress(kc, cmp_pe_k, cmp_w1_k, cmp_w2_k)
    vcmp = compress(vc, cmp_pe_v, cmp_w1_v, cmp_w2_v)
    ks_blk, vs_blk = to_sel_blocks(ks), to_sel_blocks(vs)
    nb = T // NSA_QBLOCK
    q_blocks = jnp.moveaxis(qb.reshape(B, nb, NSA_QBLOCK, N_KV_B, HPG, HEAD_DIM), 1, 0)
    o_c, o_s = lax.map(lambda a: nsa_cmp_slc(a[0], a[1], kcmp, vcmp, ks_blk, vs_blk),
                       (q_blocks, pos.reshape(nb, NSA_QBLOCK)))
    unblock = lambda t: jnp.moveaxis(t, 0, 1).reshape(B, T, N_KV_B, HPG, HEAD_DIM)
    o_w = window_prompt(qb, kw, vw)
    o_b = nsa_merge(unblock(o_c), unblock(o_s), o_w, gates)
    mix = jnp.concatenate([o_a.reshape(B, T, D_A), o_b.reshape(B, T, D_B)], axis=-1)
    new_a = jnp.stack([ka, va], axis=2)
    new_nsa = jnp.stack([kc, vc, ks, vs], axis=2)
    new_win = jnp.stack([kw, vw], axis=2)[:, T - min(WINDOW, T):]
    return mix, new_a, new_nsa, new_win


def mixers_sample(x, cache_a, cache_nsa, state_win, page_table, w_in,
                  cmp_pe_k, cmp_w1_k, cmp_w2_k, cmp_pe_v, cmp_w1_v, cmp_w2_v):
    B, T, _ = x.shape
    past = page_table.shape[1] * PAGE_SIZE
    pos = past + jnp.arange(T, dtype=jnp.int32)
    k_pos = jnp.arange(past + T, dtype=jnp.int32)
    qa, ka, va, qb, kc, vc, ks, vs, kw, vw, gates = project(x, w_in, pos)
    past_a = cache_a[page_table].reshape(B, past, 2, N_HEADS_A, HEAD_DIM)
    o_a = stick_breaking(qa, jnp.concatenate([past_a[:, :, 0], ka], axis=1),
                         jnp.concatenate([past_a[:, :, 1], va], axis=1), pos, k_pos)
    past_n = cache_nsa[page_table].reshape(B, past, 4, N_KV_B, HEAD_DIM)
    rows = [jnp.concatenate([past_n[:, :, i], new], axis=1) for i, new in enumerate((kc, vc, ks, vs))]
    kcmp = compress(rows[0], cmp_pe_k, cmp_w1_k, cmp_w2_k)
    vcmp = compress(rows[1], cmp_pe_v, cmp_w1_v, cmp_w2_v)
    o_c, o_s = nsa_cmp_slc(qb, pos, kcmp, vcmp, to_sel_blocks(rows[2]), to_sel_blocks(rows[3]))
    slab = jnp.concatenate([state_win, jnp.stack([kw, vw], axis=2)], axis=1)
    n_buf = state_win.shape[1]
    w_pos = past - n_buf + jnp.arange(n_buf + T, dtype=jnp.int32)
    o_w = band_attend(qb[:, None], pos[None], slab[:, None, :, 0], slab[:, None, :, 1], w_pos[None])[:, 0]
    o_b = nsa_merge(o_c, o_s, o_w, gates)
    mix = jnp.concatenate([o_a.reshape(B, T, D_A), o_b.reshape(B, T, D_B)], axis=-1)
    new_a = jnp.stack([ka, va], axis=2)
    new_nsa = jnp.stack([kc, vc, ks, vs], axis=2)
    keep = min(WINDOW, past + T)
    new_win = slab[:, slab.shape[1] - keep:]
    return mix, new_a, new_nsa, new_win


def peer_ffn(x, w_query, sub_keys, u_tab, v_tab):
    B, T, D = x.shape
    n = B * T
    xt = jnp.pad(x.reshape(n, D), ((0, -n % PEER_TBLOCK), (0, 0)))
    K = PEER_TOPK

    def block(xc):
        q = (xc @ w_query).reshape(PEER_TBLOCK, PEER_HEADS, 2, PEER_QDIM // 2)
        s = jnp.einsum('thcd,ckd->thck', q, sub_keys).astype(jnp.float32)
        top_s, top_i = lax.top_k(s, K)
        cand_s = (top_s[:, :, 0, :, None] + top_s[:, :, 1, None, :]).reshape(PEER_TBLOCK, PEER_HEADS, K * K)
        cand_i = (top_i[:, :, 0, :, None] * PEER_NKEYS + top_i[:, :, 1, None, :]).reshape(PEER_TBLOCK, PEER_HEADS, K * K)
        best_s, best_j = lax.top_k(cand_s, K)
        experts = jnp.take_along_axis(cand_i, best_j, axis=-1)
        gate = jax.nn.softmax(best_s, axis=-1)
        act = jax.nn.gelu(jnp.einsum('thkd,td->thk', u_tab[experts], xc))
        v = v_tab[experts]
        return jnp.einsum('thk,thkd->td', (gate * act).astype(v.dtype), v)

    y = lax.map(block, xt.reshape(-1, PEER_TBLOCK, D)).reshape(-1, D)[:n]
    return y.reshape(B, T, D)


def post_norm_block(x, mix, w_out, ln1_g, ln1_b, peer_w_query, peer_sub_keys, peer_u, peer_v, ln2_g, ln2_b):
    h = layer_norm(ALPHA * x + mix @ w_out, ln1_g, ln1_b)
    return layer_norm(ALPHA * h + peer_ffn(h, peer_w_query, peer_sub_keys, peer_u, peer_v), ln2_g, ln2_b)


def setup_inputs(seed: int = 0) -> dict:
    key = jax.random.key(seed)
    ks = jax.random.split(key, 24)
    f32 = jnp.float32
    nrm = lambda k, shape, scale: jax.random.normal(k, shape, f32) * scale
    n_pages = PAST_LEN // PAGE_SIZE
    n_used = DEC_BATCH * n_pages
    n_pool = (5 * n_used + 3) // 4
    page_table = jax.random.permutation(ks[0], n_pool)[:n_used].reshape(DEC_BATCH, n_pages).astype(jnp.int32)
    col_scale = np.concatenate([np.full(s, BETA if i in VALUE_PARTS else 1.0, np.float32)
                                for i, s in enumerate(IN_SIZES)])
    w_in = nrm(ks[1], (DEPTH, D_MODEL, IN_COLS), D_MODEL ** -0.5) * jnp.asarray(col_scale)
    return {
        'x_prompt': nrm(ks[2], (BATCH, SEQ, D_MODEL), 1.0),
        'x_sample': nrm(ks[3], (DEC_BATCH, DEC_SEQ, D_MODEL), 1.0),
        'cache_a': nrm(ks[4], (DEPTH, n_pool, PAGE_SIZE, 2, N_HEADS_A, HEAD_DIM), 1.0),
        'cache_nsa': nrm(ks[5], (DEPTH, n_pool, PAGE_SIZE, 4, N_KV_B, HEAD_DIM), 1.0),
        'state_win': nrm(ks[6], (DEPTH, DEC_BATCH, min(WINDOW, PAST_LEN), 2, N_KV_B, HEAD_DIM), 1.0),
        'page_table': page_table,
        'w_in': w_in,
        'cmp_pe_k': nrm(ks[7], (DEPTH, CMP_LEN, HEAD_DIM), 0.02),
        'cmp_w1_k': nrm(ks[8], (DEPTH, CMP_LEN, HEAD_DIM, HEAD_DIM), (CMP_LEN * HEAD_DIM) ** -0.5),
        'cmp_w2_k': nrm(ks[9], (DEPTH, HEAD_DIM, HEAD_DIM), HEAD_DIM ** -0.5),
        'cmp_pe_v': nrm(ks[10], (DEPTH, CMP_LEN, HEAD_DIM), 0.02),
        'cmp_w1_v': nrm(ks[11], (DEPTH, CMP_LEN, HEAD_DIM, HEAD_DIM), (CMP_LEN * HEAD_DIM) ** -0.5),
        'cmp_w2_v': nrm(ks[12], (DEPTH, HEAD_DIM, HEAD_DIM), HEAD_DIM ** -0.5),
        'w_out': nrm(ks[13], (DEPTH, D_MODEL, D_MODEL), BETA * D_MODEL ** -0.5),
        'ln1_g': 1.0 + nrm(ks[14], (DEPTH, D_MODEL), 0.02),
        'ln1_b': nrm(ks[15], (DEPTH, D_MODEL), 0.02),
        'peer_w_query': nrm(ks[16], (DEPTH, D_MODEL, PEER_HEADS * PEER_QDIM), D_MODEL ** -0.5),
        'peer_sub_keys': nrm(ks[17], (DEPTH, 2, PEER_NKEYS, PEER_QDIM // 2), (PEER_QDIM // 2) ** -0.5),
        'peer_u': nrm(ks[18], (DEPTH, PEER_N, D_MODEL), D_MODEL ** -0.5),
        'peer_v': nrm(ks[19], (DEPTH, PEER_N, D_MODEL), BETA),
        'ln2_g': 1.0 + nrm(ks[20], (DEPTH, D_MODEL), 0.02),
        'ln2_b': nrm(ks[21], (DEPTH, D_MODEL), 0.02),
    }


def reference(x_prompt, x_sample, cache_a, cache_nsa, state_win, page_table, w_in,
              cmp_pe_k, cmp_w1_k, cmp_w2_k, cmp_pe_v, cmp_w1_v, cmp_w2_v, w_out, ln1_g, ln1_b,
              peer_w_query, peer_sub_keys, peer_u, peer_v, ln2_g, ln2_b):
    yp, ys = x_prompt, x_sample
    a_p, n_p, w_p, a_s, n_s, w_s = [], [], [], [], [], []
    for l in range(DEPTH):
        cmp_args = (cmp_pe_k[l], cmp_w1_k[l], cmp_w2_k[l], cmp_pe_v[l], cmp_w1_v[l], cmp_w2_v[l])
        ffn_args = (w_out[l], ln1_g[l], ln1_b[l], peer_w_query[l], peer_sub_keys[l], peer_u[l], peer_v[l], ln2_g[l], ln2_b[l])
        mix_p, na_p, nn_p, nw_p = mixers_prompt(yp, w_in[l], *cmp_args)
        mix_s, na_s, nn_s, nw_s = mixers_sample(ys, cache_a[l], cache_nsa[l], state_win[l], page_table, w_in[l], *cmp_args)
        yp = post_norm_block(yp, mix_p, *ffn_args)
        ys = post_norm_block(ys, mix_s, *ffn_args)
        a_p.append(na_p); n_p.append(nn_p); w_p.append(nw_p)
        a_s.append(na_s); n_s.append(nn_s); w_s.append(nw_s)
    return (yp, ys, jnp.stack(a_p), jnp.stack(n_p), jnp.stack(w_p), jnp.stack(a_s), jnp.stack(n_s), jnp.stack(w_s))
```

```python
import functools
import math

import numpy as np
import jax
import jax.numpy as jnp
from jax import lax
from jax.experimental import pallas as pl
from jax.experimental.pallas import tpu as pltpu

F32 = jnp.float32
BF16 = jnp.bfloat16

LANES = 128
VMEM_LIMIT = 56 * 1024 * 1024

HEAD_DIM = 128
N_HEADS_A = 8
N_HEADS_B = 8
N_KV_B = 2
HPG = N_HEADS_B // N_KV_B
D_A = N_HEADS_A * HEAD_DIM
D_B = N_HEADS_B * HEAD_DIM
D_KV_B = N_KV_B * HEAD_DIM
ROPE_DIM = HEAD_DIM // 4
ROPE_THETA = 500000.0
CMP_LEN = 32
CMP_STRIDE = 16
SLC_BLOCK = 64
N_SELECT = 16
WINDOW = 512
FORCE_SCORE = 1.0e4
PEER_HEADS = 8
PEER_NKEYS = 128
PEER_TOPK = 16
DEPTH = 1
ALPHA = (2 * DEPTH) ** 0.25
LN_EPS = 1e-5
NEG_BIG = -1e30
SCALE = 1.0 / math.sqrt(HEAD_DIM)


def _dot(a, b):
    return jnp.dot(a, b, preferred_element_type=F32)


def _dot_nt(a, b):
    return lax.dot_general(a, b, (((1,), (1,)), ((), ())), preferred_element_type=F32)


def _dot_tn(a, b):
    return lax.dot_general(a, b, (((0,), (0,)), ((), ())), preferred_element_type=F32)


def _dot_split(x, w):
    hi = x.astype(BF16)
    lo = (x - hi.astype(F32)).astype(BF16)
    return _dot(hi, w) + _dot(lo, w)


def _params(*sem):
    return pltpu.CompilerParams(dimension_semantics=sem, vmem_limit_bytes=VMEM_LIMIT)


def _layer_norm(r, g, b):
    mu = jnp.mean(r, axis=-1, keepdims=True)
    d = r - mu
    var = jnp.mean(d * d, axis=-1, keepdims=True)
    return d * lax.rsqrt(var + LN_EPS) * g + b


def _inproj_kernel(x_ref, w_ref, c_ref, sa_ref, sb_ref, o_ref, *, rope_flags, sigmoid):
    acc = _dot(x_ref[...], w_ref[...])
    for j, flag in enumerate(rope_flags):
        blk = acc[:, j * LANES:(j + 1) * LANES]
        if flag:
            blk = (blk * c_ref[...]
                   + pltpu.roll(blk, LANES - ROPE_DIM // 2, 1) * sa_ref[...]
                   + pltpu.roll(blk, ROPE_DIM // 2, 1) * sb_ref[...])
        if sigmoid:
            blk = jax.nn.sigmoid(blk)
        o_ref[:, j * LANES:(j + 1) * LANES] = blk


def _inproj(xb, w, tabs, rope_flags, sigmoid=False):
    m, k = xb.shape
    n = w.shape[1]
    c, sa, sb = tabs
    tm = min(256, m)
    nt = c.shape[0] // tm
    tab_spec = pl.BlockSpec((tm, LANES), lambda i: (i % nt, 0))
    return pl.pallas_call(
        functools.partial(_inproj_kernel, rope_flags=tuple(rope_flags), sigmoid=sigmoid),
        grid=(m // tm,),
        in_specs=[pl.BlockSpec((tm, k), lambda i: (i, 0)),
                  pl.BlockSpec((k, n), lambda i: (0, 0)),
                  tab_spec, tab_spec, tab_spec],
        out_specs=pl.BlockSpec((tm, n), lambda i: (i, 0)),
        out_shape=jax.ShapeDtypeStruct((m, n), F32),
        compiler_params=_params("parallel"),
    )(xb, w, c, sa, sb)


def _rope_tables(pos):
    half = ROPE_DIM // 2
    inv = ROPE_THETA ** (-jnp.arange(half, dtype=F32) / half)
    ang = pos.astype(F32)[:, None] * inv[None, :]
    cos, sin = jnp.cos(ang), jnp.sin(ang)
    t = pos.shape[0]
    ones = jnp.ones((t, LANES - ROPE_DIM), F32)
    zeros = jnp.zeros((t, LANES - half), F32)
    c = jnp.concatenate([cos, cos, ones], axis=1)
    sa = jnp.concatenate([-sin, zeros], axis=1)
    sb = jnp.concatenate([jnp.zeros((t, half), F32), sin, jnp.zeros((t, LANES - ROPE_DIM), F32)], axis=1)
    return c, sa, sb


def _project_all(xb, wparts, tabs):
    qa = _inproj(xb, wparts["qa"], tabs, [0] * 8)
    kva = _inproj(xb, wparts["kva"], tabs, [0] * 16)
    qb = _inproj(xb, wparts["qb"], tabs, [1] * 8)
    nsa = _inproj(xb, wparts["nsa"], tabs, [1, 1, 0, 0, 1, 1, 0, 0])
    win = _inproj(xb, wparts["win"], tabs, [1, 1, 0, 0])
    gates = _inproj(xb, wparts["gates"], tabs, [0, 0], sigmoid=True)
    return qa, kva, qb, nsa, win, gates


def _sb_block(q, k, v, mask, c, tri):
    z = _dot_nt(q, k) * SCALE
    ls = jnp.minimum(z, 0.0) - jnp.log1p(jnp.exp(-jnp.abs(z)))
    lr = jnp.where(mask, ls - z, 0.0)
    after = _dot_split(lr, tri) + c
    w = jnp.where(mask, jnp.exp(ls + after), 0.0)
    return w, c + jnp.sum(lr, axis=1, keepdims=True)


def _tri(n):
    row = lax.broadcasted_iota(jnp.int32, (n, n), 0)
    col = lax.broadcasted_iota(jnp.int32, (n, n), 1)
    return (row > col).astype(BF16)


def _sb_prompt_kernel(q_ref, k_ref, v_ref, o_ref, *, tq):
    i = pl.program_id(2)
    q = q_ref[...].astype(BF16)
    tri = _tri(tq)
    row = lax.broadcasted_iota(jnp.int32, (tq, tq), 0)
    col = lax.broadcasted_iota(jnp.int32, (tq, tq), 1)

    def body(step, carry):
        c, acc = carry
        off = pl.multiple_of((i - step) * tq, tq)
        k = k_ref[pl.ds(off, tq), :].astype(BF16)
        v = v_ref[pl.ds(off, tq), :].astype(BF16)
        mask = (col < row) | (step > 0)
        w, c = _sb_block(q, k, v, mask, c, tri)
        return c, acc + _dot(w.astype(BF16), v)

    _, acc = lax.fori_loop(0, i + 1, body,
                           (jnp.zeros((tq, 1), F32), jnp.zeros((tq, HEAD_DIM), F32)))
    o_ref[...] = acc


def _sb_prompt(qa, kva, bsz, t):
    tq = 128
    nq = t // tq
    return pl.pallas_call(
        functools.partial(_sb_prompt_kernel, tq=tq),
        grid=(bsz, N_HEADS_A, nq),
        in_specs=[pl.BlockSpec((tq, HEAD_DIM), lambda b, h, i: (b * nq + i, h)),
                  pl.BlockSpec((t, HEAD_DIM), lambda b, h, i: (b, h)),
                  pl.BlockSpec((t, HEAD_DIM), lambda b, h, i: (b, N_HEADS_A + h))],
        out_specs=pl.BlockSpec((tq, HEAD_DIM), lambda b, h, i: (b * nq + i, h)),
        out_shape=jax.ShapeDtypeStruct((bsz * t, D_A), F32),
        compiler_params=_params("parallel", "parallel", "parallel"),
    )(qa, kva, kva)


def _sb_decode_kernel(pt_ref, q_ref, new_ref, page_ref, o_ref, c_ref, acc_ref, *, n_new, page):
    p = pl.program_id(1)
    rows = N_HEADS_A * n_new
    q = q_ref[...]
    qrep = jnp.concatenate([q] * N_HEADS_A, axis=0)
    rr = lax.broadcasted_iota(jnp.int32, (rows, D_A), 0)
    ll = lax.broadcasted_iota(jnp.int32, (rows, D_A), 1)
    qbd = jnp.where(ll // HEAD_DIM == rr // n_new, qrep, 0.0).astype(BF16)
    tri = _tri(page)
    t_of_row = lax.broadcasted_iota(jnp.int32, (rows, page), 0) % n_new
    col = lax.broadcasted_iota(jnp.int32, (rows, page), 1)

    def step(kv_ref, mask):
        k = kv_ref[:, 0:D_A].astype(BF16)
        v = kv_ref[:, D_A:2 * D_A].astype(BF16)
        w, c = _sb_block(qbd, k, v, mask, c_ref[...], tri)
        c_ref[...] = c
        acc_ref[...] += _dot(w.astype(BF16), v)

    @pl.when(p == 0)
    def _():
        c_ref[...] = jnp.zeros(c_ref.shape, F32)
        acc_ref[...] = jnp.zeros(acc_ref.shape, F32)
        step(new_ref, col < t_of_row)

    @pl.when(p > 0)
    def _():
        step(page_ref, col >= 0)

    @pl.when(p == pl.num_programs(1) - 1)
    def _():
        for h in range(N_HEADS_A):
            o_ref[:, h * HEAD_DIM:(h + 1) * HEAD_DIM] = acc_ref[h * n_new:(h + 1) * n_new,
                                                                h * HEAD_DIM:(h + 1) * HEAD_DIM]


def _sb_decode(qa, kva_new_pad, cache_a, page_table):
    bsz, n_new, _ = qa.shape
    n_pages = page_table.shape[1]
    page = cache_a.shape[1]
    rows = N_HEADS_A * n_new
    grid_spec = pltpu.PrefetchScalarGridSpec(
        num_scalar_prefetch=1,
        grid=(bsz, n_pages + 1),
        in_specs=[pl.BlockSpec((None, n_new, D_A), lambda b, p, pt: (b, 0, 0)),
                  pl.BlockSpec((None, page, 2 * D_A), lambda b, p, pt: (b, 0, 0)),
                  pl.BlockSpec((None, page, 2 * D_A),
                               lambda b, p, pt: (pt[b, jnp.minimum(n_pages - p, n_pages - 1)], 0, 0))],
        out_specs=pl.BlockSpec((None, n_new, D_A), lambda b, p, pt: (b, 0, 0)),
        scratch_shapes=[pltpu.VMEM((rows, 1), F32), pltpu.VMEM((rows, D_A), F32)],
    )
    return pl.pallas_call(
        functools.partial(_sb_decode_kernel, n_new=n_new, page=page),
        grid_spec=grid_spec,
        out_shape=jax.ShapeDtypeStruct((bsz, n_new, D_A), F32),
        compiler_params=_params("parallel", "arbitrary"),
    )(page_table, qa, kva_new_pad, cache_a)


def _compress_partial(load_rows, pe_ref, w1_ref, nchunk):
    half = CMP_LEN // 2
    a = jnp.zeros((nchunk, HEAD_DIM), F32)
    b = jnp.zeros((nchunk, HEAD_DIM), F32)
    for l in range(half):
        rows = load_rows(l)
        a = a + _dot((rows + pe_ref[l:l + 1, :]).astype(BF16), w1_ref[l])
        b = b + _dot((rows + pe_ref[half + l:half + l + 1, :]).astype(BF16), w1_ref[half + l])
    return a, b


def _compress_finish(a, b, w2):
    n = a.shape[0]
    pre = a + pltpu.roll(b, n - 1, 0)
    return _dot(jax.nn.gelu(pre).astype(BF16), w2)


def _compress_prompt_kernel(r_ref, pe_ref, w1_ref, w2_ref, o_ref, *, nchunk):
    load = lambda l: r_ref[pl.ds(l, nchunk, stride=CMP_STRIDE), :]
    a, b = _compress_partial(load, pe_ref, w1_ref, nchunk)
    o_ref[...] = _compress_finish(a, b, w2_ref[...])


def _compress_prompt(nsa, pe, w1, w2, bsz, t):
    nchunk = t // CMP_STRIDE
    return pl.pallas_call(
        functools.partial(_compress_prompt_kernel, nchunk=nchunk),
        grid=(bsz, 4),
        in_specs=[pl.BlockSpec((t, HEAD_DIM), lambda b, s: (b, s)),
                  pl.BlockSpec((None, CMP_LEN, HEAD_DIM), lambda b, s: (s // 2, 0, 0)),
                  pl.BlockSpec((None, CMP_LEN, HEAD_DIM, HEAD_DIM), lambda b, s: (s // 2, 0, 0, 0)),
                  pl.BlockSpec((None, HEAD_DIM, HEAD_DIM), lambda b, s: (s // 2, 0, 0))],
        out_specs=pl.BlockSpec((None, None, nchunk, HEAD_DIM), lambda b, s: (b, s, 0, 0)),
        out_shape=jax.ShapeDtypeStruct((bsz, 4, nchunk, HEAD_DIM), F32),
        compiler_params=_params("parallel", "parallel"),
    )(nsa, pe, w1, w2)


PAGES_PER_STEP = 16


def _compress_pages_kernel(pt_ref, pe_ref, w1_ref, *refs, chunks_per_page):
    pages = refs[:PAGES_PER_STEP]
    a_ref, b_ref = refs[PAGES_PER_STEP:]
    nchunk = PAGES_PER_STEP * chunks_per_page
    load = lambda l: jnp.concatenate(
        [pg[pl.ds(l, chunks_per_page, stride=CMP_STRIDE), :] for pg in pages], axis=0)
    a, b = _compress_partial(load, pe_ref, w1_ref, nchunk)
    a_ref[...] = a
    b_ref[...] = b


def _compress_pages(cache_kv, page_table, pe, w1):
    bsz, n_pages = page_table.shape
    page = cache_kv.shape[1]
    cpp = page // CMP_STRIDE
    nchunk = PAGES_PER_STEP * cpp
    steps = n_pages // PAGES_PER_STEP
    page_specs = [
        pl.BlockSpec((None, page, HEAD_DIM),
                     lambda b, s, st, pt, k=k: (pt[b, s * PAGES_PER_STEP + k], 0, st))
        for k in range(PAGES_PER_STEP)]
    out_spec = pl.BlockSpec((None, None, nchunk, HEAD_DIM), lambda b, s, st, pt: (b, st, s, 0))
    grid_spec = pltpu.PrefetchScalarGridSpec(
        num_scalar_prefetch=1,
        grid=(bsz, steps, 4),
        in_specs=[pl.BlockSpec((None, CMP_LEN, HEAD_DIM), lambda b, s, st, pt: (st // 2, 0, 0)),
                  pl.BlockSpec((None, CMP_LEN, HEAD_DIM, HEAD_DIM),
                               lambda b, s, st, pt: (st // 2, 0, 0, 0))]
                 + page_specs,
        out_specs=[out_spec, out_spec],
    )
    shape = jax.ShapeDtypeStruct((bsz, 4, n_pages * cpp, HEAD_DIM), F32)
    return pl.pallas_call(
        functools.partial(_compress_pages_kernel, chunks_per_page=cpp),
        grid_spec=grid_spec,
        out_shape=[shape, shape],
        compiler_params=_params("parallel", "parallel", "parallel"),
    )(page_table, pe, w1, *([cache_kv] * PAGES_PER_STEP))


def _cmp_branch(qs, kcmp, vcmp, pos_rows):
    n = kcmp.shape[0]
    s = _dot_nt(qs, kcmp.astype(BF16)) * SCALE
    cmp_end = lax.broadcasted_iota(jnp.int32, (1, n), 1) * CMP_STRIDE + (CMP_LEN - 1)
    mask = cmp_end <= pos_rows
    s = jnp.where(mask, s, NEG_BIG)
    m = jnp.max(s, axis=-1, keepdims=True)
    e = jnp.where(mask, jnp.exp(s - m), 0.0)
    p = e / jnp.maximum(jnp.sum(e, axis=-1, keepdims=True), 1e-30)
    return _dot(p.astype(BF16), vcmp.astype(BF16)), p


def _select_blocks(imp, pos_rows, n_sel):
    r, width = imp.shape
    lane = lax.broadcasted_iota(jnp.int32, (r, width), 1)
    qblk = pos_rows // SLC_BLOCK
    forced = (lane == 0) | (lane == qblk) | (lane == qblk - 1)
    valid = lane * SLC_BLOCK <= pos_rows
    imp = jnp.where(forced, FORCE_SCORE, jnp.where(valid, imp, -FORCE_SCORE))
    imp = jnp.where(lane < n_sel, imp, -jnp.inf)
    rank = jnp.zeros((r, width), F32)
    for i in range(n_sel):
        ci = imp[:, i:i + 1]
        better = (ci > imp) | ((ci == imp) & (lane > i))
        rank = rank + jnp.where(better, 1.0, 0.0)
    return jnp.where((rank < float(min(N_SELECT, n_sel))) & (lane < n_sel), 1.0, 0.0)


def _softmax_step(s, mask, v, carry):
    m, l, acc = carry
    s = jnp.where(mask, s, NEG_BIG)
    m_new = jnp.maximum(m, jnp.max(s, axis=-1, keepdims=True))
    alpha = jnp.exp(m - m_new)
    e = jnp.where(mask, jnp.exp(s - m_new), 0.0)
    l = alpha * l + jnp.sum(e, axis=-1, keepdims=True)
    acc = alpha * acc + _dot(e.astype(BF16), v)
    return m_new, l, acc


def _softmax_init(r):
    return (jnp.full((r, 1), NEG_BIG, F32), jnp.zeros((r, 1), F32), jnp.zeros((r, HEAD_DIM), F32))


def _softmax_finish(carry):
    _, l, acc = carry
    return acc / jnp.maximum(l, 1e-30)


def _nsa_prompt_kernel(q_ref, g_ref, kc_ref, vc_ref, ks_ref, vs_ref, kw_ref, vw_ref,
                       m_ref, e_ref, o_ref, *, tq, n_sel):
    i = pl.program_id(2)
    r = HPG * tq
    qs = jnp.concatenate([q_ref[:, h * HEAD_DIM:(h + 1) * HEAD_DIM] for h in range(HPG)],
                         axis=0).astype(BF16)
    pos_t = i * tq + lax.broadcasted_iota(jnp.int32, (tq, 1), 0)
    pos_r = jnp.concatenate([pos_t] * HPG, axis=0)

    o_c, p = _cmp_branch(qs, kc_ref[...], vc_ref[...], pos_r)
    psum = p[0:tq]
    for h in range(1, HPG):
        psum = psum + p[h * tq:(h + 1) * tq]
    sel = _select_blocks(_dot_split(psum, m_ref[...]), pos_t, n_sel).astype(BF16)

    lane = lax.broadcasted_iota(jnp.int32, (r, tq), 1)

    def slc_body(c, carry):
        off = pl.multiple_of(c * tq, tq)
        k = ks_ref[pl.ds(off, tq), :].astype(BF16)
        v = vs_ref[pl.ds(off, tq), :].astype(BF16)
        hit = _dot(sel, e_ref[:, pl.ds(off, tq)])
        mask = (jnp.concatenate([hit] * HPG, axis=0) > 0.5) & (off + lane <= pos_r)
        return _softmax_step(_dot_nt(qs, k) * SCALE, mask, v, carry)

    o_s = _softmax_finish(lax.fori_loop(0, i + 1, slc_body, _softmax_init(r)))

    def win_body(c, carry):
        off = pl.multiple_of(c * tq, tq)
        k = kw_ref[pl.ds(off, tq), :].astype(BF16)
        v = vw_ref[pl.ds(off, tq), :].astype(BF16)
        dlt = pos_r - (off + lane)
        mask = (dlt >= 0) & (dlt < WINDOW)
        return _softmax_step(_dot_nt(qs, k) * SCALE, mask, v, carry)

    lo = jnp.maximum(i - WINDOW // tq, 0)
    o_w = _softmax_finish(lax.fori_loop(lo, i + 1, win_body, _softmax_init(r)))

    for h in range(HPG):
        rows = slice(h * tq, (h + 1) * tq)
        gc = g_ref[:, 3 * h:3 * h + 1]
        gs = g_ref[:, 3 * h + 1:3 * h + 2]
        gw = g_ref[:, 3 * h + 2:3 * h + 3]
        o_ref[:, h * HEAD_DIM:(h + 1) * HEAD_DIM] = gc * o_c[rows] + gs * o_s[rows] + gw * o_w[rows]


def _cmp_to_slc(n_cmp_pad, n_cmp, n_sel, width):
    i = np.arange(n_cmp_pad)[:, None]
    j = np.arange(width)[None, :]
    lo = np.maximum(i * CMP_STRIDE, j * SLC_BLOCK)
    hi = np.minimum(i * CMP_STRIDE + CMP_LEN, (j + 1) * SLC_BLOCK)
    m = np.maximum(hi - lo, 0) / CMP_STRIDE
    m = np.where((i < n_cmp) & (j < n_sel), m, 0.0)
    return jnp.asarray(m, dtype=BF16)


def _nsa_prompt(qb, gates, cmp_kv, nsa, win, bsz, t):
    tq = 128
    nq = t // tq
    n_cmp = (t - CMP_LEN) // CMP_STRIDE + 1
    n_cmp_pad = cmp_kv.shape[2]
    n_sel = -(-t // SLC_BLOCK)
    m_mat = _cmp_to_slc(n_cmp_pad, n_cmp, n_sel, LANES)
    e_mat = jnp.asarray(np.arange(LANES)[:, None] == (np.arange(t) // SLC_BLOCK)[None, :], dtype=BF16)
    full = lambda shape: pl.BlockSpec(shape, lambda b, g, i: (0,) * len(shape))
    rows = lambda col: pl.BlockSpec((t, HEAD_DIM), lambda b, g, i, col=col: (b, col + g))
    return pl.pallas_call(
        functools.partial(_nsa_prompt_kernel, tq=tq, n_sel=n_sel),
        grid=(bsz, N_KV_B, nq),
        in_specs=[pl.BlockSpec((tq, HPG * HEAD_DIM), lambda b, g, i: (b * nq + i, g)),
                  pl.BlockSpec((tq, LANES), lambda b, g, i: (b * nq + i, g)),
                  pl.BlockSpec((None, None, n_cmp_pad, HEAD_DIM), lambda b, g, i: (b, g, 0, 0)),
                  pl.BlockSpec((None, None, n_cmp_pad, HEAD_DIM), lambda b, g, i: (b, 2 + g, 0, 0)),
                  rows(4), rows(6), rows(0), rows(2),
                  full(m_mat.shape), full(e_mat.shape)],
        out_specs=pl.BlockSpec((tq, HPG * HEAD_DIM), lambda b, g, i: (b * nq + i, g)),
        out_shape=jax.ShapeDtypeStruct((bsz * t, D_B), F32),
        compiler_params=_params("parallel", "parallel", "parallel"),
    )(qb, gates, cmp_kv, cmp_kv, nsa, nsa, win, win, m_mat, e_mat)


def _nsa_select_kernel(q_ref, ak_ref, av_ref, bk_ref, bv_ref, w2_ref, m_ref, oc_ref, sel_ref,
                       *, n_new, past, n_sel):
    qs = jnp.concatenate([q_ref[:, h * HEAD_DIM:(h + 1) * HEAD_DIM] for h in range(HPG)],
                         axis=0).astype(BF16)
    pos_t = past + lax.broadcasted_iota(jnp.int32, (n_new, 1), 0)
    pos_r = jnp.concatenate([pos_t] * HPG, axis=0)
    kcmp = _compress_finish(ak_ref[...], bk_ref[...], w2_ref[0])
    vcmp = _compress_finish(av_ref[...], bv_ref[...], w2_ref[1])
    o_c, p = _cmp_branch(qs, kcmp, vcmp, pos_r)
    oc_ref[...] = o_c
    psum = p[0:n_new]
    for h in range(1, HPG):
        psum = psum + p[h * n_new:(h + 1) * n_new]
    sel = _select_blocks(_dot_split(psum, m_ref[...]), pos_t, n_sel)
    sel_ref[...] = jnp.concatenate([sel] * HPG, axis=0)


def _nsa_select(qb, part_a, part_b, w2, past):
    bsz, n_new, _ = qb.shape
    n_cmp_pad = part_a.shape[2]
    total = past + n_new
    n_cmp = (total - CMP_LEN) // CMP_STRIDE + 1
    n_sel = -(-total // SLC_BLOCK)
    width = -(-n_sel // LANES) * LANES
    m_mat = _cmp_to_slc(n_cmp_pad, n_cmp, n_sel, width)
    assert n_cmp == n_cmp_pad - 1, "compressed tokens must come from the paged rows only"
    r = HPG * n_new
    k_spec = pl.BlockSpec((None, None, n_cmp_pad, HEAD_DIM), lambda b, g: (b, g, 0, 0))
    v_spec = pl.BlockSpec((None, None, n_cmp_pad, HEAD_DIM), lambda b, g: (b, N_KV_B + g, 0, 0))
    return pl.pallas_call(
        functools.partial(_nsa_select_kernel, n_new=n_new, past=past, n_sel=n_sel),
        grid=(bsz, N_KV_B),
        in_specs=[pl.BlockSpec((None, n_new, HPG * HEAD_DIM), lambda b, g: (b, 0, g)),
                  k_spec, v_spec, k_spec, v_spec,
                  pl.BlockSpec((2, HEAD_DIM, HEAD_DIM), lambda b, g: (0, 0, 0)),
                  pl.BlockSpec(m_mat.shape, lambda b, g: (0, 0))],
        out_specs=[pl.BlockSpec((None, None, r, HEAD_DIM), lambda b, g: (b, g, 0, 0)),
                   pl.BlockSpec((None, None, r, width), lambda b, g: (b, g, 0, 0))],
        out_shape=[jax.ShapeDtypeStruct((bsz, N_KV_B, r, HEAD_DIM), F32),
                   jax.ShapeDtypeStruct((bsz, N_KV_B, r, width), F32)],
        compiler_params=_params("parallel", "parallel"),
    )(qb, part_a, part_a, part_b, part_b, w2, m_mat)


def _nsa_slc_decode_kernel(pt_ref, q_ref, sel_ref, new_ref, page_ref, o_ref, m_ref, l_ref, acc_ref,
                           *, n_new, past, page):
    p = pl.program_id(1)
    n_pages = pl.num_programs(1) - 1
    r = HPG * n_new
    t_of_row = lax.broadcasted_iota(jnp.int32, (r, 1), 0) % n_new
    lane = lax.broadcasted_iota(jnp.int32, (r, page), 1)
    bpp = page // SLC_BLOCK

    @pl.when(p == 0)
    def _():
        m_ref[...] = jnp.full(m_ref.shape, NEG_BIG, F32)
        l_ref[...] = jnp.zeros(l_ref.shape, F32)
        acc_ref[...] = jnp.zeros(acc_ref.shape, F32)

    def step(kv_ref, first_pos):
        first_blk = first_pos // SLC_BLOCK
        for g in range(N_KV_B):
            qs = jnp.concatenate(
                [q_ref[:, (g * HPG + h) * HEAD_DIM:(g * HPG + h + 1) * HEAD_DIM] for h in range(HPG)],
                axis=0).astype(BF16)
            sel = sel_ref[g]
            blk_lane = lax.broadcasted_iota(jnp.int32, sel.shape, 1)
            hit = jnp.zeros((r, page), F32)
            for j in range(bpp):
                flag = jnp.sum(jnp.where(blk_lane == first_blk + j, sel, 0.0), axis=-1, keepdims=True)
                hit = jnp.where(lane // SLC_BLOCK == j, flag, hit)
            mask = (hit > 0.5) & (first_pos + lane <= past + t_of_row)
            k = kv_ref[:, g * HEAD_DIM:(g + 1) * HEAD_DIM].astype(BF16)
            v = kv_ref[:, (N_KV_B + g) * HEAD_DIM:(N_KV_B + g + 1) * HEAD_DIM].astype(BF16)
            carry = _softmax_step(_dot_nt(qs, k) * SCALE, mask, v, (m_ref[g], l_ref[g], acc_ref[g]))
            m_ref[g], l_ref[g], acc_ref[g] = carry

    @pl.when(p < n_pages)
    def _():
        step(page_ref, p * page)

    @pl.when(p == n_pages)
    def _():
        step(new_ref, past)
        for g in range(N_KV_B):
            o_ref[g] = _softmax_finish((m_ref[g], l_ref[g], acc_ref[g]))


def _nsa_slc_decode(qb, sel, slc_new_pad, cache_slc, page_table, past):
    bsz, n_new, _ = qb.shape
    n_pages = page_table.shape[1]
    page = cache_slc.shape[1]
    r = HPG * n_new
    width = sel.shape[-1]
    kvw = 2 * D_KV_B
    grid_spec = pltpu.PrefetchScalarGridSpec(
        num_scalar_prefetch=1,
        grid=(bsz, n_pages + 1),
        in_specs=[pl.BlockSpec((None, n_new, D_B), lambda b, p, pt: (b, 0, 0)),
                  pl.BlockSpec((None, N_KV_B, r, width), lambda b, p, pt: (b, 0, 0, 0)),
                  pl.BlockSpec((None, page, kvw), lambda b, p, pt: (b, 0, 0)),
                  pl.BlockSpec((None, page, kvw),
                               lambda b, p, pt: (pt[b, jnp.minimum(p, n_pages - 1)], 0, 1))],
        out_specs=pl.BlockSpec((None, N_KV_B, r, HEAD_DIM), lambda b, p, pt: (b, 0, 0, 0)),
        scratch_shapes=[pltpu.VMEM((N_KV_B, r, 1), F32), pltpu.VMEM((N_KV_B, r, 1), F32),
                        pltpu.VMEM((N_KV_B, r, HEAD_DIM), F32)],
    )
    return pl.pallas_call(
        functools.partial(_nsa_slc_decode_kernel, n_new=n_new, past=past, page=page),
        grid_spec=grid_spec,
        out_shape=jax.ShapeDtypeStruct((bsz, N_KV_B, r, HEAD_DIM), F32),
        compiler_params=_params("parallel", "arbitrary"),
    )(page_table, qb, sel, slc_new_pad, cache_slc)


def _nsa_merge_decode_kernel(q_ref, g_ref, oc_ref, os_ref, st_ref, new_ref, o_ref, nw_ref,
                             *, n_new, past, n_buf):
    r = HPG * n_new
    n_pad = new_ref.shape[0]
    t_of_row = lax.broadcasted_iota(jnp.int32, (r, 1), 0) % n_new
    pos_r = past + t_of_row
    buf_pos = past - n_buf + lax.broadcasted_iota(jnp.int32, (r, n_buf), 1)
    new_pos = past + lax.broadcasted_iota(jnp.int32, (r, n_pad), 1)

    def in_window(kpos):
        dlt = pos_r - kpos
        return (dlt >= 0) & (dlt < WINDOW) & (kpos >= 0)

    for g in range(N_KV_B):
        qs = jnp.concatenate(
            [q_ref[:, (g * HPG + h) * HEAD_DIM:(g * HPG + h + 1) * HEAD_DIM] for h in range(HPG)],
            axis=0).astype(BF16)
        kcol = slice(g * HEAD_DIM, (g + 1) * HEAD_DIM)
        vcol = slice((N_KV_B + g) * HEAD_DIM, (N_KV_B + g + 1) * HEAD_DIM)
        carry = _softmax_init(r)
        carry = _softmax_step(_dot_nt(qs, st_ref[:, kcol].astype(BF16)) * SCALE, in_window(buf_pos),
                              st_ref[:, vcol].astype(BF16), carry)
        carry = _softmax_step(_dot_nt(qs, new_ref[:, kcol].astype(BF16)) * SCALE, in_window(new_pos),
                              new_ref[:, vcol].astype(BF16), carry)
        o_w = _softmax_finish(carry)
        for h in range(HPG):
            rows = slice(h * n_new, (h + 1) * n_new)
            c0 = g * LANES + 3 * h
            out = (g_ref[:, c0:c0 + 1] * oc_ref[g, rows] + g_ref[:, c0 + 1:c0 + 2] * os_ref[g, rows]
                   + g_ref[:, c0 + 2:c0 + 3] * o_w[rows])
            o_ref[:, (g * HPG + h) * HEAD_DIM:(g * HPG + h + 1) * HEAD_DIM] = out
    shifted = pltpu.roll(st_ref[...], n_buf - n_new, 0)
    placed = pltpu.roll(new_ref[...], n_pad - n_new, 0)
    tail_row = lax.broadcasted_iota(jnp.int32, (n_pad, placed.shape[1]), 0)
    nw_ref[0:n_buf - n_pad, :] = shifted[0:n_buf - n_pad]
    nw_ref[n_buf - n_pad:n_buf, :] = jnp.where(tail_row >= n_pad - n_new, placed, shifted[n_buf - n_pad:])


def _nsa_merge_decode(qb, gates, o_cmp, o_slc, state_win, win_new, past):
    bsz, n_new, _ = qb.shape
    n_buf = state_win.shape[1]
    n_pad = win_new.shape[1]
    keep = min(WINDOW, past + n_new)
    assert keep == n_buf and n_buf > n_pad >= n_new
    r = HPG * n_new
    cols = 2 * D_KV_B
    per_b = lambda shape: pl.BlockSpec((None,) + shape, lambda b: (b,) + (0,) * len(shape))
    return pl.pallas_call(
        functools.partial(_nsa_merge_decode_kernel, n_new=n_new, past=past, n_buf=n_buf),
        grid=(bsz,),
        in_specs=[per_b((n_new, D_B)), per_b((n_new, 2 * LANES)),
                  per_b((N_KV_B, r, HEAD_DIM)), per_b((N_KV_B, r, HEAD_DIM)),
                  per_b((n_buf, cols)), per_b((n_pad, cols))],
        out_specs=[per_b((n_new, D_B)), per_b((keep, cols))],
        out_shape=[jax.ShapeDtypeStruct((bsz, n_new, D_B), F32),
                   jax.ShapeDtypeStruct((bsz, keep, cols), F32)],
        compiler_params=_params("parallel"),
    )(qb, gates, o_cmp, o_slc, state_win, win_new)


def _outproj_kernel(x_ref, oa_ref, ob_ref, wa_ref, wb_ref, g_ref, b_ref, h_ref):
    y = _dot(oa_ref[...].astype(BF16), wa_ref[...]) + _dot(ob_ref[...].astype(BF16), wb_ref[...])
    h_ref[...] = _layer_norm(ALPHA * x_ref[...] + y, g_ref[...], b_ref[...])


def _outproj(x, o_a, o_b, wa, wb, g, b):
    m, d = x.shape
    tm = min(256, m)
    row = lambda w: pl.BlockSpec((tm, w), lambda i: (i, 0))
    full = lambda a: pl.BlockSpec(a.shape, lambda i: (0, 0))
    return pl.pallas_call(
        _outproj_kernel,
        grid=(m // tm,),
        in_specs=[row(d), row(D_A), row(D_B), full(wa), full(wb), full(g), full(b)],
        out_specs=row(d),
        out_shape=jax.ShapeDtypeStruct((m, d), F32),
        compiler_params=_params("parallel"),
    )(x, o_a, o_b, wa, wb, g, b)


def _peer_query_kernel(h_ref, w_ref, q_ref):
    q_ref[...] = _dot(h_ref[...].astype(BF16), w_ref[...])


def _peer_query(h, wq):
    m, d = h.shape
    n = wq.shape[1]
    tm = min(256, m)
    return pl.pallas_call(
        _peer_query_kernel,
        grid=(m // tm,),
        in_specs=[pl.BlockSpec((tm, d), lambda i: (i, 0)), pl.BlockSpec((d, n), lambda i: (0, 0))],
        out_specs=pl.BlockSpec((tm, n), lambda i: (i, 0)),
        out_shape=jax.ShapeDtypeStruct((m, n), F32),
        compiler_params=_params("parallel"),
    )(h, wq)


def _top_values(x, k):
    n = x.shape[0]
    row = lax.broadcasted_iota(jnp.int32, x.shape, 0).astype(F32)
    out = []
    for _ in range(k):
        m = jnp.max(x, axis=0, keepdims=True)
        first = jnp.min(jnp.where(x == m, row, float(n)), axis=0, keepdims=True)
        x = jnp.where(row == first, -jnp.inf, x)
        out.append(m)
    return out


_PEER_PAIRS = [(i, j) for i in range(PEER_TOPK) for j in range(PEER_TOPK) if (i + 1) * (j + 1) <= PEER_TOPK]
_PEER_CAND_ROWS = -(-len(_PEER_PAIRS) // 8) * 8


def _peer_route_kernel(q_ref, sk_ref, s1_ref, s2_ref, e1_ref, e2_ref, tau_ref, cand_ref):
    half = sk_ref.shape[2]
    cand_ref[...] = jnp.full(cand_ref.shape, -jnp.inf, F32)
    for h in range(PEER_HEADS):
        s, tops = [], []
        for c in range(2):
            col = (2 * h + c) * half
            st = _dot_nt(sk_ref[c], q_ref[:, col:col + half].astype(BF16))
            s.append(st)
            tops.append(_top_values(st, PEER_TOPK))
        for n, (i, j) in enumerate(_PEER_PAIRS):
            cand_ref[n:n + 1, :] = tops[0][i] + tops[1][j]
        best = _top_values(cand_ref[...], PEER_TOPK)
        top = best[0]
        z = jnp.ones_like(top)
        for v in best[1:]:
            z = z + jnp.exp(v - top)
        s1_ref[h] = s[0]
        s2_ref[h] = s[1]
        e1_ref[h] = jnp.exp(s[0] - tops[0][0])
        e2_ref[h] = jnp.exp(s[1] - tops[1][0]) / z
        tau_ref[h:h + 1, :] = best[PEER_TOPK - 1]


def _peer_route(q, sub_keys):
    n = q.shape[0]
    tm = 128
    nk = sub_keys.shape[1]
    big = jax.ShapeDtypeStruct((PEER_HEADS, nk, n), F32)
    big_spec = pl.BlockSpec((PEER_HEADS, nk, tm), lambda i: (0, 0, i))
    return pl.pallas_call(
        _peer_route_kernel,
        grid=(n // tm,),
        in_specs=[pl.BlockSpec((tm, q.shape[1]), lambda i: (i, 0)),
                  pl.BlockSpec(sub_keys.shape, lambda i: (0, 0, 0))],
        out_specs=[big_spec] * 4 + [pl.BlockSpec((PEER_HEADS, tm), lambda i: (0, i))],
        out_shape=[big] * 4 + [jax.ShapeDtypeStruct((PEER_HEADS, n), F32)],
        scratch_shapes=[pltpu.VMEM((_PEER_CAND_ROWS, tm), F32)],
        compiler_params=_params("parallel"),
    )(q, sub_keys)


def _peer_dense_kernel(h_ref, u_ref, v_ref, s1_ref, s2_ref, e1_ref, e2_ref, tau_ref, g_ref, b_ref,
                       o_ref, hb_ref, wt_ref, *, tm, ac):
    c = pl.program_id(1)
    nk = s2_ref.shape[1]

    @pl.when(c == 0)
    def _():
        o_ref[...] = jnp.zeros(o_ref.shape, F32)
        hb_ref[...] = h_ref[...].astype(BF16)

    act = jax.nn.gelu(_dot_nt(u_ref[...], hb_ref[...]))
    for aa in range(ac):
        for ts in range(tm // LANES):
            tok = slice(ts * LANES, (ts + 1) * LANES)
            gate = jnp.zeros((nk, LANES), F32)
            for h in range(PEER_HEADS):
                score = s1_ref[h, aa:aa + 1, tok] + s2_ref[h, :, tok]
                chosen = score >= tau_ref[h:h + 1, tok]
                gate = gate + jnp.where(chosen, e1_ref[h, aa:aa + 1, tok] * e2_ref[h, :, tok], 0.0)
            wt_ref[aa * nk:(aa + 1) * nk, tok] = (gate * act[aa * nk:(aa + 1) * nk, tok]).astype(BF16)
    o_ref[...] += _dot_tn(wt_ref[...], v_ref[...])

    @pl.when(c == pl.num_programs(1) - 1)
    def _():
        o_ref[...] = _layer_norm(ALPHA * h_ref[...] + o_ref[...], g_ref[...], b_ref[...])


def _peer_dense(h, u, v, route, g, b):
    n, d = h.shape
    s1, s2, e1, e2, tau = route
    nk = s1.shape[1]
    tm = min(512, n)
    ac = 8
    big_spec = pl.BlockSpec((PEER_HEADS, nk, tm), lambda i, c: (0, 0, i))
    row_spec = pl.BlockSpec((PEER_HEADS, ac, tm), lambda i, c: (0, c, i))
    tab_spec = pl.BlockSpec((ac * nk, d), lambda i, c: (c, 0))
    full = lambda a: pl.BlockSpec(a.shape, lambda i, c: (0, 0))
    return pl.pallas_call(
        functools.partial(_peer_dense_kernel, tm=tm, ac=ac),
        grid=(n // tm, nk // ac),
        in_specs=[pl.BlockSpec((tm, d), lambda i, c: (i, 0)), tab_spec, tab_spec,
                  row_spec, big_spec, row_spec, big_spec,
                  pl.BlockSpec((PEER_HEADS, tm), lambda i, c: (0, i)), full(g), full(b)],
        out_specs=pl.BlockSpec((tm, d), lambda i, c: (i, 0)),
        out_shape=jax.ShapeDtypeStruct((n, d), F32),
        scratch_shapes=[pltpu.VMEM((tm, d), BF16), pltpu.VMEM((ac * nk, tm), BF16)],
        compiler_params=_params("parallel", "arbitrary"),
    )(h, u, v, s1, s2, e1, e2, tau, g, b)


def _post_block(x, o_a, o_b, wts):
    h = _outproj(x, o_a, o_b, wts["wo_a"], wts["wo_b"], wts["ln1_g"], wts["ln1_b"])
    q = _peer_query(h, wts["wq"])
    route = _peer_route(q, wts["sub_keys"])
    return _peer_dense(h, wts["u"], wts["v"], route, wts["ln2_g"], wts["ln2_b"])


def _pad_rows(a, rows):
    return jnp.pad(a, ((0, rows - a.shape[0]),) + ((0, 0),) * (a.ndim - 1))


def kernel(x_prompt, x_sample, cache_a, cache_nsa, state_win, page_table, w_in, cmp_pe_k, cmp_w1_k,
           cmp_w2_k, cmp_pe_v, cmp_w1_v, cmp_w2_v, w_out, ln1_g, ln1_b, peer_w_query, peer_sub_keys,
           peer_u, peer_v, ln2_g, ln2_b):
    assert w_in.shape[0] == DEPTH == 1
    bsz, t, d = x_prompt.shape
    dbsz, n_new, _ = x_sample.shape
    n_pool, page = cache_a.shape[1], cache_a.shape[2]
    past = page_table.shape[1] * page

    w = w_in[0]
    o = np.cumsum((0, D_A, 2 * D_A, D_B, 4 * D_KV_B, 2 * D_KV_B, 3 * N_HEADS_B))
    wg = w[:, o[5]:o[6]]
    per_g = 3 * HPG
    gate_w = jnp.concatenate(
        [jnp.pad(wg[:, g * per_g:(g + 1) * per_g], ((0, 0), (0, LANES - per_g))) for g in range(N_KV_B)],
        axis=1)
    wparts = {"qa": w[:, o[0]:o[1]], "kva": w[:, o[1]:o[2]], "qb": w[:, o[2]:o[3]],
              "nsa": w[:, o[3]:o[4]], "win": w[:, o[4]:o[5]], "gates": gate_w}
    wparts = {k: v.astype(BF16) for k, v in wparts.items()}
    row2 = lambda a: a[0].reshape(1, -1)
    wts = {"wo_a": w_out[0, :D_A].astype(BF16), "wo_b": w_out[0, D_A:].astype(BF16),
           "ln1_g": row2(ln1_g), "ln1_b": row2(ln1_b), "ln2_g": row2(ln2_g), "ln2_b": row2(ln2_b),
           "wq": peer_w_query[0].astype(BF16), "sub_keys": peer_sub_keys[0].astype(BF16),
           "u": peer_u[0].astype(BF16), "v": peer_v[0].astype(BF16)}
    pe = jnp.stack([cmp_pe_k[0], cmp_pe_v[0]])
    w1 = jnp.stack([cmp_w1_k[0], cmp_w1_v[0]]).astype(BF16)
    w2 = jnp.stack([cmp_w2_k[0], cmp_w2_v[0]]).astype(BF16)

    xp = x_prompt.reshape(bsz * t, d)
    tabs_p = _rope_tables(jnp.arange(t, dtype=jnp.int32))
    qa, kva, qb, nsa, win, gates = _project_all(xp.astype(BF16), wparts, tabs_p)
    o_a = _sb_prompt(qa, kva, bsz, t)
    cmp_kv = _compress_prompt(nsa, pe, w1, w2, bsz, t)
    o_b = _nsa_prompt(qb, gates, cmp_kv, nsa, win, bsz, t)
    y_prompt = _post_block(xp, o_a, o_b, wts).reshape(bsz, t, d)
    new_a_p = kva.reshape(1, bsz, t, 2, N_HEADS_A, HEAD_DIM)
    new_nsa_p = nsa.reshape(1, bsz, t, 4, N_KV_B, HEAD_DIM)
    keep_p = min(WINDOW, t)
    new_win_p = win.reshape(bsz, t, 2, N_KV_B, HEAD_DIM)[None, :, t - keep_p:]

    ns = dbsz * n_new
    xs = x_sample.reshape(ns, d)
    pos_s = past + jnp.arange(n_new, dtype=jnp.int32)
    tabs_s = tuple(jnp.tile(tb, (dbsz, 1)) for tb in _rope_tables(pos_s))
    qa_s, kva_s, qb_s, nsa_s, win_s, gates_s = _project_all(xs.astype(BF16), wparts, tabs_s)
    by_b = lambda a: a.reshape(dbsz, n_new, a.shape[-1])
    pad_page = lambda a: jnp.pad(by_b(a), ((0, 0), (0, page - n_new), (0, 0)))
    o_a_s = _sb_decode(by_b(qa_s), pad_page(kva_s), cache_a[0].reshape(n_pool, page, 2 * D_A), page_table)
    cache_n = cache_nsa[0].reshape(n_pool, page, 4 * D_KV_B)
    part_a, part_b = _compress_pages(cache_n, page_table, pe, w1)
    o_cmp, sel = _nsa_select(by_b(qb_s), part_a, part_b, w2, past)
    o_slc = _nsa_slc_decode(by_b(qb_s), sel, pad_page(nsa_s[:, 2 * D_KV_B:]), cache_n, page_table, past)
    n_buf = state_win.shape[2]
    o_b_s, new_win = _nsa_merge_decode(by_b(qb_s), by_b(gates_s), o_cmp, o_slc,
                                       state_win[0].reshape(dbsz, n_buf, 2 * D_KV_B), pad_page(win_s), past)
    rows_s = -(-ns // LANES) * LANES
    y_s = _post_block(_pad_rows(xs, rows_s), _pad_rows(o_a_s.reshape(ns, D_A), rows_s),
                      _pad_rows(o_b_s.reshape(ns, D_B), rows_s), wts)
    y_sample = y_s[:ns].reshape(dbsz, n_new, d)
    new_a_s = kva_s.reshape(1, dbsz, n_new, 2, N_HEADS_A, HEAD_DIM)
    new_nsa_s = nsa_s.reshape(1, dbsz, n_new, 4, N_KV_B, HEAD_DIM)
    new_win_s = new_win.reshape(1, dbsz, new_win.shape[1], 2, N_KV_B, HEAD_DIM)
    return (y_prompt, y_sample, new_a_p, new_nsa_p, new_win_p, new_a_s, new_nsa_s, new_win_s)
```

```python
import functools
import math

import numpy as np
import jax
import jax.numpy as jnp
from jax import lax
from jax.experimental import pallas as pl
from jax.experimental.pallas import tpu as pltpu

F32 = jnp.float32
BF16 = jnp.bfloat16

LANES = 128
VMEM_LIMIT = 56 * 1024 * 1024

HEAD_DIM = 128
N_HEADS_A = 8
N_HEADS_B = 8
N_KV_B = 2
HPG = N_HEADS_B // N_KV_B
D_A = N_HEADS_A * HEAD_DIM
D_B = N_HEADS_B * HEAD_DIM
D_KV_B = N_KV_B * HEAD_DIM
ROPE_DIM = HEAD_DIM // 4
ROPE_THETA = 500000.0
CMP_LEN = 32
CMP_STRIDE = 16
SLC_BLOCK = 64
N_SELECT = 16
WINDOW = 512
FORCE_SCORE = 1.0e4
PEER_HEADS = 8
PEER_NKEYS = 128
PEER_TOPK = 16
DEPTH = 1
ALPHA = (2 * DEPTH) ** 0.25
LN_EPS = 1e-5
NEG_BIG = -1e30
SCALE = 1.0 / math.sqrt(HEAD_DIM)


def _dot(a, b):
    return jnp.dot(a, b, preferred_element_type=F32)


def _dot_nt(a, b):
    return lax.dot_general(a, b, (((1,), (1,)), ((), ())), preferred_element_type=F32)


def _dot_tn(a, b):
    return lax.dot_general(a, b, (((0,), (0,)), ((), ())), preferred_element_type=F32)


def _dot_split(x, w):
    hi = x.astype(BF16)
    lo = (x - hi.astype(F32)).astype(BF16)
    return _dot(hi, w) + _dot(lo, w)


def _params(*sem):
    return pltpu.CompilerParams(dimension_semantics=sem, vmem_limit_bytes=VMEM_LIMIT)


def _layer_norm(r, g, b):
    mu = jnp.mean(r, axis=-1, keepdims=True)
    d = r - mu
    var = jnp.mean(d * d, axis=-1, keepdims=True)
    return d * lax.rsqrt(var + LN_EPS) * g + b


def _inproj_kernel(x_ref, w_ref, c_ref, sa_ref, sb_ref, o_ref, *, rope_flags, sigmoid):
    acc = _dot(x_ref[...], w_ref[...])
    for j, flag in enumerate(rope_flags):
        blk = acc[:, j * LANES:(j + 1) * LANES]
        if flag:
            blk = (blk * c_ref[...]
                   + pltpu.roll(blk, LANES - ROPE_DIM // 2, 1) * sa_ref[...]
                   + pltpu.roll(blk, ROPE_DIM // 2, 1) * sb_ref[...])
        if sigmoid:
            blk = jax.nn.sigmoid(blk)
        o_ref[:, j * LANES:(j + 1) * LANES] = blk


def _inproj(xb, w, tabs, rope_flags, sigmoid=False):
    m, k = xb.shape
    n = w.shape[1]
    c, sa, sb = tabs
    tm = min(256, m)
    nt = c.shape[0] // tm
    tab_spec = pl.BlockSpec((tm, LANES), lambda i: (i % nt, 0))
    return pl.pallas_call(
        functools.partial(_inproj_kernel, rope_flags=tuple(rope_flags), sigmoid=sigmoid),
        grid=(m // tm,),
        in_specs=[pl.BlockSpec((tm, k), lambda i: (i, 0)),
                  pl.BlockSpec((k, n), lambda i: (0, 0)),
                  tab_spec, tab_spec, tab_spec],
        out_specs=pl.BlockSpec((tm, n), lambda i: (i, 0)),
        out_shape=jax.ShapeDtypeStruct((m, n), F32),
        compiler_params=_params("parallel"),
        name="inproj",
    )(xb, w, c, sa, sb)


def _rope_tables(pos):
    half = ROPE_DIM // 2
    inv = ROPE_THETA ** (-jnp.arange(half, dtype=F32) / half)
    ang = pos.astype(F32)[:, None] * inv[None, :]
    cos, sin = jnp.cos(ang), jnp.sin(ang)
    t = pos.shape[0]
    ones = jnp.ones((t, LANES - ROPE_DIM), F32)
    zeros = jnp.zeros((t, LANES - half), F32)
    c = jnp.concatenate([cos, cos, ones], axis=1)
    sa = jnp.concatenate([-sin, zeros], axis=1)
    sb = jnp.concatenate([jnp.zeros((t, half), F32), sin, jnp.zeros((t, LANES - ROPE_DIM), F32)], axis=1)
    return c, sa, sb


def _project_all(xb, wparts, tabs):
    qa = _inproj(xb, wparts["qa"], tabs, [0] * 8)
    kva = _inproj(xb, wparts["kva"], tabs, [0] * 16)
    qb = _inproj(xb, wparts["qb"], tabs, [1] * 8)
    nsa = _inproj(xb, wparts["nsa"], tabs, [1, 1, 0, 0, 1, 1, 0, 0])
    win = _inproj(xb, wparts["win"], tabs, [1, 1, 0, 0])
    gates = _inproj(xb, wparts["gates"], tabs, [0, 0], sigmoid=True)
    return qa, kva, qb, nsa, win, gates


def _sb_block(q, k, v, mask, c, tri):
    ls, lr = _sb_logs(_dot_nt(q, k), mask)
    return _sb_weights(ls, lr, mask, c, tri), c + jnp.sum(lr, axis=1, keepdims=True)


def _sb_logs(qk, mask):
    z = qk * SCALE
    ls = jnp.minimum(z, 0.0) - jnp.log1p(jnp.exp(-jnp.abs(z)))
    lr = ls - z
    return ls, (lr if mask is None else jnp.where(mask, lr, 0.0))


def _sb_weights(ls, lr, mask, c, tri):
    w = jnp.exp(ls + _dot_split(lr, tri) + c)
    return w if mask is None else jnp.where(mask, w, 0.0)


def _tri(n):
    row = lax.broadcasted_iota(jnp.int32, (n, n), 0)
    col = lax.broadcasted_iota(jnp.int32, (n, n), 1)
    return (row > col).astype(BF16)


def _sb_prompt_kernel(q_ref, k_ref, v_ref, o_ref, *, tq, tk):
    i = pl.program_id(2)
    q = q_ref[...].astype(BF16)
    tri = _tri(tk)
    per_q = tq // tk
    row = i * tq + lax.broadcasted_iota(jnp.int32, (tq, tk), 0)
    col = lax.broadcasted_iota(jnp.int32, (tq, tk), 1)

    def block(j, carry, masked):
        c, acc = carry
        off = pl.multiple_of(j * tk, tk)
        k = k_ref[pl.ds(off, tk), :].astype(BF16)
        v = v_ref[pl.ds(off, tk), :].astype(BF16)
        w, c = _sb_block(q, k, v, (off + col < row) if masked else None, c, tri)
        return c, acc + _dot(w.astype(BF16), v)

    carry = (jnp.zeros((tq, 1), F32), jnp.zeros((tq, HEAD_DIM), F32))
    for d in range(per_q):
        carry = block((i + 1) * per_q - 1 - d, carry, True)
    _, acc = lax.fori_loop(0, i * per_q, lambda s, cr: block(i * per_q - 1 - s, cr, False), carry)
    o_ref[...] = acc


def _sb_prompt(qa, kva, bsz, t):
    tq, tk = 512, 256
    nq = t // tq
    return pl.pallas_call(
        functools.partial(_sb_prompt_kernel, tq=tq, tk=tk),
        grid=(bsz, N_HEADS_A, nq),
        in_specs=[pl.BlockSpec((tq, HEAD_DIM), lambda b, h, i: (b * nq + i, h)),
                  pl.BlockSpec((t, HEAD_DIM), lambda b, h, i: (b, h)),
                  pl.BlockSpec((t, HEAD_DIM), lambda b, h, i: (b, N_HEADS_A + h))],
        out_specs=pl.BlockSpec((tq, HEAD_DIM), lambda b, h, i: (b * nq + i, h)),
        out_shape=jax.ShapeDtypeStruct((bsz * t, D_A), F32),
        compiler_params=_params("parallel", "parallel", "parallel"),
        name="sb_prompt",
    )(qa, kva, kva)


SB_PAGES_PER_STEP = 8
PAD_T = 8


def _sb_decode_kernel(pt_ref, q_ref, new_ref, *refs, page):
    pages = refs[:SB_PAGES_PER_STEP]
    o_ref, c_ref, acc_ref = refs[SB_PAGES_PER_STEP:]
    p = pl.program_id(1)
    rows = N_HEADS_A * PAD_T
    rec = 2 * N_HEADS_A
    tri = _tri(page)
    qh = [q_ref[h].astype(BF16) for h in range(N_HEADS_A)]

    def scores(ref):
        return jnp.concatenate(
            [_dot_nt(qh[h], ref[pl.ds(h, page, stride=rec), :].astype(BF16)) for h in range(N_HEADS_A)],
            axis=0)

    def values(ref, w):
        wb = w.astype(BF16)
        return [_dot(wb[h * PAD_T:(h + 1) * PAD_T],
                     ref[pl.ds(N_HEADS_A + h, page, stride=rec), :].astype(BF16)) for h in range(N_HEADS_A)]

    @pl.when(p == 0)
    def _():
        t_of_row = lax.broadcasted_iota(jnp.int32, (rows, page), 0) % PAD_T
        mask = lax.broadcasted_iota(jnp.int32, (rows, page), 1) < t_of_row
        ls, lr = _sb_logs(scores(new_ref), mask)
        w = _sb_weights(ls, lr, mask, jnp.zeros((rows, 1), F32), tri)
        acc_ref[...] = jnp.concatenate(values(new_ref, w), axis=0)
        c_ref[...] = jnp.sum(lr, axis=1, keepdims=True)

    @pl.when(p > 0)
    def _():
        logs = [_sb_logs(scores(pg), None) for pg in pages]
        c = c_ref[...]
        acc = None
        for pg, (ls, lr) in zip(pages, logs):
            part = values(pg, _sb_weights(ls, lr, None, c, tri))
            acc = part if acc is None else [x + y for x, y in zip(acc, part)]
            c = c + jnp.sum(lr, axis=1, keepdims=True)
        c_ref[...] = c
        acc_ref[...] += jnp.concatenate(acc, axis=0)

    @pl.when(p == pl.num_programs(1) - 1)
    def _():
        o_ref[...] = acc_ref[...]


def _sb_decode(q_heads, new_rows, cache_rows, page_table, page):
    bsz = q_heads.shape[0]
    n_pages = page_table.shape[1]
    pps = SB_PAGES_PER_STEP
    steps = n_pages // pps
    rows = N_HEADS_A * PAD_T
    blk = page * 2 * N_HEADS_A
    page_specs = [
        pl.BlockSpec((blk, HEAD_DIM),
                     lambda b, p, pt, k=k: (pt[b, n_pages - 1 - (jnp.maximum(p - 1, 0) * pps + k)], 0))
        for k in range(pps)]
    grid_spec = pltpu.PrefetchScalarGridSpec(
        num_scalar_prefetch=1,
        grid=(bsz, steps + 1),
        in_specs=[pl.BlockSpec((None, N_HEADS_A, PAD_T, HEAD_DIM), lambda b, p, pt: (b, 0, 0, 0)),
                  pl.BlockSpec((None, blk, HEAD_DIM), lambda b, p, pt: (b, 0, 0))] + page_specs,
        out_specs=pl.BlockSpec((None, rows, HEAD_DIM), lambda b, p, pt: (b, 0, 0)),
        scratch_shapes=[pltpu.VMEM((rows, 1), F32), pltpu.VMEM((rows, HEAD_DIM), F32)],
    )
    return pl.pallas_call(
        functools.partial(_sb_decode_kernel, page=page),
        grid_spec=grid_spec,
        out_shape=jax.ShapeDtypeStruct((bsz, rows, HEAD_DIM), F32),
        compiler_params=_params("parallel", "arbitrary"),
        name="sb_decode",
    )(page_table, q_heads, new_rows, *([cache_rows] * pps))


def _compress_partial(load_rows, pe_ref, w1_ref, nchunk):
    half = CMP_LEN // 2
    a = jnp.zeros((nchunk, HEAD_DIM), F32)
    b = jnp.zeros((nchunk, HEAD_DIM), F32)
    for l in range(half):
        rows = load_rows(l)
        a = a + _dot((rows + pe_ref[l:l + 1, :]).astype(BF16), w1_ref[l])
        b = b + _dot((rows + pe_ref[half + l:half + l + 1, :]).astype(BF16), w1_ref[half + l])
    return a, b


def _compress_finish(a, b, w2):
    n = a.shape[0]
    pre = a + pltpu.roll(b, n - 1, 0)
    return _dot(jax.nn.gelu(pre).astype(BF16), w2)


def _compress_prompt_kernel(r_ref, pe_ref, w1_ref, w2_ref, o_ref, *, nchunk):
    load = lambda l: r_ref[pl.ds(l, nchunk, stride=CMP_STRIDE), :]
    a, b = _compress_partial(load, pe_ref, w1_ref, nchunk)
    o_ref[...] = _compress_finish(a, b, w2_ref[...])


def _compress_prompt(nsa, pe, w1, w2, bsz, t):
    nchunk = t // CMP_STRIDE
    return pl.pallas_call(
        functools.partial(_compress_prompt_kernel, nchunk=nchunk),
        grid=(bsz, 4),
        in_specs=[pl.BlockSpec((t, HEAD_DIM), lambda b, s: (b, s)),
                  pl.BlockSpec((None, CMP_LEN, HEAD_DIM), lambda b, s: (s // 2, 0, 0)),
                  pl.BlockSpec((None, CMP_LEN, HEAD_DIM, HEAD_DIM), lambda b, s: (s // 2, 0, 0, 0)),
                  pl.BlockSpec((None, HEAD_DIM, HEAD_DIM), lambda b, s: (s // 2, 0, 0))],
        out_specs=pl.BlockSpec((None, None, nchunk, HEAD_DIM), lambda b, s: (b, s, 0, 0)),
        out_shape=jax.ShapeDtypeStruct((bsz, 4, nchunk, HEAD_DIM), F32),
        compiler_params=_params("parallel", "parallel"),
        name="compress_prompt",
    )(nsa, pe, w1, w2)


PAGES_PER_STEP = 16


NSA_REC = 4 * N_KV_B


def _compress_pages_kernel(pt_ref, pe_ref, w1_ref, *refs, chunks_per_page):
    pages = refs[:PAGES_PER_STEP]
    a_ref, b_ref = refs[PAGES_PER_STEP:]
    nchunk = PAGES_PER_STEP * chunks_per_page
    for s in range(2 * N_KV_B):
        load = lambda l, s=s: jnp.concatenate(
            [pg[pl.ds(l * NSA_REC + s, chunks_per_page, stride=CMP_STRIDE * NSA_REC), :] for pg in pages],
            axis=0)
        a, b = _compress_partial(load, pe_ref.at[s // N_KV_B], w1_ref.at[s // N_KV_B], nchunk)
        a_ref[s] = a
        b_ref[s] = b


def _compress_pages(cache_rows, page_table, pe, w1, page):
    bsz, n_pages = page_table.shape
    cpp = page // CMP_STRIDE
    nchunk = PAGES_PER_STEP * cpp
    steps = n_pages // PAGES_PER_STEP
    page_specs = [
        pl.BlockSpec((page * NSA_REC, HEAD_DIM),
                     lambda b, s, pt, k=k: (pt[b, s * PAGES_PER_STEP + k], 0))
        for k in range(PAGES_PER_STEP)]
    out_spec = pl.BlockSpec((None, 2 * N_KV_B, nchunk, HEAD_DIM), lambda b, s, pt: (b, 0, s, 0))
    grid_spec = pltpu.PrefetchScalarGridSpec(
        num_scalar_prefetch=1,
        grid=(bsz, steps),
        in_specs=[pl.BlockSpec((2, CMP_LEN, HEAD_DIM), lambda b, s, pt: (0, 0, 0)),
                  pl.BlockSpec((2, CMP_LEN, HEAD_DIM, HEAD_DIM), lambda b, s, pt: (0, 0, 0, 0))]
                 + page_specs,
        out_specs=[out_spec, out_spec],
    )
    shape = jax.ShapeDtypeStruct((bsz, 2 * N_KV_B, n_pages * cpp, HEAD_DIM), F32)
    return pl.pallas_call(
        functools.partial(_compress_pages_kernel, chunks_per_page=cpp),
        grid_spec=grid_spec,
        out_shape=[shape, shape],
        compiler_params=_params("parallel", "parallel"),
        name="compress_pages",
    )(page_table, pe, w1, *([cache_rows] * PAGES_PER_STEP))


def _cmp_branch(qs, kcmp, vcmp, pos_rows):
    n = kcmp.shape[0]
    s = _dot_nt(qs, kcmp.astype(BF16)) * SCALE
    cmp_end = lax.broadcasted_iota(jnp.int32, (1, n), 1) * CMP_STRIDE + (CMP_LEN - 1)
    mask = cmp_end <= pos_rows
    s = jnp.where(mask, s, NEG_BIG)
    m = jnp.max(s, axis=-1, keepdims=True)
    e = jnp.where(mask, jnp.exp(s - m), 0.0)
    p = e / jnp.maximum(jnp.sum(e, axis=-1, keepdims=True), 1e-30)
    return _dot(p.astype(BF16), vcmp.astype(BF16)), p


def _select_blocks(imp, pos_rows, n_sel):
    r, width = imp.shape
    lane = lax.broadcasted_iota(jnp.int32, (r, width), 1)
    qblk = pos_rows // SLC_BLOCK
    forced = (lane == 0) | (lane == qblk) | (lane == qblk - 1)
    valid = lane * SLC_BLOCK <= pos_rows
    imp = jnp.where(forced, FORCE_SCORE, jnp.where(valid, imp, -FORCE_SCORE))
    imp = jnp.where(lane < n_sel, imp, -jnp.inf)
    rank = jnp.zeros((r, width), F32)
    for i in range(n_sel):
        ci = imp[:, i:i + 1]
        better = (ci > imp) | ((ci == imp) & (lane > i))
        rank = rank + jnp.where(better, 1.0, 0.0)
    return jnp.where((rank < float(min(N_SELECT, n_sel))) & (lane < n_sel), 1.0, 0.0)


def _softmax_step(s, mask, v, carry):
    m, l, acc = carry
    s = jnp.where(mask, s, NEG_BIG)
    m_new = jnp.maximum(m, jnp.max(s, axis=-1, keepdims=True))
    alpha = jnp.exp(m - m_new)
    e = jnp.where(mask, jnp.exp(s - m_new), 0.0)
    l = alpha * l + jnp.sum(e, axis=-1, keepdims=True)
    acc = alpha * acc + _dot(e.astype(BF16), v)
    return m_new, l, acc


def _softmax_init(r):
    return (jnp.full((r, 1), NEG_BIG, F32), jnp.zeros((r, 1), F32), jnp.zeros((r, HEAD_DIM), F32))


def _softmax_finish(carry):
    _, l, acc = carry
    return acc / jnp.maximum(l, 1e-30)


def _nsa_prompt_kernel(q_ref, g_ref, kc_ref, vc_ref, ks_ref, vs_ref, kw_ref, vw_ref,
                       m_ref, e_ref, o_ref, *, tq, n_sel):
    i = pl.program_id(2)
    r = HPG * tq
    qs = jnp.concatenate([q_ref[:, h * HEAD_DIM:(h + 1) * HEAD_DIM] for h in range(HPG)],
                         axis=0).astype(BF16)
    pos_t = i * tq + lax.broadcasted_iota(jnp.int32, (tq, 1), 0)
    pos_r = jnp.concatenate([pos_t] * HPG, axis=0)

    o_c, p = _cmp_branch(qs, kc_ref[...], vc_ref[...], pos_r)
    psum = p[0:tq]
    for h in range(1, HPG):
        psum = psum + p[h * tq:(h + 1) * tq]
    sel = _select_blocks(_dot_split(psum, m_ref[...]), pos_t, n_sel).astype(BF16)

    lane = lax.broadcasted_iota(jnp.int32, (r, tq), 1)

    def slc_body(c, carry):
        off = pl.multiple_of(c * tq, tq)
        k = ks_ref[pl.ds(off, tq), :].astype(BF16)
        v = vs_ref[pl.ds(off, tq), :].astype(BF16)
        hit = _dot(sel, e_ref[:, pl.ds(off, tq)])
        mask = (jnp.concatenate([hit] * HPG, axis=0) > 0.5) & (off + lane <= pos_r)
        return _softmax_step(_dot_nt(qs, k) * SCALE, mask, v, carry)

    o_s = _softmax_finish(lax.fori_loop(0, i + 1, slc_body, _softmax_init(r)))

    def win_body(c, carry):
        off = pl.multiple_of(c * tq, tq)
        k = kw_ref[pl.ds(off, tq), :].astype(BF16)
        v = vw_ref[pl.ds(off, tq), :].astype(BF16)
        dlt = pos_r - (off + lane)
        mask = (dlt >= 0) & (dlt < WINDOW)
        return _softmax_step(_dot_nt(qs, k) * SCALE, mask, v, carry)

    lo = jnp.maximum(i - WINDOW // tq, 0)
    o_w = _softmax_finish(lax.fori_loop(lo, i + 1, win_body, _softmax_init(r)))

    for h in range(HPG):
        rows = slice(h * tq, (h + 1) * tq)
        gc = g_ref[:, 3 * h:3 * h + 1]
        gs = g_ref[:, 3 * h + 1:3 * h + 2]
        gw = g_ref[:, 3 * h + 2:3 * h + 3]
        o_ref[:, h * HEAD_DIM:(h + 1) * HEAD_DIM] = gc * o_c[rows] + gs * o_s[rows] + gw * o_w[rows]


def _cmp_to_slc(n_cmp_pad, n_cmp, n_sel, width):
    i = np.arange(n_cmp_pad)[:, None]
    j = np.arange(width)[None, :]
    lo = np.maximum(i * CMP_STRIDE, j * SLC_BLOCK)
    hi = np.minimum(i * CMP_STRIDE + CMP_LEN, (j + 1) * SLC_BLOCK)
    m = np.maximum(hi - lo, 0) / CMP_STRIDE
    m = np.where((i < n_cmp) & (j < n_sel), m, 0.0)
    return jnp.asarray(m, dtype=BF16)


def _nsa_prompt(qb, gates, cmp_kv, nsa, win, bsz, t):
    tq = 128
    nq = t // tq
    n_cmp = (t - CMP_LEN) // CMP_STRIDE + 1
    n_cmp_pad = cmp_kv.shape[2]
    n_sel = -(-t // SLC_BLOCK)
    m_mat = _cmp_to_slc(n_cmp_pad, n_cmp, n_sel, LANES)
    e_mat = jnp.asarray(np.arange(LANES)[:, None] == (np.arange(t) // SLC_BLOCK)[None, :], dtype=BF16)
    full = lambda shape: pl.BlockSpec(shape, lambda b, g, i: (0,) * len(shape))
    rows = lambda col: pl.BlockSpec((t, HEAD_DIM), lambda b, g, i, col=col: (b, col + g))
    return pl.pallas_call(
        functools.partial(_nsa_prompt_kernel, tq=tq, n_sel=n_sel),
        grid=(bsz, N_KV_B, nq),
        in_specs=[pl.BlockSpec((tq, HPG * HEAD_DIM), lambda b, g, i: (b * nq + i, g)),
                  pl.BlockSpec((tq, LANES), lambda b, g, i: (b * nq + i, g)),
                  pl.BlockSpec((None, None, n_cmp_pad, HEAD_DIM), lambda b, g, i: (b, g, 0, 0)),
                  pl.BlockSpec((None, None, n_cmp_pad, HEAD_DIM), lambda b, g, i: (b, 2 + g, 0, 0)),
                  rows(4), rows(6), rows(0), rows(2),
                  full(m_mat.shape), full(e_mat.shape)],
        out_specs=pl.BlockSpec((tq, HPG * HEAD_DIM), lambda b, g, i: (b * nq + i, g)),
        out_shape=jax.ShapeDtypeStruct((bsz * t, D_B), F32),
        compiler_params=_params("parallel", "parallel", "parallel"),
        name="nsa_prompt",
    )(qb, gates, cmp_kv, cmp_kv, nsa, nsa, win, win, m_mat, e_mat)


def _nsa_select_kernel(q_ref, ak_ref, av_ref, bk_ref, bv_ref, w2_ref, m_ref, oc_ref, sel_ref,
                       *, n_new, past, n_sel):
    qs = jnp.concatenate([q_ref[:, h * HEAD_DIM:(h + 1) * HEAD_DIM] for h in range(HPG)],
                         axis=0).astype(BF16)
    pos_t = past + lax.broadcasted_iota(jnp.int32, (n_new, 1), 0)
    pos_r = jnp.concatenate([pos_t] * HPG, axis=0)
    kcmp = _compress_finish(ak_ref[...], bk_ref[...], w2_ref[0])
    vcmp = _compress_finish(av_ref[...], bv_ref[...], w2_ref[1])
    o_c, p = _cmp_branch(qs, kcmp, vcmp, pos_r)
    oc_ref[...] = o_c
    psum = p[0:n_new]
    for h in range(1, HPG):
        psum = psum + p[h * n_new:(h + 1) * n_new]
    sel = _select_blocks(_dot_split(psum, m_ref[...]), pos_t, n_sel)
    sel_ref[...] = jnp.concatenate([sel] * HPG, axis=0)


def _nsa_select(qb, part_a, part_b, w2, past):
    bsz, n_new, _ = qb.shape
    n_cmp_pad = part_a.shape[2]
    total = past + n_new
    n_cmp = (total - CMP_LEN) // CMP_STRIDE + 1
    n_sel = -(-total // SLC_BLOCK)
    width = -(-n_sel // LANES) * LANES
    m_mat = _cmp_to_slc(n_cmp_pad, n_cmp, n_sel, width)
    assert n_cmp == n_cmp_pad - 1, "compressed tokens must come from the paged rows only"
    r = HPG * n_new
    k_spec = pl.BlockSpec((None, None, n_cmp_pad, HEAD_DIM), lambda b, g: (b, g, 0, 0))
    v_spec = pl.BlockSpec((None, None, n_cmp_pad, HEAD_DIM), lambda b, g: (b, N_KV_B + g, 0, 0))
    return pl.pallas_call(
        functools.partial(_nsa_select_kernel, n_new=n_new, past=past, n_sel=n_sel),
        grid=(bsz, N_KV_B),
        in_specs=[pl.BlockSpec((None, n_new, HPG * HEAD_DIM), lambda b, g: (b, 0, g)),
                  k_spec, v_spec, k_spec, v_spec,
                  pl.BlockSpec((2, HEAD_DIM, HEAD_DIM), lambda b, g: (0, 0, 0)),
                  pl.BlockSpec(m_mat.shape, lambda b, g: (0, 0))],
        out_specs=[pl.BlockSpec((None, None, r, HEAD_DIM), lambda b, g: (b, g, 0, 0)),
                   pl.BlockSpec((None, None, r, width), lambda b, g: (b, g, 0, 0))],
        out_shape=[jax.ShapeDtypeStruct((bsz, N_KV_B, r, HEAD_DIM), F32),
                   jax.ShapeDtypeStruct((bsz, N_KV_B, r, width), F32)],
        compiler_params=_params("parallel", "parallel"),
        name="nsa_select",
    )(qb, part_a, part_a, part_b, part_b, w2, m_mat)


SLC_PAGES_PER_STEP = 8


def _nsa_slc_decode_kernel(pt_ref, q_ref, sel_ref, new_ref, *refs, n_new, past, page):
    pps = SLC_PAGES_PER_STEP
    pages = refs[:pps]
    o_ref, m_ref, l_ref, acc_ref = refs[pps:]
    p = pl.program_id(1)
    n_steps = pl.num_programs(1) - 1
    r = HPG * n_new
    rows = N_KV_B * r
    t_of_row = lax.broadcasted_iota(jnp.int32, (rows, 1), 0) % n_new
    lane = lax.broadcasted_iota(jnp.int32, (rows, page), 1)
    bpp = page // SLC_BLOCK
    qg = [q_ref[g].astype(BF16) for g in range(N_KV_B)]
    sel = jnp.concatenate([sel_ref[g] for g in range(N_KV_B)], axis=0)
    blk_lane = lax.broadcasted_iota(jnp.int32, sel.shape, 1)

    @pl.when(p == 0)
    def _():
        m_ref[...] = jnp.full(m_ref.shape, NEG_BIG, F32)
        l_ref[...] = jnp.zeros(l_ref.shape, F32)
        acc_ref[...] = jnp.zeros(acc_ref.shape, F32)

    def masked_scores(ref, first_pos):
        first_blk = first_pos // SLC_BLOCK
        hit = jnp.zeros((rows, page), F32)
        for j in range(bpp):
            flag = jnp.sum(jnp.where(blk_lane == first_blk + j, sel, 0.0), axis=-1, keepdims=True)
            hit = jnp.where(lane // SLC_BLOCK == j, flag, hit)
        mask = (hit > 0.5) & (first_pos + lane <= past + t_of_row)
        s = jnp.concatenate(
            [_dot_nt(qg[g], ref[pl.ds(2 * N_KV_B + g, page, stride=NSA_REC), :].astype(BF16))
             for g in range(N_KV_B)], axis=0) * SCALE
        return jnp.where(mask, s, NEG_BIG), mask

    def visit(blocks):
        scored = [masked_scores(ref, pos) for ref, pos in blocks]
        m_old = m_ref[...]
        m_new = m_old
        for s, _ in scored:
            m_new = jnp.maximum(m_new, jnp.max(s, axis=-1, keepdims=True))
        alpha = jnp.exp(m_old - m_new)
        l = alpha * l_ref[...]
        acc = alpha * acc_ref[...]
        for (ref, _), (s, mask) in zip(blocks, scored):
            e = jnp.where(mask, jnp.exp(s - m_new), 0.0)
            l = l + jnp.sum(e, axis=-1, keepdims=True)
            eb = e.astype(BF16)
            acc = acc + jnp.concatenate(
                [_dot(eb[g * r:(g + 1) * r], ref[pl.ds(3 * N_KV_B + g, page, stride=NSA_REC), :].astype(BF16))
                 for g in range(N_KV_B)], axis=0)
        m_ref[...] = m_new
        l_ref[...] = l
        acc_ref[...] = acc

    @pl.when(p < n_steps)
    def _():
        visit([(pg, (p * pps + k) * page) for k, pg in enumerate(pages)])

    @pl.when(p == n_steps)
    def _():
        visit([(new_ref, past)])
        o_ref[...] = acc_ref[...] / jnp.maximum(l_ref[...], 1e-30)


def _nsa_slc_decode(q_groups, sel, new_rows, cache_rows, page_table, past, page, n_new):
    bsz = q_groups.shape[0]
    n_pages = page_table.shape[1]
    pps = SLC_PAGES_PER_STEP
    steps = n_pages // pps
    r = HPG * n_new
    rows = N_KV_B * r
    width = sel.shape[-1]
    blk = page * NSA_REC
    page_specs = [
        pl.BlockSpec((blk, HEAD_DIM),
                     lambda b, p, pt, k=k: (pt[b, jnp.minimum(p, steps - 1) * pps + k], 0))
        for k in range(pps)]
    grid_spec = pltpu.PrefetchScalarGridSpec(
        num_scalar_prefetch=1,
        grid=(bsz, steps + 1),
        in_specs=[pl.BlockSpec((None, N_KV_B, r, HEAD_DIM), lambda b, p, pt: (b, 0, 0, 0)),
                  pl.BlockSpec((None, N_KV_B, r, width), lambda b, p, pt: (b, 0, 0, 0)),
                  pl.BlockSpec((None, blk, HEAD_DIM), lambda b, p, pt: (b, 0, 0))] + page_specs,
        out_specs=pl.BlockSpec((None, rows, HEAD_DIM), lambda b, p, pt: (b, 0, 0)),
        scratch_shapes=[pltpu.VMEM((rows, 1), F32), pltpu.VMEM((rows, 1), F32),
                        pltpu.VMEM((rows, HEAD_DIM), F32)],
    )
    return pl.pallas_call(
        functools.partial(_nsa_slc_decode_kernel, n_new=n_new, past=past, page=page),
        grid_spec=grid_spec,
        out_shape=jax.ShapeDtypeStruct((bsz, rows, HEAD_DIM), F32),
        compiler_params=_params("parallel", "arbitrary"),
        name="nsa_slc_decode",
    )(page_table, q_groups, sel, new_rows, *([cache_rows] * pps))


def _nsa_merge_decode_kernel(q_ref, g_ref, oc_ref, os_ref, st_ref, new_ref, o_ref, nw_ref,
                             *, n_new, past, n_buf):
    r = HPG * n_new
    n_pad = new_ref.shape[0]
    t_of_row = lax.broadcasted_iota(jnp.int32, (r, 1), 0) % n_new
    pos_r = past + t_of_row
    buf_pos = past - n_buf + lax.broadcasted_iota(jnp.int32, (r, n_buf), 1)
    new_pos = past + lax.broadcasted_iota(jnp.int32, (r, n_pad), 1)

    def in_window(kpos):
        dlt = pos_r - kpos
        return (dlt >= 0) & (dlt < WINDOW) & (kpos >= 0)

    for g in range(N_KV_B):
        qs = jnp.concatenate(
            [q_ref[:, (g * HPG + h) * HEAD_DIM:(g * HPG + h + 1) * HEAD_DIM] for h in range(HPG)],
            axis=0).astype(BF16)
        kcol = slice(g * HEAD_DIM, (g + 1) * HEAD_DIM)
        vcol = slice((N_KV_B + g) * HEAD_DIM, (N_KV_B + g + 1) * HEAD_DIM)
        carry = _softmax_init(r)
        carry = _softmax_step(_dot_nt(qs, st_ref[:, kcol].astype(BF16)) * SCALE, in_window(buf_pos),
                              st_ref[:, vcol].astype(BF16), carry)
        carry = _softmax_step(_dot_nt(qs, new_ref[:, kcol].astype(BF16)) * SCALE, in_window(new_pos),
                              new_ref[:, vcol].astype(BF16), carry)
        o_w = _softmax_finish(carry)
        for h in range(HPG):
            rows = slice(h * n_new, (h + 1) * n_new)
            c0 = g * LANES + 3 * h
            out = (g_ref[:, c0:c0 + 1] * oc_ref[g, rows] + g_ref[:, c0 + 1:c0 + 2] * os_ref[g, rows]
                   + g_ref[:, c0 + 2:c0 + 3] * o_w[rows])
            o_ref[:, (g * HPG + h) * HEAD_DIM:(g * HPG + h + 1) * HEAD_DIM] = out
    shifted = pltpu.roll(st_ref[...], n_buf - n_new, 0)
    placed = pltpu.roll(new_ref[...], n_pad - n_new, 0)
    tail_row = lax.broadcasted_iota(jnp.int32, (n_pad, placed.shape[1]), 0)
    nw_ref[0:n_buf - n_pad, :] = shifted[0:n_buf - n_pad]
    nw_ref[n_buf - n_pad:n_buf, :] = jnp.where(tail_row >= n_pad - n_new, placed, shifted[n_buf - n_pad:])


def _nsa_merge_decode(qb, gates, o_cmp, o_slc, state_win, win_new, past):
    bsz, n_new, _ = qb.shape
    n_buf = state_win.shape[1]
    n_pad = win_new.shape[1]
    keep = min(WINDOW, past + n_new)
    assert keep == n_buf and n_buf > n_pad >= n_new
    r = HPG * n_new
    cols = 2 * D_KV_B
    per_b = lambda shape: pl.BlockSpec((None,) + shape, lambda b: (b,) + (0,) * len(shape))
    return pl.pallas_call(
        functools.partial(_nsa_merge_decode_kernel, n_new=n_new, past=past, n_buf=n_buf),
        grid=(bsz,),
        in_specs=[per_b((n_new, D_B)), per_b((n_new, 2 * LANES)),
                  per_b((N_KV_B, r, HEAD_DIM)), per_b((N_KV_B, r, HEAD_DIM)),
                  per_b((n_buf, cols)), per_b((n_pad, cols))],
        out_specs=[per_b((n_new, D_B)), per_b((keep, cols))],
        out_shape=[jax.ShapeDtypeStruct((bsz, n_new, D_B), F32),
                   jax.ShapeDtypeStruct((bsz, keep, cols), F32)],
        compiler_params=_params("parallel"),
        name="nsa_merge_decode",
    )(qb, gates, o_cmp, o_slc, state_win, win_new)


def _outproj_kernel(x_ref, oa_ref, ob_ref, wa_ref, wb_ref, g_ref, b_ref, h_ref):
    y = _dot(oa_ref[...].astype(BF16), wa_ref[...]) + _dot(ob_ref[...].astype(BF16), wb_ref[...])
    h_ref[...] = _layer_norm(ALPHA * x_ref[...] + y, g_ref[...], b_ref[...])


def _outproj(x, o_a, o_b, wa, wb, g, b):
    m, d = x.shape
    tm = min(256, m)
    row = lambda w: pl.BlockSpec((tm, w), lambda i: (i, 0))
    full = lambda a: pl.BlockSpec(a.shape, lambda i: (0, 0))
    return pl.pallas_call(
        _outproj_kernel,
        grid=(m // tm,),
        in_specs=[row(d), row(D_A), row(D_B), full(wa), full(wb), full(g), full(b)],
        out_specs=row(d),
        out_shape=jax.ShapeDtypeStruct((m, d), F32),
        compiler_params=_params("parallel"),
        name="outproj_ln",
    )(x, o_a, o_b, wa, wb, g, b)


def _peer_query_kernel(h_ref, w_ref, q_ref):
    q_ref[...] = _dot(h_ref[...].astype(BF16), w_ref[...])


def _peer_query(h, wq):
    m, d = h.shape
    n = wq.shape[1]
    tm = min(256, m)
    return pl.pallas_call(
        _peer_query_kernel,
        grid=(m // tm,),
        in_specs=[pl.BlockSpec((tm, d), lambda i: (i, 0)), pl.BlockSpec((d, n), lambda i: (0, 0))],
        out_specs=pl.BlockSpec((tm, n), lambda i: (i, 0)),
        out_shape=jax.ShapeDtypeStruct((m, n), F32),
        compiler_params=_params("parallel"),
        name="peer_query",
    )(h, wq)


def _top_values(x, k):
    n = x.shape[0]
    row = lax.broadcasted_iota(jnp.int32, x.shape, 0).astype(F32)
    out = []
    for _ in range(k):
        m = jnp.max(x, axis=0, keepdims=True)
        first = jnp.min(jnp.where(x == m, row, float(n)), axis=0, keepdims=True)
        x = jnp.where(row == first, -jnp.inf, x)
        out.append(m)
    return out


_PEER_PAIRS = [(i, j) for i in range(PEER_TOPK) for j in range(PEER_TOPK) if (i + 1) * (j + 1) <= PEER_TOPK]
_PEER_CAND_ROWS = -(-len(_PEER_PAIRS) // 8) * 8


def _peer_route_kernel(q_ref, sk_ref, s1_ref, s2_ref, e1_ref, e2_ref, tau_ref, cand_ref):
    half = sk_ref.shape[2]
    cand_ref[...] = jnp.full(cand_ref.shape, -jnp.inf, F32)
    for h in range(PEER_HEADS):
        s, tops = [], []
        for c in range(2):
            col = (2 * h + c) * half
            st = _dot_nt(sk_ref[c], q_ref[:, col:col + half].astype(BF16))
            s.append(st)
            tops.append(_top_values(st, PEER_TOPK))
        for n, (i, j) in enumerate(_PEER_PAIRS):
            cand_ref[n:n + 1, :] = tops[0][i] + tops[1][j]
        best = _top_values(cand_ref[...], PEER_TOPK)
        top = best[0]
        z = jnp.ones_like(top)
        for v in best[1:]:
            z = z + jnp.exp(v - top)
        s1_ref[h] = s[0]
        s2_ref[h] = s[1]
        e1_ref[h] = jnp.exp(s[0] - tops[0][0])
        e2_ref[h] = jnp.exp(s[1] - tops[1][0]) / z
        tau_ref[h:h + 1, :] = best[PEER_TOPK - 1]


def _peer_route(q, sub_keys):
    n = q.shape[0]
    tm = 128
    nk = sub_keys.shape[1]
    big = jax.ShapeDtypeStruct((PEER_HEADS, nk, n), F32)
    big_spec = pl.BlockSpec((PEER_HEADS, nk, tm), lambda i: (0, 0, i))
    return pl.pallas_call(
        _peer_route_kernel,
        grid=(n // tm,),
        in_specs=[pl.BlockSpec((tm, q.shape[1]), lambda i: (i, 0)),
                  pl.BlockSpec(sub_keys.shape, lambda i: (0, 0, 0))],
        out_specs=[big_spec] * 4 + [pl.BlockSpec((PEER_HEADS, tm), lambda i: (0, i))],
        out_shape=[big] * 4 + [jax.ShapeDtypeStruct((PEER_HEADS, n), F32)],
        scratch_shapes=[pltpu.VMEM((_PEER_CAND_ROWS, tm), F32)],
        compiler_params=_params("parallel"),
        name="peer_route",
    )(q, sub_keys)


def _peer_dense_kernel(h_ref, u_ref, v_ref, s1_ref, s2_ref, e1_ref, e2_ref, tau_ref, g_ref, b_ref,
                       o_ref, hb_ref, wt_ref, *, tm, ac, sub_ac):
    c = pl.program_id(1)
    nk = s2_ref.shape[1]
    kb_rows = nk // 2

    @pl.when(c == 0)
    def _():
        o_ref[...] = jnp.zeros(o_ref.shape, F32)
        hb_ref[...] = h_ref[...].astype(BF16)

    for sub in range(ac // sub_ac):
        ex = slice(sub * sub_ac * nk, (sub + 1) * sub_ac * nk)
        act = jax.nn.gelu(_dot_nt(u_ref[ex, :], hb_ref[...]))
        for ts in range(tm // LANES):
            tok = slice(ts * LANES, (ts + 1) * LANES)
            for kb in range(nk // kb_rows):
                keys = slice(kb * kb_rows, (kb + 1) * kb_rows)
                gates = [jnp.zeros((kb_rows, LANES), F32) for _ in range(sub_ac)]
                for h in range(PEER_HEADS):
                    s2 = s2_ref[h, keys, tok]
                    e2 = e2_ref[h, keys, tok]
                    tau = tau_ref[h:h + 1, tok]
                    for k in range(sub_ac):
                        aa = sub * sub_ac + k
                        chosen = s1_ref[h, aa:aa + 1, tok] + s2 >= tau
                        gates[k] = gates[k] + jnp.where(chosen, e1_ref[h, aa:aa + 1, tok] * e2, 0.0)
                for k in range(sub_ac):
                    rows = slice(k * nk + kb * kb_rows, k * nk + (kb + 1) * kb_rows)
                    wt_ref[sub * sub_ac * nk + rows.start:sub * sub_ac * nk + rows.stop, tok] = (
                        gates[k] * act[rows, tok]).astype(BF16)
    o_ref[...] += _dot_tn(wt_ref[...], v_ref[...])

    @pl.when(c == pl.num_programs(1) - 1)
    def _():
        o_ref[...] = _layer_norm(ALPHA * h_ref[...] + o_ref[...], g_ref[...], b_ref[...])


def _peer_dense(h, u, v, route, g, b):
    n, d = h.shape
    s1, s2, e1, e2, tau = route
    nk = s1.shape[1]
    tm = min(512, n)
    ac = 8
    big_spec = pl.BlockSpec((PEER_HEADS, nk, tm), lambda i, c: (0, 0, i))
    row_spec = pl.BlockSpec((PEER_HEADS, ac, tm), lambda i, c: (0, c, i))
    tab_spec = pl.BlockSpec((ac * nk, d), lambda i, c: (c, 0))
    full = lambda a: pl.BlockSpec(a.shape, lambda i, c: (0, 0))
    return pl.pallas_call(
        functools.partial(_peer_dense_kernel, tm=tm, ac=ac, sub_ac=4),
        grid=(n // tm, nk // ac),
        in_specs=[pl.BlockSpec((tm, d), lambda i, c: (i, 0)), tab_spec, tab_spec,
                  row_spec, big_spec, row_spec, big_spec,
                  pl.BlockSpec((PEER_HEADS, tm), lambda i, c: (0, i)), full(g), full(b)],
        out_specs=pl.BlockSpec((tm, d), lambda i, c: (i, 0)),
        out_shape=jax.ShapeDtypeStruct((n, d), F32),
        scratch_shapes=[pltpu.VMEM((tm, d), BF16), pltpu.VMEM((ac * nk, tm), BF16)],
        compiler_params=_params("parallel", "arbitrary"),
        name="peer_dense",
    )(h, u, v, s1, s2, e1, e2, tau, g, b)


def _post_block(x, o_a, o_b, wts):
    h = _outproj(x, o_a, o_b, wts["wo_a"], wts["wo_b"], wts["ln1_g"], wts["ln1_b"])
    q = _peer_query(h, wts["wq"])
    route = _peer_route(q, wts["sub_keys"])
    return _peer_dense(h, wts["u"], wts["v"], route, wts["ln2_g"], wts["ln2_b"])


def _pad_rows(a, rows):
    return jnp.pad(a, ((0, rows - a.shape[0]),) + ((0, 0),) * (a.ndim - 1))


def kernel(x_prompt, x_sample, cache_a, cache_nsa, state_win, page_table, w_in, cmp_pe_k, cmp_w1_k,
           cmp_w2_k, cmp_pe_v, cmp_w1_v, cmp_w2_v, w_out, ln1_g, ln1_b, peer_w_query, peer_sub_keys,
           peer_u, peer_v, ln2_g, ln2_b):
    assert w_in.shape[0] == DEPTH == 1
    bsz, t, d = x_prompt.shape
    dbsz, n_new, _ = x_sample.shape
    n_pool, page = cache_a.shape[1], cache_a.shape[2]
    past = page_table.shape[1] * page

    w = w_in[0]
    o = np.cumsum((0, D_A, 2 * D_A, D_B, 4 * D_KV_B, 2 * D_KV_B, 3 * N_HEADS_B))
    wg = w[:, o[5]:o[6]]
    per_g = 3 * HPG
    gate_w = jnp.concatenate(
        [jnp.pad(wg[:, g * per_g:(g + 1) * per_g], ((0, 0), (0, LANES - per_g))) for g in range(N_KV_B)],
        axis=1)
    wparts = {"qa": w[:, o[0]:o[1]], "kva": w[:, o[1]:o[2]], "qb": w[:, o[2]:o[3]],
              "nsa": w[:, o[3]:o[4]], "win": w[:, o[4]:o[5]], "gates": gate_w}
    wparts = {k: v.astype(BF16) for k, v in wparts.items()}
    row2 = lambda a: a[0].reshape(1, -1)
    wts = {"wo_a": w_out[0, :D_A].astype(BF16), "wo_b": w_out[0, D_A:].astype(BF16),
           "ln1_g": row2(ln1_g), "ln1_b": row2(ln1_b), "ln2_g": row2(ln2_g), "ln2_b": row2(ln2_b),
           "wq": peer_w_query[0].astype(BF16), "sub_keys": peer_sub_keys[0].astype(BF16),
           "u": peer_u[0].astype(BF16), "v": peer_v[0].astype(BF16)}
    pe = jnp.stack([cmp_pe_k[0], cmp_pe_v[0]])
    w1 = jnp.stack([cmp_w1_k[0], cmp_w1_v[0]]).astype(BF16)
    w2 = jnp.stack([cmp_w2_k[0], cmp_w2_v[0]]).astype(BF16)

    xp = x_prompt.reshape(bsz * t, d)
    tabs_p = _rope_tables(jnp.arange(t, dtype=jnp.int32))
    qa, kva, qb, nsa, win, gates = _project_all(xp.astype(BF16), wparts, tabs_p)
    o_a = _sb_prompt(qa, kva, bsz, t)
    cmp_kv = _compress_prompt(nsa, pe, w1, w2, bsz, t)
    o_b = _nsa_prompt(qb, gates, cmp_kv, nsa, win, bsz, t)
    y_prompt = _post_block(xp, o_a, o_b, wts).reshape(bsz, t, d)
    new_a_p = kva.reshape(1, bsz, t, 2, N_HEADS_A, HEAD_DIM)
    new_nsa_p = nsa.reshape(1, bsz, t, 4, N_KV_B, HEAD_DIM)
    keep_p = min(WINDOW, t)
    new_win_p = win.reshape(bsz, t, 2, N_KV_B, HEAD_DIM)[None, :, t - keep_p:]

    ns = dbsz * n_new
    xs = x_sample.reshape(ns, d)
    pos_s = past + jnp.arange(n_new, dtype=jnp.int32)
    tabs_s = tuple(jnp.tile(tb, (dbsz, 1)) for tb in _rope_tables(pos_s))
    qa_s, kva_s, qb_s, nsa_s, win_s, gates_s = _project_all(xs.astype(BF16), wparts, tabs_s)
    by_b = lambda a: a.reshape(dbsz, n_new, a.shape[-1])
    pad_page = lambda a: jnp.pad(by_b(a), ((0, 0), (0, page - n_new), (0, 0)))
    as_records = lambda a, rec: jnp.pad(a.reshape(dbsz, n_new * rec, HEAD_DIM),
                                        ((0, 0), (0, (page - n_new) * rec), (0, 0)))
    q_heads = jnp.pad(qa_s.reshape(dbsz, n_new, N_HEADS_A, HEAD_DIM).transpose(0, 2, 1, 3),
                      ((0, 0), (0, 0), (0, PAD_T - n_new), (0, 0)))
    o_a_h = _sb_decode(q_heads, as_records(kva_s, 2 * N_HEADS_A),
                       cache_a.reshape(n_pool * page * 2 * N_HEADS_A, HEAD_DIM), page_table, page)
    o_a_s = o_a_h.reshape(dbsz, N_HEADS_A, PAD_T, HEAD_DIM)[:, :, :n_new].transpose(0, 2, 1, 3)
    cache_n = cache_nsa.reshape(n_pool * page * NSA_REC, HEAD_DIM)
    part_a, part_b = _compress_pages(cache_n, page_table, pe, w1, page)
    o_cmp, sel = _nsa_select(by_b(qb_s), part_a, part_b, w2, past)
    q_groups = qb_s.reshape(dbsz, n_new, N_KV_B, HPG, HEAD_DIM).transpose(0, 2, 3, 1, 4).reshape(
        dbsz, N_KV_B, HPG * n_new, HEAD_DIM)
    o_slc = _nsa_slc_decode(q_groups, sel, as_records(nsa_s, NSA_REC), cache_n, page_table, past, page,
                            n_new).reshape(dbsz, N_KV_B, HPG * n_new, HEAD_DIM)
    n_buf = state_win.shape[2]
    o_b_s, new_win = _nsa_merge_decode(by_b(qb_s), by_b(gates_s), o_cmp, o_slc,
                                       state_win[0].reshape(dbsz, n_buf, 2 * D_KV_B), pad_page(win_s), past)
    rows_s = -(-ns // LANES) * LANES
    y_s = _post_block(_pad_rows(xs, rows_s), _pad_rows(o_a_s.reshape(ns, D_A), rows_s),
                      _pad_rows(o_b_s.reshape(ns, D_B), rows_s), wts)
    y_sample = y_s[:ns].reshape(dbsz, n_new, d)
    new_a_s = kva_s.reshape(1, dbsz, n_new, 2, N_HEADS_A, HEAD_DIM)
    new_nsa_s = nsa_s.reshape(1, dbsz, n_new, 4, N_KV_B, HEAD_DIM)
    new_win_s = new_win.reshape(1, dbsz, new_win.shape[1], 2, N_KV_B, HEAD_DIM)
    return (y_prompt, y_sample, new_a_p, new_nsa_p, new_win_p, new_a_s, new_nsa_s, new_win_s)
```

```python
import functools
import math

import numpy as np
import jax
import jax.numpy as jnp
from jax import lax
from jax.experimental import pallas as pl
from jax.experimental.pallas import tpu as pltpu

F32 = jnp.float32
BF16 = jnp.bfloat16

LANES = 128
VMEM_LIMIT = 56 * 1024 * 1024

HEAD_DIM = 128
N_HEADS_A = 8
N_HEADS_B = 8
N_KV_B = 2
HPG = N_HEADS_B // N_KV_B
D_A = N_HEADS_A * HEAD_DIM
D_B = N_HEADS_B * HEAD_DIM
D_KV_B = N_KV_B * HEAD_DIM
ROPE_DIM = HEAD_DIM // 4
ROPE_THETA = 500000.0
CMP_LEN = 32
CMP_STRIDE = 16
SLC_BLOCK = 64
N_SELECT = 16
WINDOW = 512
FORCE_SCORE = 1.0e4
PEER_HEADS = 8
PEER_NKEYS = 128
PEER_TOPK = 16
DEPTH = 1
ALPHA = (2 * DEPTH) ** 0.25
LN_EPS = 1e-5
NEG_BIG = -1e30
SCALE = 1.0 / math.sqrt(HEAD_DIM)


def _dot(a, b):
    return jnp.dot(a, b, preferred_element_type=F32)


def _dot_nt(a, b):
    return lax.dot_general(a, b, (((1,), (1,)), ((), ())), preferred_element_type=F32)


def _dot_tn(a, b):
    return lax.dot_general(a, b, (((0,), (0,)), ((), ())), preferred_element_type=F32)


def _dot_split(x, w):
    hi = x.astype(BF16)
    lo = (x - hi.astype(F32)).astype(BF16)
    return _dot(hi, w) + _dot(lo, w)


def _params(*sem):
    return pltpu.CompilerParams(dimension_semantics=sem, vmem_limit_bytes=VMEM_LIMIT)


def _layer_norm(r, g, b):
    mu = jnp.mean(r, axis=-1, keepdims=True)
    d = r - mu
    var = jnp.mean(d * d, axis=-1, keepdims=True)
    return d * lax.rsqrt(var + LN_EPS) * g + b


def _inproj_kernel(x_ref, w_ref, c_ref, sa_ref, sb_ref, o_ref, *, rope_flags, sigmoid):
    acc = _dot(x_ref[...], w_ref[...])
    for j, flag in enumerate(rope_flags):
        blk = acc[:, j * LANES:(j + 1) * LANES]
        if flag:
            blk = (blk * c_ref[...]
                   + pltpu.roll(blk, LANES - ROPE_DIM // 2, 1) * sa_ref[...]
                   + pltpu.roll(blk, ROPE_DIM // 2, 1) * sb_ref[...])
        if sigmoid:
            blk = jax.nn.sigmoid(blk)
        o_ref[:, j * LANES:(j + 1) * LANES] = blk


def _inproj(xb, w, tabs, rope_flags, sigmoid=False):
    m, k = xb.shape
    n = w.shape[1]
    c, sa, sb = tabs
    tm = min(256, m)
    nt = c.shape[0] // tm
    tab_spec = pl.BlockSpec((tm, LANES), lambda i: (i % nt, 0))
    return pl.pallas_call(
        functools.partial(_inproj_kernel, rope_flags=tuple(rope_flags), sigmoid=sigmoid),
        grid=(m // tm,),
        in_specs=[pl.BlockSpec((tm, k), lambda i: (i, 0)),
                  pl.BlockSpec((k, n), lambda i: (0, 0)),
                  tab_spec, tab_spec, tab_spec],
        out_specs=pl.BlockSpec((tm, n), lambda i: (i, 0)),
        out_shape=jax.ShapeDtypeStruct((m, n), F32),
        compiler_params=_params("parallel"),
        name="inproj",
    )(xb, w, c, sa, sb)


def _rope_tables(pos):
    half = ROPE_DIM // 2
    inv = ROPE_THETA ** (-jnp.arange(half, dtype=F32) / half)
    ang = pos.astype(F32)[:, None] * inv[None, :]
    cos, sin = jnp.cos(ang), jnp.sin(ang)
    t = pos.shape[0]
    ones = jnp.ones((t, LANES - ROPE_DIM), F32)
    zeros = jnp.zeros((t, LANES - half), F32)
    c = jnp.concatenate([cos, cos, ones], axis=1)
    sa = jnp.concatenate([-sin, zeros], axis=1)
    sb = jnp.concatenate([jnp.zeros((t, half), F32), sin, jnp.zeros((t, LANES - ROPE_DIM), F32)], axis=1)
    return c, sa, sb


def _project_all(xb, wparts, tabs):
    qa = _inproj(xb, wparts["qa"], tabs, [0] * 8)
    kva = _inproj(xb, wparts["kva"], tabs, [0] * 16)
    qb = _inproj(xb, wparts["qb"], tabs, [1] * 8)
    nsa = _inproj(xb, wparts["nsa"], tabs, [1, 1, 0, 0, 1, 1, 0, 0])
    win = _inproj(xb, wparts["win"], tabs, [1, 1, 0, 0])
    gates = _inproj(xb, wparts["gates"], tabs, [0, 0], sigmoid=True)
    return qa, kva, qb, nsa, win, gates


def _sb_block(q, k, v, mask, c, tri):
    ls, lr = _sb_logs(_dot_nt(q, k), mask)
    return _sb_weights(ls, lr, mask, c, tri), c + jnp.sum(lr, axis=1, keepdims=True)


def _sb_logs(qk, mask):
    z = qk * SCALE
    ls = jnp.minimum(z, 0.0) - jnp.log1p(jnp.exp(-jnp.abs(z)))
    lr = ls - z
    return ls, (lr if mask is None else jnp.where(mask, lr, 0.0))


def _sb_weights(ls, lr, mask, c, tri):
    w = jnp.exp(ls + _dot_split(lr, tri) + c)
    return w if mask is None else jnp.where(mask, w, 0.0)


def _tri(n):
    row = lax.broadcasted_iota(jnp.int32, (n, n), 0)
    col = lax.broadcasted_iota(jnp.int32, (n, n), 1)
    return (row > col).astype(BF16)


def _sb_prompt_kernel(q_ref, k_ref, v_ref, o_ref, *, tq, tk):
    i = pl.program_id(2)
    q = q_ref[...].astype(BF16)
    tri = _tri(tk)
    per_q = tq // tk
    row = i * tq + lax.broadcasted_iota(jnp.int32, (tq, tk), 0)
    col = lax.broadcasted_iota(jnp.int32, (tq, tk), 1)

    def block(j, carry, masked):
        c, acc = carry
        off = pl.multiple_of(j * tk, tk)
        k = k_ref[pl.ds(off, tk), :].astype(BF16)
        v = v_ref[pl.ds(off, tk), :].astype(BF16)
        w, c = _sb_block(q, k, v, (off + col < row) if masked else None, c, tri)
        return c, acc + _dot(w.astype(BF16), v)

    carry = (jnp.zeros((tq, 1), F32), jnp.zeros((tq, HEAD_DIM), F32))
    for d in range(per_q):
        carry = block((i + 1) * per_q - 1 - d, carry, True)
    _, acc = lax.fori_loop(0, i * per_q, lambda s, cr: block(i * per_q - 1 - s, cr, False), carry)
    o_ref[...] = acc


def _sb_prompt(qa, kva, bsz, t):
    tq, tk = 512, 256
    nq = t // tq
    return pl.pallas_call(
        functools.partial(_sb_prompt_kernel, tq=tq, tk=tk),
        grid=(bsz, N_HEADS_A, nq),
        in_specs=[pl.BlockSpec((tq, HEAD_DIM), lambda b, h, i: (b * nq + i, h)),
                  pl.BlockSpec((t, HEAD_DIM), lambda b, h, i: (b, h)),
                  pl.BlockSpec((t, HEAD_DIM), lambda b, h, i: (b, N_HEADS_A + h))],
        out_specs=pl.BlockSpec((tq, HEAD_DIM), lambda b, h, i: (b * nq + i, h)),
        out_shape=jax.ShapeDtypeStruct((bsz * t, D_A), F32),
        compiler_params=_params("parallel", "parallel", "parallel"),
        name="sb_prompt",
    )(qa, kva, kva)


SB_PAGES_PER_STEP = 8
PAD_T = 8


def _sb_decode_kernel(pt_ref, q_ref, new_ref, *refs, page):
    pages = refs[:SB_PAGES_PER_STEP]
    o_ref, c_ref, acc_ref = refs[SB_PAGES_PER_STEP:]
    p = pl.program_id(1)
    rows = N_HEADS_A * PAD_T
    rec = 2 * N_HEADS_A
    tri = _tri(page)
    qh = [q_ref[h].astype(BF16) for h in range(N_HEADS_A)]

    def scores(ref):
        return jnp.concatenate(
            [_dot_nt(qh[h], ref[pl.ds(h, page, stride=rec), :].astype(BF16)) for h in range(N_HEADS_A)],
            axis=0)

    def values(ref, w):
        wb = w.astype(BF16)
        return [_dot(wb[h * PAD_T:(h + 1) * PAD_T],
                     ref[pl.ds(N_HEADS_A + h, page, stride=rec), :].astype(BF16)) for h in range(N_HEADS_A)]

    @pl.when(p == 0)
    def _():
        t_of_row = lax.broadcasted_iota(jnp.int32, (rows, page), 0) % PAD_T
        mask = lax.broadcasted_iota(jnp.int32, (rows, page), 1) < t_of_row
        ls, lr = _sb_logs(scores(new_ref), mask)
        w = _sb_weights(ls, lr, mask, jnp.zeros((rows, 1), F32), tri)
        acc_ref[...] = jnp.concatenate(values(new_ref, w), axis=0)
        c_ref[...] = jnp.sum(lr, axis=1, keepdims=True)

    @pl.when(p > 0)
    def _():
        logs = [_sb_logs(scores(pg), None) for pg in pages]
        c = c_ref[...]
        acc = None
        for pg, (ls, lr) in zip(pages, logs):
            part = values(pg, _sb_weights(ls, lr, None, c, tri))
            acc = part if acc is None else [x + y for x, y in zip(acc, part)]
            c = c + jnp.sum(lr, axis=1, keepdims=True)
        c_ref[...] = c
        acc_ref[...] += jnp.concatenate(acc, axis=0)

    @pl.when(p == pl.num_programs(1) - 1)
    def _():
        o_ref[...] = acc_ref[...]


def _sb_decode(q_heads, new_rows, cache_rows, page_table, page):
    bsz = q_heads.shape[0]
    n_pages = page_table.shape[1]
    pps = SB_PAGES_PER_STEP
    steps = n_pages // pps
    rows = N_HEADS_A * PAD_T
    blk = page * 2 * N_HEADS_A
    page_specs = [
        pl.BlockSpec((blk, HEAD_DIM),
                     lambda b, p, pt, k=k: (pt[b, n_pages - 1 - (jnp.maximum(p - 1, 0) * pps + k)], 0))
        for k in range(pps)]
    grid_spec = pltpu.PrefetchScalarGridSpec(
        num_scalar_prefetch=1,
        grid=(bsz, steps + 1),
        in_specs=[pl.BlockSpec((None, N_HEADS_A, PAD_T, HEAD_DIM), lambda b, p, pt: (b, 0, 0, 0)),
                  pl.BlockSpec((None, blk, HEAD_DIM), lambda b, p, pt: (b, 0, 0))] + page_specs,
        out_specs=pl.BlockSpec((None, rows, HEAD_DIM), lambda b, p, pt: (b, 0, 0)),
        scratch_shapes=[pltpu.VMEM((rows, 1), F32), pltpu.VMEM((rows, HEAD_DIM), F32)],
    )
    return pl.pallas_call(
        functools.partial(_sb_decode_kernel, page=page),
        grid_spec=grid_spec,
        out_shape=jax.ShapeDtypeStruct((bsz, rows, HEAD_DIM), F32),
        compiler_params=_params("parallel", "arbitrary"),
        name="sb_decode",
    )(page_table, q_heads, new_rows, *([cache_rows] * pps))


def _compress_partial(load_rows, pe_ref, w1_ref, nchunk):
    half = CMP_LEN // 2
    a = jnp.zeros((nchunk, HEAD_DIM), F32)
    b = jnp.zeros((nchunk, HEAD_DIM), F32)
    for l in range(half):
        rows = load_rows(l)
        a = a + _dot((rows + pe_ref[l:l + 1, :]).astype(BF16), w1_ref[l])
        b = b + _dot((rows + pe_ref[half + l:half + l + 1, :]).astype(BF16), w1_ref[half + l])
    return a, b


def _compress_finish(a, b, w2):
    n = a.shape[0]
    pre = a + pltpu.roll(b, n - 1, 0)
    return _dot(jax.nn.gelu(pre).astype(BF16), w2)


def _compress_prompt_kernel(r_ref, pe_ref, w1_ref, w2_ref, o_ref, *, nchunk):
    load = lambda l: r_ref[pl.ds(l, nchunk, stride=CMP_STRIDE), :]
    a, b = _compress_partial(load, pe_ref, w1_ref, nchunk)
    o_ref[...] = _compress_finish(a, b, w2_ref[...])


def _compress_prompt(nsa, pe, w1, w2, bsz, t):
    nchunk = t // CMP_STRIDE
    return pl.pallas_call(
        functools.partial(_compress_prompt_kernel, nchunk=nchunk),
        grid=(bsz, 4),
        in_specs=[pl.BlockSpec((t, HEAD_DIM), lambda b, s: (b, s)),
                  pl.BlockSpec((None, CMP_LEN, HEAD_DIM), lambda b, s: (s // 2, 0, 0)),
                  pl.BlockSpec((None, CMP_LEN, HEAD_DIM, HEAD_DIM), lambda b, s: (s // 2, 0, 0, 0)),
                  pl.BlockSpec((None, HEAD_DIM, HEAD_DIM), lambda b, s: (s // 2, 0, 0))],
        out_specs=pl.BlockSpec((None, None, nchunk, HEAD_DIM), lambda b, s: (b, s, 0, 0)),
        out_shape=jax.ShapeDtypeStruct((bsz, 4, nchunk, HEAD_DIM), F32),
        compiler_params=_params("parallel", "parallel"),
        name="compress_prompt",
    )(nsa, pe, w1, w2)


PAGES_PER_STEP = 16


NSA_REC = 4 * N_KV_B


def _compress_pages_kernel(pt_ref, pe_ref, w1_ref, *refs, chunks_per_page):
    pages = refs[:PAGES_PER_STEP]
    a_ref, b_ref = refs[PAGES_PER_STEP:]
    nchunk = PAGES_PER_STEP * chunks_per_page
    for s in range(2 * N_KV_B):
        load = lambda l, s=s: jnp.concatenate(
            [pg[pl.ds(l * NSA_REC + s, chunks_per_page, stride=CMP_STRIDE * NSA_REC), :] for pg in pages],
            axis=0)
        a, b = _compress_partial(load, pe_ref.at[s // N_KV_B], w1_ref.at[s // N_KV_B], nchunk)
        a_ref[s] = a
        b_ref[s] = b


def _compress_pages(cache_rows, page_table, pe, w1, page):
    bsz, n_pages = page_table.shape
    cpp = page // CMP_STRIDE
    nchunk = PAGES_PER_STEP * cpp
    steps = n_pages // PAGES_PER_STEP
    page_specs = [
        pl.BlockSpec((page * NSA_REC, HEAD_DIM),
                     lambda b, s, pt, k=k: (pt[b, s * PAGES_PER_STEP + k], 0))
        for k in range(PAGES_PER_STEP)]
    out_spec = pl.BlockSpec((None, 2 * N_KV_B, nchunk, HEAD_DIM), lambda b, s, pt: (b, 0, s, 0))
    grid_spec = pltpu.PrefetchScalarGridSpec(
        num_scalar_prefetch=1,
        grid=(bsz, steps),
        in_specs=[pl.BlockSpec((2, CMP_LEN, HEAD_DIM), lambda b, s, pt: (0, 0, 0)),
                  pl.BlockSpec((2, CMP_LEN, HEAD_DIM, HEAD_DIM), lambda b, s, pt: (0, 0, 0, 0))]
                 + page_specs,
        out_specs=[out_spec, out_spec],
    )
    shape = jax.ShapeDtypeStruct((bsz, 2 * N_KV_B, n_pages * cpp, HEAD_DIM), F32)
    return pl.pallas_call(
        functools.partial(_compress_pages_kernel, chunks_per_page=cpp),
        grid_spec=grid_spec,
        out_shape=[shape, shape],
        compiler_params=_params("parallel", "parallel"),
        name="compress_pages",
    )(page_table, pe, w1, *([cache_rows] * PAGES_PER_STEP))


def _cmp_branch(qs, kcmp, vcmp, pos_rows):
    n = kcmp.shape[0]
    s = _dot_nt(qs, kcmp.astype(BF16)) * SCALE
    cmp_end = lax.broadcasted_iota(jnp.int32, (1, n), 1) * CMP_STRIDE + (CMP_LEN - 1)
    mask = cmp_end <= pos_rows
    s = jnp.where(mask, s, NEG_BIG)
    m = jnp.max(s, axis=-1, keepdims=True)
    e = jnp.where(mask, jnp.exp(s - m), 0.0)
    p = e / jnp.maximum(jnp.sum(e, axis=-1, keepdims=True), 1e-30)
    return _dot(p.astype(BF16), vcmp.astype(BF16)), p


def _select_blocks(imp, pos_rows, n_sel):
    r, width = imp.shape
    lane = lax.broadcasted_iota(jnp.int32, (r, width), 1)
    qblk = pos_rows // SLC_BLOCK
    forced = (lane == 0) | (lane == qblk) | (lane == qblk - 1)
    valid = lane * SLC_BLOCK <= pos_rows
    imp = jnp.where(forced, FORCE_SCORE, jnp.where(valid, imp, -FORCE_SCORE))
    imp = jnp.where(lane < n_sel, imp, -jnp.inf)
    rank = jnp.zeros((r, width), F32)
    for i in range(n_sel):
        ci = imp[:, i:i + 1]
        better = (ci > imp) | ((ci == imp) & (lane > i))
        rank = rank + jnp.where(better, 1.0, 0.0)
    return jnp.where((rank < float(min(N_SELECT, n_sel))) & (lane < n_sel), 1.0, 0.0)


def _softmax_step(s, mask, v, carry):
    m, l, acc = carry
    s = jnp.where(mask, s, NEG_BIG)
    m_new = jnp.maximum(m, jnp.max(s, axis=-1, keepdims=True))
    alpha = jnp.exp(m - m_new)
    e = jnp.where(mask, jnp.exp(s - m_new), 0.0)
    l = alpha * l + jnp.sum(e, axis=-1, keepdims=True)
    acc = alpha * acc + _dot(e.astype(BF16), v)
    return m_new, l, acc


def _softmax_init(r):
    return (jnp.full((r, 1), NEG_BIG, F32), jnp.zeros((r, 1), F32), jnp.zeros((r, HEAD_DIM), F32))


def _softmax_finish(carry):
    _, l, acc = carry
    return acc / jnp.maximum(l, 1e-30)


def _select_blocks_t(imp, pos, n_sel):
    rows, t = imp.shape
    blk = lax.broadcasted_iota(jnp.int32, (rows, t), 0)
    qblk = pos // SLC_BLOCK
    forced = (blk == 0) | (blk == qblk) | (blk == qblk - 1)
    valid = blk * SLC_BLOCK <= pos
    imp = jnp.where(forced, FORCE_SCORE, jnp.where(valid, imp, -FORCE_SCORE))
    imp = jnp.where(blk < n_sel, imp, -jnp.inf)
    rank = jnp.zeros((rows, t), F32)
    for i in range(n_sel):
        ri = imp[i:i + 1, :]
        better = (ri > imp) | ((ri == imp) & (blk > i))
        rank = rank + jnp.where(better, 1.0, 0.0)
    return jnp.where((rank < float(min(N_SELECT, n_sel))) & (blk < n_sel), 1.0, 0.0)


def _nsa_prompt_kernel(q_ref, g_ref, kc_ref, vc_ref, ks_ref, vs_ref, kw_ref, vw_ref,
                       mt_ref, e_ref, o_ref, *, tq, n_sel):
    i = pl.program_id(2)
    r = HPG * tq
    qs = jnp.concatenate([q_ref[:, h * HEAD_DIM:(h + 1) * HEAD_DIM] for h in range(HPG)],
                         axis=0).astype(BF16)
    pos_t = i * tq + lax.broadcasted_iota(jnp.int32, (tq, 1), 0)
    pos_r = jnp.concatenate([pos_t] * HPG, axis=0)

    o_c, p = _cmp_branch(qs, kc_ref[...], vc_ref[...], pos_r)
    psum = p[0:tq]
    for h in range(1, HPG):
        psum = psum + p[h * tq:(h + 1) * tq]
    hi = psum.astype(BF16)
    lo = (psum - hi.astype(F32)).astype(BF16)
    imp_t = _dot_nt(mt_ref[...], hi) + _dot_nt(mt_ref[...], lo)
    sel_rows = -(-n_sel // 8) * 8
    pos_lane = i * tq + lax.broadcasted_iota(jnp.int32, (1, tq), 1)
    sel_t = _select_blocks_t(imp_t[0:sel_rows], pos_lane, n_sel)
    sel = jnp.concatenate([sel_t, jnp.zeros((imp_t.shape[0] - sel_rows, tq), F32)], axis=0).T.astype(BF16)

    lane = lax.broadcasted_iota(jnp.int32, (r, tq), 1)

    def slc_body(c, carry):
        off = pl.multiple_of(c * tq, tq)
        k = ks_ref[pl.ds(off, tq), :].astype(BF16)
        v = vs_ref[pl.ds(off, tq), :].astype(BF16)
        hit = _dot(sel, e_ref[:, pl.ds(off, tq)])
        mask = (jnp.concatenate([hit] * HPG, axis=0) > 0.5) & (off + lane <= pos_r)
        return _softmax_step(_dot_nt(qs, k) * SCALE, mask, v, carry)

    o_s = _softmax_finish(lax.fori_loop(0, i + 1, slc_body, _softmax_init(r)))

    carry = _softmax_init(r)
    for d in range(WINDOW // tq, -1, -1):
        c = i - d
        off = pl.multiple_of(jnp.maximum(c, 0) * tq, tq)
        k = kw_ref[pl.ds(off, tq), :].astype(BF16)
        v = vw_ref[pl.ds(off, tq), :].astype(BF16)
        kpos = c * tq + lane
        dlt = pos_r - kpos
        mask = (dlt >= 0) & (dlt < WINDOW) & (kpos >= 0)
        carry = _softmax_step(_dot_nt(qs, k) * SCALE, mask, v, carry)
    o_w = _softmax_finish(carry)

    for h in range(HPG):
        rows = slice(h * tq, (h + 1) * tq)
        gc = g_ref[:, 3 * h:3 * h + 1]
        gs = g_ref[:, 3 * h + 1:3 * h + 2]
        gw = g_ref[:, 3 * h + 2:3 * h + 3]
        o_ref[:, h * HEAD_DIM:(h + 1) * HEAD_DIM] = gc * o_c[rows] + gs * o_s[rows] + gw * o_w[rows]


def _cmp_to_slc(n_cmp_pad, n_cmp, n_sel, width):
    i = np.arange(n_cmp_pad)[:, None]
    j = np.arange(width)[None, :]
    lo = np.maximum(i * CMP_STRIDE, j * SLC_BLOCK)
    hi = np.minimum(i * CMP_STRIDE + CMP_LEN, (j + 1) * SLC_BLOCK)
    m = np.maximum(hi - lo, 0) / CMP_STRIDE
    m = np.where((i < n_cmp) & (j < n_sel), m, 0.0)
    return jnp.asarray(m, dtype=BF16)


def _nsa_prompt(qb, gates, cmp_kv, nsa, win, bsz, t):
    tq = 256
    nq = t // tq
    n_cmp = (t - CMP_LEN) // CMP_STRIDE + 1
    n_cmp_pad = cmp_kv.shape[2]
    n_sel = -(-t // SLC_BLOCK)
    assert n_sel <= LANES
    m_mat = _cmp_to_slc(n_cmp_pad, n_cmp, n_sel, LANES).T
    e_mat = jnp.asarray(np.arange(LANES)[:, None] == (np.arange(t) // SLC_BLOCK)[None, :], dtype=BF16)
    full = lambda shape: pl.BlockSpec(shape, lambda b, g, i: (0,) * len(shape))
    rows = lambda col: pl.BlockSpec((t, HEAD_DIM), lambda b, g, i, col=col: (b, col + g))
    return pl.pallas_call(
        functools.partial(_nsa_prompt_kernel, tq=tq, n_sel=n_sel),
        grid=(bsz, N_KV_B, nq),
        in_specs=[pl.BlockSpec((tq, HPG * HEAD_DIM), lambda b, g, i: (b * nq + i, g)),
                  pl.BlockSpec((tq, LANES), lambda b, g, i: (b * nq + i, g)),
                  pl.BlockSpec((None, None, n_cmp_pad, HEAD_DIM), lambda b, g, i: (b, g, 0, 0)),
                  pl.BlockSpec((None, None, n_cmp_pad, HEAD_DIM), lambda b, g, i: (b, 2 + g, 0, 0)),
                  rows(4), rows(6), rows(0), rows(2),
                  full(m_mat.shape), full(e_mat.shape)],
        out_specs=pl.BlockSpec((tq, HPG * HEAD_DIM), lambda b, g, i: (b * nq + i, g)),
        out_shape=jax.ShapeDtypeStruct((bsz * t, D_B), F32),
        compiler_params=_params("parallel", "parallel", "parallel"),
        name="nsa_prompt",
    )(qb, gates, cmp_kv, cmp_kv, nsa, nsa, win, win, m_mat, e_mat)


def _nsa_select_kernel(q_ref, ak_ref, av_ref, bk_ref, bv_ref, w2_ref, m_ref, oc_ref, sel_ref,
                       *, n_new, past, n_sel):
    qs = jnp.concatenate([q_ref[:, h * HEAD_DIM:(h + 1) * HEAD_DIM] for h in range(HPG)],
                         axis=0).astype(BF16)
    pos_t = past + lax.broadcasted_iota(jnp.int32, (n_new, 1), 0)
    pos_r = jnp.concatenate([pos_t] * HPG, axis=0)
    kcmp = _compress_finish(ak_ref[...], bk_ref[...], w2_ref[0])
    vcmp = _compress_finish(av_ref[...], bv_ref[...], w2_ref[1])
    o_c, p = _cmp_branch(qs, kcmp, vcmp, pos_r)
    oc_ref[...] = o_c
    psum = p[0:n_new]
    for h in range(1, HPG):
        psum = psum + p[h * n_new:(h + 1) * n_new]
    sel = _select_blocks(_dot_split(psum, m_ref[...]), pos_t, n_sel)
    sel_ref[...] = jnp.concatenate([sel] * HPG, axis=0)


def _nsa_select(qb, part_a, part_b, w2, past):
    bsz, n_new, _ = qb.shape
    n_cmp_pad = part_a.shape[2]
    total = past + n_new
    n_cmp = (total - CMP_LEN) // CMP_STRIDE + 1
    n_sel = -(-total // SLC_BLOCK)
    width = -(-n_sel // LANES) * LANES
    m_mat = _cmp_to_slc(n_cmp_pad, n_cmp, n_sel, width)
    assert n_cmp == n_cmp_pad - 1, "compressed tokens must come from the paged rows only"
    r = HPG * n_new
    k_spec = pl.BlockSpec((None, None, n_cmp_pad, HEAD_DIM), lambda b, g: (b, g, 0, 0))
    v_spec = pl.BlockSpec((None, None, n_cmp_pad, HEAD_DIM), lambda b, g: (b, N_KV_B + g, 0, 0))
    return pl.pallas_call(
        functools.partial(_nsa_select_kernel, n_new=n_new, past=past, n_sel=n_sel),
        grid=(bsz, N_KV_B),
        in_specs=[pl.BlockSpec((None, n_new, HPG * HEAD_DIM), lambda b, g: (b, 0, g)),
                  k_spec, v_spec, k_spec, v_spec,
                  pl.BlockSpec((2, HEAD_DIM, HEAD_DIM), lambda b, g: (0, 0, 0)),
                  pl.BlockSpec(m_mat.shape, lambda b, g: (0, 0))],
        out_specs=[pl.BlockSpec((None, None, r, HEAD_DIM), lambda b, g: (b, g, 0, 0)),
                   pl.BlockSpec((None, None, r, width), lambda b, g: (b, g, 0, 0))],
        out_shape=[jax.ShapeDtypeStruct((bsz, N_KV_B, r, HEAD_DIM), F32),
                   jax.ShapeDtypeStruct((bsz, N_KV_B, r, width), F32)],
        compiler_params=_params("parallel", "parallel"),
        name="nsa_select",
    )(qb, part_a, part_a, part_b, part_b, w2, m_mat)


SLC_PAGES_PER_STEP = 8


def _nsa_slc_decode_kernel(pt_ref, q_ref, sel_ref, new_ref, *refs, n_new, past, page):
    pps = SLC_PAGES_PER_STEP
    pages = refs[:pps]
    o_ref, m_ref, l_ref, acc_ref = refs[pps:]
    p = pl.program_id(1)
    n_steps = pl.num_programs(1) - 1
    r = HPG * n_new
    rows = N_KV_B * r
    t_of_row = lax.broadcasted_iota(jnp.int32, (rows, 1), 0) % n_new
    lane = lax.broadcasted_iota(jnp.int32, (rows, page), 1)
    bpp = page // SLC_BLOCK
    qg = [q_ref[g].astype(BF16) for g in range(N_KV_B)]
    sel = jnp.concatenate([sel_ref[g] for g in range(N_KV_B)], axis=0)
    blk_lane = lax.broadcasted_iota(jnp.int32, sel.shape, 1)

    @pl.when(p == 0)
    def _():
        m_ref[...] = jnp.full(m_ref.shape, NEG_BIG, F32)
        l_ref[...] = jnp.zeros(l_ref.shape, F32)
        acc_ref[...] = jnp.zeros(acc_ref.shape, F32)

    def masked_scores(ref, first_pos):
        first_blk = first_pos // SLC_BLOCK
        hit = jnp.zeros((rows, page), F32)
        for j in range(bpp):
            flag = jnp.sum(jnp.where(blk_lane == first_blk + j, sel, 0.0), axis=-1, keepdims=True)
            hit = jnp.where(lane // SLC_BLOCK == j, flag, hit)
        mask = (hit > 0.5) & (first_pos + lane <= past + t_of_row)
        s = jnp.concatenate(
            [_dot_nt(qg[g], ref[pl.ds(2 * N_KV_B + g, page, stride=NSA_REC), :].astype(BF16))
             for g in range(N_KV_B)], axis=0) * SCALE
        return jnp.where(mask, s, NEG_BIG), mask

    def visit(blocks):
        scored = [masked_scores(ref, pos) for ref, pos in blocks]
        m_old = m_ref[...]
        m_new = m_old
        for s, _ in scored:
            m_new = jnp.maximum(m_new, jnp.max(s, axis=-1, keepdims=True))
        alpha = jnp.exp(m_old - m_new)
        l = alpha * l_ref[...]
        acc = alpha * acc_ref[...]
        for (ref, _), (s, mask) in zip(blocks, scored):
            e = jnp.where(mask, jnp.exp(s - m_new), 0.0)
            l = l + jnp.sum(e, axis=-1, keepdims=True)
            eb = e.astype(BF16)
            acc = acc + jnp.concatenate(
                [_dot(eb[g * r:(g + 1) * r], ref[pl.ds(3 * N_KV_B + g, page, stride=NSA_REC), :].astype(BF16))
                 for g in range(N_KV_B)], axis=0)
        m_ref[...] = m_new
        l_ref[...] = l
        acc_ref[...] = acc

    @pl.when(p < n_steps)
    def _():
        visit([(pg, (p * pps + k) * page) for k, pg in enumerate(pages)])

    @pl.when(p == n_steps)
    def _():
        visit([(new_ref, past)])
        o_ref[...] = acc_ref[...] / jnp.maximum(l_ref[...], 1e-30)


def _nsa_slc_decode(q_groups, sel, new_rows, cache_rows, page_table, past, page, n_new):
    bsz = q_groups.shape[0]
    n_pages = page_table.shape[1]
    pps = SLC_PAGES_PER_STEP
    steps = n_pages // pps
    r = HPG * n_new
    rows = N_KV_B * r
    width = sel.shape[-1]
    blk = page * NSA_REC
    page_specs = [
        pl.BlockSpec((blk, HEAD_DIM),
                     lambda b, p, pt, k=k: (pt[b, jnp.minimum(p, steps - 1) * pps + k], 0))
        for k in range(pps)]
    grid_spec = pltpu.PrefetchScalarGridSpec(
        num_scalar_prefetch=1,
        grid=(bsz, steps + 1),
        in_specs=[pl.BlockSpec((None, N_KV_B, r, HEAD_DIM), lambda b, p, pt: (b, 0, 0, 0)),
                  pl.BlockSpec((None, N_KV_B, r, width), lambda b, p, pt: (b, 0, 0, 0)),
                  pl.BlockSpec((None, blk, HEAD_DIM), lambda b, p, pt: (b, 0, 0))] + page_specs,
        out_specs=pl.BlockSpec((None, rows, HEAD_DIM), lambda b, p, pt: (b, 0, 0)),
        scratch_shapes=[pltpu.VMEM((rows, 1), F32), pltpu.VMEM((rows, 1), F32),
                        pltpu.VMEM((rows, HEAD_DIM), F32)],
    )
    return pl.pallas_call(
        functools.partial(_nsa_slc_decode_kernel, n_new=n_new, past=past, page=page),
        grid_spec=grid_spec,
        out_shape=jax.ShapeDtypeStruct((bsz, rows, HEAD_DIM), F32),
        compiler_params=_params("parallel", "arbitrary"),
        name="nsa_slc_decode",
    )(page_table, q_groups, sel, new_rows, *([cache_rows] * pps))


def _nsa_merge_decode_kernel(q_ref, g_ref, oc_ref, os_ref, st_ref, new_ref, o_ref, nw_ref,
                             *, n_new, past, n_buf):
    r = HPG * n_new
    n_pad = new_ref.shape[0]
    t_of_row = lax.broadcasted_iota(jnp.int32, (r, 1), 0) % n_new
    pos_r = past + t_of_row
    buf_pos = past - n_buf + lax.broadcasted_iota(jnp.int32, (r, n_buf), 1)
    new_pos = past + lax.broadcasted_iota(jnp.int32, (r, n_pad), 1)

    def in_window(kpos):
        dlt = pos_r - kpos
        return (dlt >= 0) & (dlt < WINDOW) & (kpos >= 0)

    for g in range(N_KV_B):
        qs = jnp.concatenate(
            [q_ref[:, (g * HPG + h) * HEAD_DIM:(g * HPG + h + 1) * HEAD_DIM] for h in range(HPG)],
            axis=0).astype(BF16)
        kcol = slice(g * HEAD_DIM, (g + 1) * HEAD_DIM)
        vcol = slice((N_KV_B + g) * HEAD_DIM, (N_KV_B + g + 1) * HEAD_DIM)
        carry = _softmax_init(r)
        carry = _softmax_step(_dot_nt(qs, st_ref[:, kcol].astype(BF16)) * SCALE, in_window(buf_pos),
                              st_ref[:, vcol].astype(BF16), carry)
        carry = _softmax_step(_dot_nt(qs, new_ref[:, kcol].astype(BF16)) * SCALE, in_window(new_pos),
                              new_ref[:, vcol].astype(BF16), carry)
        o_w = _softmax_finish(carry)
        for h in range(HPG):
            rows = slice(h * n_new, (h + 1) * n_new)
            c0 = g * LANES + 3 * h
            out = (g_ref[:, c0:c0 + 1] * oc_ref[g, rows] + g_ref[:, c0 + 1:c0 + 2] * os_ref[g, rows]
                   + g_ref[:, c0 + 2:c0 + 3] * o_w[rows])
            o_ref[:, (g * HPG + h) * HEAD_DIM:(g * HPG + h + 1) * HEAD_DIM] = out
    shifted = pltpu.roll(st_ref[...], n_buf - n_new, 0)
    placed = pltpu.roll(new_ref[...], n_pad - n_new, 0)
    tail_row = lax.broadcasted_iota(jnp.int32, (n_pad, placed.shape[1]), 0)
    nw_ref[0:n_buf - n_pad, :] = shifted[0:n_buf - n_pad]
    nw_ref[n_buf - n_pad:n_buf, :] = jnp.where(tail_row >= n_pad - n_new, placed, shifted[n_buf - n_pad:])


def _nsa_merge_decode(qb, gates, o_cmp, o_slc, state_win, win_new, past):
    bsz, n_new, _ = qb.shape
    n_buf = state_win.shape[1]
    n_pad = win_new.shape[1]
    keep = min(WINDOW, past + n_new)
    assert keep == n_buf and n_buf > n_pad >= n_new
    r = HPG * n_new
    cols = 2 * D_KV_B
    per_b = lambda shape: pl.BlockSpec((None,) + shape, lambda b: (b,) + (0,) * len(shape))
    return pl.pallas_call(
        functools.partial(_nsa_merge_decode_kernel, n_new=n_new, past=past, n_buf=n_buf),
        grid=(bsz,),
        in_specs=[per_b((n_new, D_B)), per_b((n_new, 2 * LANES)),
                  per_b((N_KV_B, r, HEAD_DIM)), per_b((N_KV_B, r, HEAD_DIM)),
                  per_b((n_buf, cols)), per_b((n_pad, cols))],
        out_specs=[per_b((n_new, D_B)), per_b((keep, cols))],
        out_shape=[jax.ShapeDtypeStruct((bsz, n_new, D_B), F32),
                   jax.ShapeDtypeStruct((bsz, keep, cols), F32)],
        compiler_params=_params("parallel"),
        name="nsa_merge_decode",
    )(qb, gates, o_cmp, o_slc, state_win, win_new)


def _outproj_kernel(x_ref, oa_ref, ob_ref, wa_ref, wb_ref, g_ref, b_ref, h_ref):
    y = _dot(oa_ref[...].astype(BF16), wa_ref[...]) + _dot(ob_ref[...].astype(BF16), wb_ref[...])
    h_ref[...] = _layer_norm(ALPHA * x_ref[...] + y, g_ref[...], b_ref[...])


def _outproj(x, o_a, o_b, wa, wb, g, b):
    m, d = x.shape
    tm = min(256, m)
    row = lambda w: pl.BlockSpec((tm, w), lambda i: (i, 0))
    full = lambda a: pl.BlockSpec(a.shape, lambda i: (0, 0))
    return pl.pallas_call(
        _outproj_kernel,
        grid=(m // tm,),
        in_specs=[row(d), row(D_A), row(D_B), full(wa), full(wb), full(g), full(b)],
        out_specs=row(d),
        out_shape=jax.ShapeDtypeStruct((m, d), F32),
        compiler_params=_params("parallel"),
        name="outproj_ln",
    )(x, o_a, o_b, wa, wb, g, b)


def _peer_query_kernel(h_ref, w_ref, q_ref):
    q_ref[...] = _dot(h_ref[...].astype(BF16), w_ref[...])


def _peer_query(h, wq):
    m, d = h.shape
    n = wq.shape[1]
    tm = min(256, m)
    return pl.pallas_call(
        _peer_query_kernel,
        grid=(m // tm,),
        in_specs=[pl.BlockSpec((tm, d), lambda i: (i, 0)), pl.BlockSpec((d, n), lambda i: (0, 0))],
        out_specs=pl.BlockSpec((tm, n), lambda i: (i, 0)),
        out_shape=jax.ShapeDtypeStruct((m, n), F32),
        compiler_params=_params("parallel"),
        name="peer_query",
    )(h, wq)


def _top_values(x, k):
    n = x.shape[0]
    row = lax.broadcasted_iota(jnp.int32, x.shape, 0).astype(F32)
    out = []
    for _ in range(k):
        m = jnp.max(x, axis=0, keepdims=True)
        first = jnp.min(jnp.where(x == m, row, float(n)), axis=0, keepdims=True)
        x = jnp.where(row == first, -jnp.inf, x)
        out.append(m)
    return out


_PEER_PAIRS = [(i, j) for i in range(PEER_TOPK) for j in range(PEER_TOPK) if (i + 1) * (j + 1) <= PEER_TOPK]
_PEER_CAND_ROWS = -(-len(_PEER_PAIRS) // 8) * 8


def _peer_route_kernel(q_ref, sk_ref, s1_ref, s2_ref, e1_ref, e2_ref, tau_ref, cand_ref):
    half = sk_ref.shape[2]
    cand_ref[...] = jnp.full(cand_ref.shape, -jnp.inf, F32)
    for h in range(PEER_HEADS):
        s, tops = [], []
        for c in range(2):
            col = (2 * h + c) * half
            st = _dot_nt(sk_ref[c], q_ref[:, col:col + half].astype(BF16))
            s.append(st)
            tops.append(_top_values(st, PEER_TOPK))
        for n, (i, j) in enumerate(_PEER_PAIRS):
            cand_ref[n:n + 1, :] = tops[0][i] + tops[1][j]
        best = _top_values(cand_ref[...], PEER_TOPK)
        top = best[0]
        z = jnp.ones_like(top)
        for v in best[1:]:
            z = z + jnp.exp(v - top)
        s1_ref[h] = s[0]
        s2_ref[h] = s[1]
        e1_ref[h] = jnp.exp(s[0] - tops[0][0])
        e2_ref[h] = jnp.exp(s[1] - tops[1][0]) / z
        tau_ref[h:h + 1, :] = best[PEER_TOPK - 1]


def _peer_route(q, sub_keys):
    n = q.shape[0]
    tm = 128
    nk = sub_keys.shape[1]
    big = jax.ShapeDtypeStruct((PEER_HEADS, nk, n), F32)
    big_spec = pl.BlockSpec((PEER_HEADS, nk, tm), lambda i: (0, 0, i))
    return pl.pallas_call(
        _peer_route_kernel,
        grid=(n // tm,),
        in_specs=[pl.BlockSpec((tm, q.shape[1]), lambda i: (i, 0)),
                  pl.BlockSpec(sub_keys.shape, lambda i: (0, 0, 0))],
        out_specs=[big_spec] * 4 + [pl.BlockSpec((PEER_HEADS, tm), lambda i: (0, i))],
        out_shape=[big] * 4 + [jax.ShapeDtypeStruct((PEER_HEADS, n), F32)],
        scratch_shapes=[pltpu.VMEM((_PEER_CAND_ROWS, tm), F32)],
        compiler_params=_params("parallel"),
        name="peer_route",
    )(q, sub_keys)


def _peer_dense_kernel(h_ref, u_ref, v_ref, s1_ref, s2_ref, e1_ref, e2_ref, tau_ref, g_ref, b_ref,
                       o_ref, hb_ref, wt_ref, *, tm, ac, sub_ac):
    c = pl.program_id(1)
    nk = s2_ref.shape[1]
    kb_rows = nk // 4

    @pl.when(c == 0)
    def _():
        o_ref[...] = jnp.zeros(o_ref.shape, F32)
        hb_ref[...] = h_ref[...].astype(BF16)

    for sub in range(ac // sub_ac):
        ex = slice(sub * sub_ac * nk, (sub + 1) * sub_ac * nk)
        act = jax.nn.gelu(_dot_nt(u_ref[ex, :], hb_ref[...]))
        for ts in range(tm // LANES):
            tok = slice(ts * LANES, (ts + 1) * LANES)
            for kb in range(nk // kb_rows):
                keys = slice(kb * kb_rows, (kb + 1) * kb_rows)
                gates = [jnp.zeros((kb_rows, LANES), F32) for _ in range(sub_ac)]
                for h in range(PEER_HEADS):
                    s2 = s2_ref[h, keys, tok]
                    e2 = e2_ref[h, keys, tok]
                    tau = tau_ref[h:h + 1, tok]
                    for k in range(sub_ac):
                        aa = sub * sub_ac + k
                        chosen = s1_ref[h, aa:aa + 1, tok] + s2 >= tau
                        gates[k] = gates[k] + jnp.where(chosen, e1_ref[h, aa:aa + 1, tok] * e2, 0.0)
                for k in range(sub_ac):
                    rows = slice(k * nk + kb * kb_rows, k * nk + (kb + 1) * kb_rows)
                    wt_ref[sub * sub_ac * nk + rows.start:sub * sub_ac * nk + rows.stop, tok] = (
                        gates[k] * act[rows, tok]).astype(BF16)
    o_ref[...] += _dot_tn(wt_ref[...], v_ref[...])

    @pl.when(c == pl.num_programs(1) - 1)
    def _():
        o_ref[...] = _layer_norm(ALPHA * h_ref[...] + o_ref[...], g_ref[...], b_ref[...])


def _peer_dense(h, u, v, route, g, b):
    n, d = h.shape
    s1, s2, e1, e2, tau = route
    nk = s1.shape[1]
    tm = min(512, n)
    ac = 8
    big_spec = pl.BlockSpec((PEER_HEADS, nk, tm), lambda i, c: (0, 0, i))
    row_spec = pl.BlockSpec((PEER_HEADS, ac, tm), lambda i, c: (0, c, i))
    tab_spec = pl.BlockSpec((ac * nk, d), lambda i, c: (c, 0))
    full = lambda a: pl.BlockSpec(a.shape, lambda i, c: (0, 0))
    return pl.pallas_call(
        functools.partial(_peer_dense_kernel, tm=tm, ac=ac, sub_ac=4),
        grid=(n // tm, nk // ac),
        in_specs=[pl.BlockSpec((tm, d), lambda i, c: (i, 0)), tab_spec, tab_spec,
                  row_spec, big_spec, row_spec, big_spec,
                  pl.BlockSpec((PEER_HEADS, tm), lambda i, c: (0, i)), full(g), full(b)],
        out_specs=pl.BlockSpec((tm, d), lambda i, c: (i, 0)),
        out_shape=jax.ShapeDtypeStruct((n, d), F32),
        scratch_shapes=[pltpu.VMEM((tm, d), BF16), pltpu.VMEM((ac * nk, tm), BF16)],
        compiler_params=_params("parallel", "arbitrary"),
        name="peer_dense",
    )(h, u, v, s1, s2, e1, e2, tau, g, b)


def _post_block(x, o_a, o_b, wts):
    h = _outproj(x, o_a, o_b, wts["wo_a"], wts["wo_b"], wts["ln1_g"], wts["ln1_b"])
    q = _peer_query(h, wts["wq"])
    route = _peer_route(q, wts["sub_keys"])
    return _peer_dense(h, wts["u"], wts["v"], route, wts["ln2_g"], wts["ln2_b"])


def _pad_rows(a, rows):
    return jnp.pad(a, ((0, rows - a.shape[0]),) + ((0, 0),) * (a.ndim - 1))


def kernel(x_prompt, x_sample, cache_a, cache_nsa, state_win, page_table, w_in, cmp_pe_k, cmp_w1_k,
           cmp_w2_k, cmp_pe_v, cmp_w1_v, cmp_w2_v, w_out, ln1_g, ln1_b, peer_w_query, peer_sub_keys,
           peer_u, peer_v, ln2_g, ln2_b):
    assert w_in.shape[0] == DEPTH == 1
    bsz, t, d = x_prompt.shape
    dbsz, n_new, _ = x_sample.shape
    n_pool, page = cache_a.shape[1], cache_a.shape[2]
    past = page_table.shape[1] * page

    w = w_in[0]
    o = np.cumsum((0, D_A, 2 * D_A, D_B, 4 * D_KV_B, 2 * D_KV_B, 3 * N_HEADS_B))
    wg = w[:, o[5]:o[6]]
    per_g = 3 * HPG
    gate_w = jnp.concatenate(
        [jnp.pad(wg[:, g * per_g:(g + 1) * per_g], ((0, 0), (0, LANES - per_g))) for g in range(N_KV_B)],
        axis=1)
    wparts = {"qa": w[:, o[0]:o[1]], "kva": w[:, o[1]:o[2]], "qb": w[:, o[2]:o[3]],
              "nsa": w[:, o[3]:o[4]], "win": w[:, o[4]:o[5]], "gates": gate_w}
    wparts = {k: v.astype(BF16) for k, v in wparts.items()}
    row2 = lambda a: a[0].reshape(1, -1)
    wts = {"wo_a": w_out[0, :D_A].astype(BF16), "wo_b": w_out[0, D_A:].astype(BF16),
           "ln1_g": row2(ln1_g), "ln1_b": row2(ln1_b), "ln2_g": row2(ln2_g), "ln2_b": row2(ln2_b),
           "wq": peer_w_query[0].astype(BF16), "sub_keys": peer_sub_keys[0].astype(BF16),
           "u": peer_u[0].astype(BF16), "v": peer_v[0].astype(BF16)}
    pe = jnp.stack([cmp_pe_k[0], cmp_pe_v[0]])
    w1 = jnp.stack([cmp_w1_k[0], cmp_w1_v[0]]).astype(BF16)
    w2 = jnp.stack([cmp_w2_k[0], cmp_w2_v[0]]).astype(BF16)

    xp = x_prompt.reshape(bsz * t, d)
    tabs_p = _rope_tables(jnp.arange(t, dtype=jnp.int32))
    qa, kva, qb, nsa, win, gates = _project_all(xp.astype(BF16), wparts, tabs_p)
    o_a = _sb_prompt(qa, kva, bsz, t)
    cmp_kv = _compress_prompt(nsa, pe, w1, w2, bsz, t)
    o_b = _nsa_prompt(qb, gates, cmp_kv, nsa, win, bsz, t)
    y_prompt = _post_block(xp, o_a, o_b, wts).reshape(bsz, t, d)
    new_a_p = kva.reshape(1, bsz, t, 2, N_HEADS_A, HEAD_DIM)
    new_nsa_p = nsa.reshape(1, bsz, t, 4, N_KV_B, HEAD_DIM)
    keep_p = min(WINDOW, t)
    new_win_p = win.reshape(bsz, t, 2, N_KV_B, HEAD_DIM)[None, :, t - keep_p:]

    ns = dbsz * n_new
    xs = x_sample.reshape(ns, d)
    pos_s = past + jnp.arange(n_new, dtype=jnp.int32)
    tabs_s = tuple(jnp.tile(tb, (dbsz, 1)) for tb in _rope_tables(pos_s))
    qa_s, kva_s, qb_s, nsa_s, win_s, gates_s = _project_all(xs.astype(BF16), wparts, tabs_s)
    by_b = lambda a: a.reshape(dbsz, n_new, a.shape[-1])
    pad_page = lambda a: jnp.pad(by_b(a), ((0, 0), (0, page - n_new), (0, 0)))
    as_records = lambda a, rec: jnp.pad(a.reshape(dbsz, n_new * rec, HEAD_DIM),
                                        ((0, 0), (0, (page - n_new) * rec), (0, 0)))
    q_heads = jnp.pad(qa_s.reshape(dbsz, n_new, N_HEADS_A, HEAD_DIM).transpose(0, 2, 1, 3),
                      ((0, 0), (0, 0), (0, PAD_T - n_new), (0, 0)))
    o_a_h = _sb_decode(q_heads, as_records(kva_s, 2 * N_HEADS_A),
                       cache_a.reshape(n_pool * page * 2 * N_HEADS_A, HEAD_DIM), page_table, page)
    o_a_s = o_a_h.reshape(dbsz, N_HEADS_A, PAD_T, HEAD_DIM)[:, :, :n_new].transpose(0, 2, 1, 3)
    cache_n = cache_nsa.reshape(n_pool * page * NSA_REC, HEAD_DIM)
    part_a, part_b = _compress_pages(cache_n, page_table, pe, w1, page)
    o_cmp, sel = _nsa_select(by_b(qb_s), part_a, part_b, w2, past)
    q_groups = qb_s.reshape(dbsz, n_new, N_KV_B, HPG, HEAD_DIM).transpose(0, 2, 3, 1, 4).reshape(
        dbsz, N_KV_B, HPG * n_new, HEAD_DIM)
    o_slc = _nsa_slc_decode(q_groups, sel, as_records(nsa_s, NSA_REC), cache_n, page_table, past, page,
                            n_new).reshape(dbsz, N_KV_B, HPG * n_new, HEAD_DIM)
    n_buf = state_win.shape[2]
    o_b_s, new_win = _nsa_merge_decode(by_b(qb_s), by_b(gates_s), o_cmp, o_slc,
                                       state_win[0].reshape(dbsz, n_buf, 2 * D_KV_B), pad_page(win_s), past)
    rows_s = -(-ns // LANES) * LANES
    y_s = _post_block(_pad_rows(xs, rows_s), _pad_rows(o_a_s.reshape(ns, D_A), rows_s),
                      _pad_rows(o_b_s.reshape(ns, D_B), rows_s), wts)
    y_sample = y_s[:ns].reshape(dbsz, n_new, d)
    new_a_s = kva_s.reshape(1, dbsz, n_new, 2, N_HEADS_A, HEAD_DIM)
    new_nsa_s = nsa_s.reshape(1, dbsz, n_new, 4, N_KV_B, HEAD_DIM)
    new_win_s = new_win.reshape(1, dbsz, new_win.shape[1], 2, N_KV_B, HEAD_DIM)
    return (y_prompt, y_sample, new_a_p, new_nsa_p, new_win_p, new_a_s, new_nsa_s, new_win_s)
```

```python
import functools
import math

import numpy as np
import jax
import jax.numpy as jnp
from jax import lax
from jax.experimental import pallas as pl
from jax.experimental.pallas import tpu as pltpu

F32 = jnp.float32
BF16 = jnp.bfloat16

LANES = 128
VMEM_LIMIT = 56 * 1024 * 1024

HEAD_DIM = 128
N_HEADS_A = 8
N_HEADS_B = 8
N_KV_B = 2
HPG = N_HEADS_B // N_KV_B
D_A = N_HEADS_A * HEAD_DIM
D_B = N_HEADS_B * HEAD_DIM
D_KV_B = N_KV_B * HEAD_DIM
ROPE_DIM = HEAD_DIM // 4
ROPE_THETA = 500000.0
CMP_LEN = 32
CMP_STRIDE = 16
SLC_BLOCK = 64
N_SELECT = 16
WINDOW = 512
FORCE_SCORE = 1.0e4
PEER_HEADS = 8
PEER_NKEYS = 128
PEER_TOPK = 16
DEPTH = 1
ALPHA = (2 * DEPTH) ** 0.25
LN_EPS = 1e-5
NEG_BIG = -1e30
SCALE = 1.0 / math.sqrt(HEAD_DIM)


def _dot(a, b):
    return jnp.dot(a, b, preferred_element_type=F32)


def _dot_nt(a, b):
    return lax.dot_general(a, b, (((1,), (1,)), ((), ())), preferred_element_type=F32)


def _dot_tn(a, b):
    return lax.dot_general(a, b, (((0,), (0,)), ((), ())), preferred_element_type=F32)


def _dot_split(x, w):
    hi = x.astype(BF16)
    lo = (x - hi.astype(F32)).astype(BF16)
    return _dot(hi, w) + _dot(lo, w)


def _params(*sem):
    return pltpu.CompilerParams(dimension_semantics=sem, vmem_limit_bytes=VMEM_LIMIT)


def _layer_norm(r, g, b):
    mu = jnp.mean(r, axis=-1, keepdims=True)
    d = r - mu
    var = jnp.mean(d * d, axis=-1, keepdims=True)
    return d * lax.rsqrt(var + LN_EPS) * g + b


def _inproj_kernel(x_ref, w_ref, c_ref, sa_ref, sb_ref, o_ref, *, rope_flags, sigmoid):
    acc = _dot(x_ref[...], w_ref[...])
    for j, flag in enumerate(rope_flags):
        blk = acc[:, j * LANES:(j + 1) * LANES]
        if flag:
            blk = (blk * c_ref[...]
                   + pltpu.roll(blk, LANES - ROPE_DIM // 2, 1) * sa_ref[...]
                   + pltpu.roll(blk, ROPE_DIM // 2, 1) * sb_ref[...])
        if sigmoid:
            blk = jax.nn.sigmoid(blk)
        o_ref[:, j * LANES:(j + 1) * LANES] = blk


def _inproj(xb, w, tabs, rope_flags, sigmoid=False):
    m, k = xb.shape
    n = w.shape[1]
    c, sa, sb = tabs
    tm = min(1024, m)
    nt = c.shape[0] // tm
    tab_spec = pl.BlockSpec((tm, LANES), lambda i: (i % nt, 0))
    return pl.pallas_call(
        functools.partial(_inproj_kernel, rope_flags=tuple(rope_flags), sigmoid=sigmoid),
        grid=(m // tm,),
        in_specs=[pl.BlockSpec((tm, k), lambda i: (i, 0)),
                  pl.BlockSpec((k, n), lambda i: (0, 0)),
                  tab_spec, tab_spec, tab_spec],
        out_specs=pl.BlockSpec((tm, n), lambda i: (i, 0)),
        out_shape=jax.ShapeDtypeStruct((m, n), F32),
        compiler_params=_params("parallel"),
        name="inproj",
    )(xb, w, c, sa, sb)


def _rope_tables(pos):
    half = ROPE_DIM // 2
    inv = ROPE_THETA ** (-jnp.arange(half, dtype=F32) / half)
    ang = pos.astype(F32)[:, None] * inv[None, :]
    cos, sin = jnp.cos(ang), jnp.sin(ang)
    t = pos.shape[0]
    ones = jnp.ones((t, LANES - ROPE_DIM), F32)
    zeros = jnp.zeros((t, LANES - half), F32)
    c = jnp.concatenate([cos, cos, ones], axis=1)
    sa = jnp.concatenate([-sin, zeros], axis=1)
    sb = jnp.concatenate([jnp.zeros((t, half), F32), sin, jnp.zeros((t, LANES - ROPE_DIM), F32)], axis=1)
    return c, sa, sb


def _project_all(xb, wparts, tabs):
    qa = _inproj(xb, wparts["qa"], tabs, [0] * 8)
    kva = _inproj(xb, wparts["kva"], tabs, [0] * 16)
    qb = _inproj(xb, wparts["qb"], tabs, [1] * 8)
    nsa = _inproj(xb, wparts["nsa"], tabs, [1, 1, 0, 0, 1, 1, 0, 0])
    win = _inproj(xb, wparts["win"], tabs, [1, 1, 0, 0])
    gates = _inproj(xb, wparts["gates"], tabs, [0, 0], sigmoid=True)
    return qa, kva, qb, nsa, win, gates


def _sb_block(q, k, v, mask, c, tri):
    ls, lr = _sb_logs(_dot_nt(q, k), mask)
    return _sb_weights(ls, lr, mask, c, tri), c + jnp.sum(lr, axis=1, keepdims=True)


def _sb_logs(qk, mask):
    z = qk * SCALE
    ls = jnp.minimum(z, 0.0) - jnp.log(1.0 + jnp.exp(-jnp.abs(z)))
    lr = ls - z
    return ls, (lr if mask is None else jnp.where(mask, lr, 0.0))


def _sb_weights(ls, lr, mask, c, tri):
    w = jnp.exp(ls + _dot_split(lr, tri) + c)
    return w if mask is None else jnp.where(mask, w, 0.0)


def _tri(n):
    row = lax.broadcasted_iota(jnp.int32, (n, n), 0)
    col = lax.broadcasted_iota(jnp.int32, (n, n), 1)
    return (row > col).astype(BF16)


def _sb_prompt_kernel(q_ref, k_ref, v_ref, o_ref, *, tq, tk):
    i = pl.program_id(2)
    q = q_ref[...].astype(BF16)
    tri = _tri(tk)
    per_q = tq // tk
    row = i * tq + lax.broadcasted_iota(jnp.int32, (tq, tk), 0)
    col = lax.broadcasted_iota(jnp.int32, (tq, tk), 1)

    def block(j, carry, masked):
        c, acc = carry
        off = pl.multiple_of(j * tk, tk)
        k = k_ref[pl.ds(off, tk), :].astype(BF16)
        v = v_ref[pl.ds(off, tk), :].astype(BF16)
        w, c = _sb_block(q, k, v, (off + col < row) if masked else None, c, tri)
        return c, acc + _dot(w.astype(BF16), v)

    carry = (jnp.zeros((tq, 1), F32), jnp.zeros((tq, HEAD_DIM), F32))
    for d in range(per_q):
        carry = block((i + 1) * per_q - 1 - d, carry, True)
    _, acc = lax.fori_loop(0, i * per_q, lambda s, cr: block(i * per_q - 1 - s, cr, False), carry)
    o_ref[...] = acc


def _sb_prompt(qa, kva, bsz, t):
    tq, tk = 512, 256
    nq = t // tq
    return pl.pallas_call(
        functools.partial(_sb_prompt_kernel, tq=tq, tk=tk),
        grid=(bsz, N_HEADS_A, nq),
        in_specs=[pl.BlockSpec((tq, HEAD_DIM), lambda b, h, i: (b * nq + i, h)),
                  pl.BlockSpec((t, HEAD_DIM), lambda b, h, i: (b, h)),
                  pl.BlockSpec((t, HEAD_DIM), lambda b, h, i: (b, N_HEADS_A + h))],
        out_specs=pl.BlockSpec((tq, HEAD_DIM), lambda b, h, i: (b * nq + i, h)),
        out_shape=jax.ShapeDtypeStruct((bsz * t, D_A), F32),
        compiler_params=_params("parallel", "parallel", "parallel"),
        name="sb_prompt",
    )(qa, kva, kva)


SB_PAGES_PER_STEP = 8
PAD_T = 8


def _sb_decode_kernel(pt_ref, q_ref, new_ref, *refs, page):
    pages = refs[:SB_PAGES_PER_STEP]
    o_ref, c_ref, acc_ref = refs[SB_PAGES_PER_STEP:]
    p = pl.program_id(1)
    rows = N_HEADS_A * PAD_T
    rec = 2 * N_HEADS_A
    tri = _tri(page)
    qh = [q_ref[h].astype(BF16) for h in range(N_HEADS_A)]

    def scores(ref):
        return jnp.concatenate(
            [_dot_nt(qh[h], ref[pl.ds(h, page, stride=rec), :].astype(BF16)) for h in range(N_HEADS_A)],
            axis=0)

    def values(ref, w):
        wb = w.astype(BF16)
        return [_dot(wb[h * PAD_T:(h + 1) * PAD_T],
                     ref[pl.ds(N_HEADS_A + h, page, stride=rec), :].astype(BF16)) for h in range(N_HEADS_A)]

    @pl.when(p == 0)
    def _():
        t_of_row = lax.broadcasted_iota(jnp.int32, (rows, page), 0) % PAD_T
        mask = lax.broadcasted_iota(jnp.int32, (rows, page), 1) < t_of_row
        ls, lr = _sb_logs(scores(new_ref), mask)
        w = _sb_weights(ls, lr, mask, jnp.zeros((rows, 1), F32), tri)
        acc_ref[...] = jnp.concatenate(values(new_ref, w), axis=0)
        c_ref[...] = jnp.sum(lr, axis=1, keepdims=True)

    @pl.when(p > 0)
    def _():
        logs = [_sb_logs(scores(pg), None) for pg in pages]
        c = c_ref[...]
        acc = None
        for pg, (ls, lr) in zip(pages, logs):
            part = values(pg, _sb_weights(ls, lr, None, c, tri))
            acc = part if acc is None else [x + y for x, y in zip(acc, part)]
            c = c + jnp.sum(lr, axis=1, keepdims=True)
        c_ref[...] = c
        acc_ref[...] += jnp.concatenate(acc, axis=0)

    @pl.when(p == pl.num_programs(1) - 1)
    def _():
        o_ref[...] = acc_ref[...]


def _sb_decode(q_heads, new_rows, cache_rows, page_table, page):
    bsz = q_heads.shape[0]
    n_pages = page_table.shape[1]
    pps = SB_PAGES_PER_STEP
    steps = n_pages // pps
    rows = N_HEADS_A * PAD_T
    blk = page * 2 * N_HEADS_A
    page_specs = [
        pl.BlockSpec((blk, HEAD_DIM),
                     lambda b, p, pt, k=k: (pt[b, n_pages - 1 - (jnp.maximum(p - 1, 0) * pps + k)], 0))
        for k in range(pps)]
    grid_spec = pltpu.PrefetchScalarGridSpec(
        num_scalar_prefetch=1,
        grid=(bsz, steps + 1),
        in_specs=[pl.BlockSpec((None, N_HEADS_A, PAD_T, HEAD_DIM), lambda b, p, pt: (b, 0, 0, 0)),
                  pl.BlockSpec((None, blk, HEAD_DIM), lambda b, p, pt: (b, 0, 0))] + page_specs,
        out_specs=pl.BlockSpec((None, rows, HEAD_DIM), lambda b, p, pt: (b, 0, 0)),
        scratch_shapes=[pltpu.VMEM((rows, 1), F32), pltpu.VMEM((rows, HEAD_DIM), F32)],
    )
    return pl.pallas_call(
        functools.partial(_sb_decode_kernel, page=page),
        grid_spec=grid_spec,
        out_shape=jax.ShapeDtypeStruct((bsz, rows, HEAD_DIM), F32),
        compiler_params=_params("parallel", "arbitrary"),
        name="sb_decode",
    )(page_table, q_heads, new_rows, *([cache_rows] * pps))


def _compress_partial(load_rows, pe_ref, w1_ref, nchunk):
    half = CMP_LEN // 2
    a = jnp.zeros((nchunk, HEAD_DIM), F32)
    b = jnp.zeros((nchunk, HEAD_DIM), F32)
    for l in range(half):
        rows = load_rows(l)
        a = a + _dot((rows + pe_ref[l:l + 1, :]).astype(BF16), w1_ref[l])
        b = b + _dot((rows + pe_ref[half + l:half + l + 1, :]).astype(BF16), w1_ref[half + l])
    return a, b


def _compress_finish(a, b, w2):
    n = a.shape[0]
    pre = a + pltpu.roll(b, n - 1, 0)
    return _dot(jax.nn.gelu(pre).astype(BF16), w2)


def _compress_prompt_kernel(r_ref, pe_ref, w1_ref, w2_ref, o_ref, *, nchunk):
    load = lambda l: r_ref[pl.ds(l, nchunk, stride=CMP_STRIDE), :]
    a, b = _compress_partial(load, pe_ref, w1_ref, nchunk)
    o_ref[...] = _compress_finish(a, b, w2_ref[...])


def _compress_prompt(nsa, pe, w1, w2, bsz, t):
    nchunk = t // CMP_STRIDE
    return pl.pallas_call(
        functools.partial(_compress_prompt_kernel, nchunk=nchunk),
        grid=(bsz, 4),
        in_specs=[pl.BlockSpec((t, HEAD_DIM), lambda b, s: (b, s)),
                  pl.BlockSpec((None, CMP_LEN, HEAD_DIM), lambda b, s: (s // 2, 0, 0)),
                  pl.BlockSpec((None, CMP_LEN, HEAD_DIM, HEAD_DIM), lambda b, s: (s // 2, 0, 0, 0)),
                  pl.BlockSpec((None, HEAD_DIM, HEAD_DIM), lambda b, s: (s // 2, 0, 0))],
        out_specs=pl.BlockSpec((None, None, nchunk, HEAD_DIM), lambda b, s: (b, s, 0, 0)),
        out_shape=jax.ShapeDtypeStruct((bsz, 4, nchunk, HEAD_DIM), F32),
        compiler_params=_params("parallel", "parallel"),
        name="compress_prompt",
    )(nsa, pe, w1, w2)


PAGES_PER_STEP = 16


NSA_REC = 4 * N_KV_B


def _compress_pages_kernel(pt_ref, pe_ref, w1_ref, *refs, chunks_per_page):
    pages = refs[:PAGES_PER_STEP]
    a_ref, b_ref = refs[PAGES_PER_STEP:]
    nchunk = PAGES_PER_STEP * chunks_per_page
    for s in range(2 * N_KV_B):
        load = lambda l, s=s: jnp.concatenate(
            [pg[pl.ds(l * NSA_REC + s, chunks_per_page, stride=CMP_STRIDE * NSA_REC), :] for pg in pages],
            axis=0)
        a, b = _compress_partial(load, pe_ref.at[s // N_KV_B], w1_ref.at[s // N_KV_B], nchunk)
        a_ref[s] = a
        b_ref[s] = b


def _compress_pages(cache_rows, page_table, pe, w1, page):
    bsz, n_pages = page_table.shape
    cpp = page // CMP_STRIDE
    nchunk = PAGES_PER_STEP * cpp
    steps = n_pages // PAGES_PER_STEP
    page_specs = [
        pl.BlockSpec((page * NSA_REC, HEAD_DIM),
                     lambda b, s, pt, k=k: (pt[b, s * PAGES_PER_STEP + k], 0))
        for k in range(PAGES_PER_STEP)]
    out_spec = pl.BlockSpec((None, 2 * N_KV_B, nchunk, HEAD_DIM), lambda b, s, pt: (b, 0, s, 0))
    grid_spec = pltpu.PrefetchScalarGridSpec(
        num_scalar_prefetch=1,
        grid=(bsz, steps),
        in_specs=[pl.BlockSpec((2, CMP_LEN, HEAD_DIM), lambda b, s, pt: (0, 0, 0)),
                  pl.BlockSpec((2, CMP_LEN, HEAD_DIM, HEAD_DIM), lambda b, s, pt: (0, 0, 0, 0))]
                 + page_specs,
        out_specs=[out_spec, out_spec],
    )
    shape = jax.ShapeDtypeStruct((bsz, 2 * N_KV_B, n_pages * cpp, HEAD_DIM), F32)
    return pl.pallas_call(
        functools.partial(_compress_pages_kernel, chunks_per_page=cpp),
        grid_spec=grid_spec,
        out_shape=[shape, shape],
        compiler_params=_params("parallel", "parallel"),
        name="compress_pages",
    )(page_table, pe, w1, *([cache_rows] * PAGES_PER_STEP))


def _cmp_branch(qs, kcmp, vcmp, pos_rows):
    n = kcmp.shape[0]
    s = _dot_nt(qs, kcmp.astype(BF16)) * SCALE
    cmp_end = lax.broadcasted_iota(jnp.int32, (1, n), 1) * CMP_STRIDE + (CMP_LEN - 1)
    mask = cmp_end <= pos_rows
    s = jnp.where(mask, s, NEG_BIG)
    m = jnp.max(s, axis=-1, keepdims=True)
    e = jnp.where(mask, jnp.exp(s - m), 0.0)
    p = e / jnp.maximum(jnp.sum(e, axis=-1, keepdims=True), 1e-30)
    return _dot(p.astype(BF16), vcmp.astype(BF16)), p


def _select_blocks(imp, pos_rows, n_sel):
    r, width = imp.shape
    lane = lax.broadcasted_iota(jnp.int32, (r, width), 1)
    qblk = pos_rows // SLC_BLOCK
    forced = (lane == 0) | (lane == qblk) | (lane == qblk - 1)
    valid = lane * SLC_BLOCK <= pos_rows
    imp = jnp.where(forced, FORCE_SCORE, jnp.where(valid, imp, -FORCE_SCORE))
    imp = jnp.where(lane < n_sel, imp, -jnp.inf)
    rank = jnp.zeros((r, width), F32)
    for i in range(n_sel):
        ci = imp[:, i:i + 1]
        better = (ci > imp) | ((ci == imp) & (lane > i))
        rank = rank + jnp.where(better, 1.0, 0.0)
    return jnp.where((rank < float(min(N_SELECT, n_sel))) & (lane < n_sel), 1.0, 0.0)


def _softmax_step(s, mask, v, carry):
    m, l, acc = carry
    s = jnp.where(mask, s, NEG_BIG)
    m_new = jnp.maximum(m, jnp.max(s, axis=-1, keepdims=True))
    alpha = jnp.exp(m - m_new)
    e = jnp.where(mask, jnp.exp(s - m_new), 0.0)
    l = alpha * l + jnp.sum(e, axis=-1, keepdims=True)
    acc = alpha * acc + _dot(e.astype(BF16), v)
    return m_new, l, acc


def _softmax_init(r):
    return (jnp.full((r, 1), NEG_BIG, F32), jnp.zeros((r, 1), F32), jnp.zeros((r, HEAD_DIM), F32))


def _softmax_finish(carry):
    _, l, acc = carry
    return acc / jnp.maximum(l, 1e-30)


def _select_blocks_t(imp, pos, n_sel):
    rows, t = imp.shape
    blk = lax.broadcasted_iota(jnp.int32, (rows, t), 0)
    qblk = pos // SLC_BLOCK
    forced = (blk == 0) | (blk == qblk) | (blk == qblk - 1)
    valid = blk * SLC_BLOCK <= pos
    imp = jnp.where(forced, FORCE_SCORE, jnp.where(valid, imp, -FORCE_SCORE))
    imp = jnp.where(blk < n_sel, imp, -jnp.inf)
    rank = jnp.zeros((rows, t), F32)
    for i in range(n_sel):
        ri = imp[i:i + 1, :]
        better = (ri > imp) | ((ri == imp) & (blk > i))
        rank = rank + jnp.where(better, 1.0, 0.0)
    return jnp.where((rank < float(min(N_SELECT, n_sel))) & (blk < n_sel), 1.0, 0.0)


def _nsa_prompt_kernel(q_ref, g_ref, kc_ref, vc_ref, ks_ref, vs_ref, kw_ref, vw_ref,
                       mt_ref, e_ref, o_ref, *, tq, n_sel):
    i = pl.program_id(2)
    r = HPG * tq
    qs = jnp.concatenate([q_ref[:, h * HEAD_DIM:(h + 1) * HEAD_DIM] for h in range(HPG)],
                         axis=0).astype(BF16)
    pos_t = i * tq + lax.broadcasted_iota(jnp.int32, (tq, 1), 0)
    pos_r = jnp.concatenate([pos_t] * HPG, axis=0)

    o_c, p = _cmp_branch(qs, kc_ref[...], vc_ref[...], pos_r)
    psum = p[0:tq]
    for h in range(1, HPG):
        psum = psum + p[h * tq:(h + 1) * tq]
    hi = psum.astype(BF16)
    lo = (psum - hi.astype(F32)).astype(BF16)
    imp_t = _dot_nt(mt_ref[...], hi) + _dot_nt(mt_ref[...], lo)
    sel_rows = -(-n_sel // 8) * 8
    pos_lane = i * tq + lax.broadcasted_iota(jnp.int32, (1, tq), 1)
    sel_t = _select_blocks_t(imp_t[0:sel_rows], pos_lane, n_sel)
    sel = jnp.concatenate([sel_t, jnp.zeros((imp_t.shape[0] - sel_rows, tq), F32)], axis=0).T.astype(BF16)

    lane = lax.broadcasted_iota(jnp.int32, (r, tq), 1)

    def slc_body(c, carry):
        off = pl.multiple_of(c * tq, tq)
        k = ks_ref[pl.ds(off, tq), :].astype(BF16)
        v = vs_ref[pl.ds(off, tq), :].astype(BF16)
        hit = _dot(sel, e_ref[:, pl.ds(off, tq)])
        mask = (jnp.concatenate([hit] * HPG, axis=0) > 0.5) & (off + lane <= pos_r)
        return _softmax_step(_dot_nt(qs, k) * SCALE, mask, v, carry)

    o_s = _softmax_finish(lax.fori_loop(0, i + 1, slc_body, _softmax_init(r)))

    carry = _softmax_init(r)
    for d in range(WINDOW // tq, -1, -1):
        c = i - d
        off = pl.multiple_of(jnp.maximum(c, 0) * tq, tq)
        k = kw_ref[pl.ds(off, tq), :].astype(BF16)
        v = vw_ref[pl.ds(off, tq), :].astype(BF16)
        kpos = c * tq + lane
        dlt = pos_r - kpos
        mask = (dlt >= 0) & (dlt < WINDOW) & (kpos >= 0)
        carry = _softmax_step(_dot_nt(qs, k) * SCALE, mask, v, carry)
    o_w = _softmax_finish(carry)

    for h in range(HPG):
        rows = slice(h * tq, (h + 1) * tq)
        gc = g_ref[:, 3 * h:3 * h + 1]
        gs = g_ref[:, 3 * h + 1:3 * h + 2]
        gw = g_ref[:, 3 * h + 2:3 * h + 3]
        o_ref[:, h * HEAD_DIM:(h + 1) * HEAD_DIM] = gc * o_c[rows] + gs * o_s[rows] + gw * o_w[rows]


def _cmp_to_slc(n_cmp_pad, n_cmp, n_sel, width):
    i = np.arange(n_cmp_pad)[:, None]
    j = np.arange(width)[None, :]
    lo = np.maximum(i * CMP_STRIDE, j * SLC_BLOCK)
    hi = np.minimum(i * CMP_STRIDE + CMP_LEN, (j + 1) * SLC_BLOCK)
    m = np.maximum(hi - lo, 0) / CMP_STRIDE
    m = np.where((i < n_cmp) & (j < n_sel), m, 0.0)
    return jnp.asarray(m, dtype=BF16)


def _nsa_prompt(qb, gates, cmp_kv, nsa, win, bsz, t):
    tq = 256
    nq = t // tq
    n_cmp = (t - CMP_LEN) // CMP_STRIDE + 1
    n_cmp_pad = cmp_kv.shape[2]
    n_sel = -(-t // SLC_BLOCK)
    assert n_sel <= LANES
    m_mat = _cmp_to_slc(n_cmp_pad, n_cmp, n_sel, LANES).T
    e_mat = jnp.asarray(np.arange(LANES)[:, None] == (np.arange(t) // SLC_BLOCK)[None, :], dtype=BF16)
    full = lambda shape: pl.BlockSpec(shape, lambda b, g, i: (0,) * len(shape))
    rows = lambda col: pl.BlockSpec((t, HEAD_DIM), lambda b, g, i, col=col: (b, col + g))
    return pl.pallas_call(
        functools.partial(_nsa_prompt_kernel, tq=tq, n_sel=n_sel),
        grid=(bsz, N_KV_B, nq),
        in_specs=[pl.BlockSpec((tq, HPG * HEAD_DIM), lambda b, g, i: (b * nq + i, g)),
                  pl.BlockSpec((tq, LANES), lambda b, g, i: (b * nq + i, g)),
                  pl.BlockSpec((None, None, n_cmp_pad, HEAD_DIM), lambda b, g, i: (b, g, 0, 0)),
                  pl.BlockSpec((None, None, n_cmp_pad, HEAD_DIM), lambda b, g, i: (b, 2 + g, 0, 0)),
                  rows(4), rows(6), rows(0), rows(2),
                  full(m_mat.shape), full(e_mat.shape)],
        out_specs=pl.BlockSpec((tq, HPG * HEAD_DIM), lambda b, g, i: (b * nq + i, g)),
        out_shape=jax.ShapeDtypeStruct((bsz * t, D_B), F32),
        compiler_params=_params("parallel", "parallel", "parallel"),
        name="nsa_prompt",
    )(qb, gates, cmp_kv, cmp_kv, nsa, nsa, win, win, m_mat, e_mat)


def _nsa_select_kernel(q_ref, ak_ref, av_ref, bk_ref, bv_ref, w2_ref, m_ref, oc_ref, sel_ref,
                       *, n_new, past, n_sel):
    qs = jnp.concatenate([q_ref[:, h * HEAD_DIM:(h + 1) * HEAD_DIM] for h in range(HPG)],
                         axis=0).astype(BF16)
    pos_t = past + lax.broadcasted_iota(jnp.int32, (n_new, 1), 0)
    pos_r = jnp.concatenate([pos_t] * HPG, axis=0)
    kcmp = _compress_finish(ak_ref[...], bk_ref[...], w2_ref[0])
    vcmp = _compress_finish(av_ref[...], bv_ref[...], w2_ref[1])
    o_c, p = _cmp_branch(qs, kcmp, vcmp, pos_r)
    oc_ref[...] = o_c
    psum = p[0:n_new]
    for h in range(1, HPG):
        psum = psum + p[h * n_new:(h + 1) * n_new]
    sel = _select_blocks(_dot_split(psum, m_ref[...]), pos_t, n_sel)
    sel_ref[...] = jnp.concatenate([sel] * HPG, axis=0)


def _nsa_select(qb, part_a, part_b, w2, past):
    bsz, n_new, _ = qb.shape
    n_cmp_pad = part_a.shape[2]
    total = past + n_new
    n_cmp = (total - CMP_LEN) // CMP_STRIDE + 1
    n_sel = -(-total // SLC_BLOCK)
    width = -(-n_sel // LANES) * LANES
    m_mat = _cmp_to_slc(n_cmp_pad, n_cmp, n_sel, width)
    assert n_cmp == n_cmp_pad - 1, "compressed tokens must come from the paged rows only"
    r = HPG * n_new
    k_spec = pl.BlockSpec((None, None, n_cmp_pad, HEAD_DIM), lambda b, g: (b, g, 0, 0))
    v_spec = pl.BlockSpec((None, None, n_cmp_pad, HEAD_DIM), lambda b, g: (b, N_KV_B + g, 0, 0))
    return pl.pallas_call(
        functools.partial(_nsa_select_kernel, n_new=n_new, past=past, n_sel=n_sel),
        grid=(bsz, N_KV_B),
        in_specs=[pl.BlockSpec((None, n_new, HPG * HEAD_DIM), lambda b, g: (b, 0, g)),
                  k_spec, v_spec, k_spec, v_spec,
                  pl.BlockSpec((2, HEAD_DIM, HEAD_DIM), lambda b, g: (0, 0, 0)),
                  pl.BlockSpec(m_mat.shape, lambda b, g: (0, 0))],
        out_specs=[pl.BlockSpec((None, None, r, HEAD_DIM), lambda b, g: (b, g, 0, 0)),
                   pl.BlockSpec((None, None, r, width), lambda b, g: (b, g, 0, 0))],
        out_shape=[jax.ShapeDtypeStruct((bsz, N_KV_B, r, HEAD_DIM), F32),
                   jax.ShapeDtypeStruct((bsz, N_KV_B, r, width), F32)],
        compiler_params=_params("parallel", "parallel"),
        name="nsa_select",
    )(qb, part_a, part_a, part_b, part_b, w2, m_mat)


SLC_PAGES_PER_STEP = 8


def _nsa_slc_decode_kernel(pt_ref, q_ref, sel_ref, new_ref, *refs, n_new, past, page):
    pps = SLC_PAGES_PER_STEP
    pages = refs[:pps]
    o_ref, m_ref, l_ref, acc_ref = refs[pps:]
    p = pl.program_id(1)
    n_steps = pl.num_programs(1) - 1
    r = HPG * n_new
    rows = N_KV_B * r
    t_of_row = lax.broadcasted_iota(jnp.int32, (rows, 1), 0) % n_new
    lane = lax.broadcasted_iota(jnp.int32, (rows, page), 1)
    bpp = page // SLC_BLOCK
    qg = [q_ref[g].astype(BF16) for g in range(N_KV_B)]
    sel = jnp.concatenate([sel_ref[g] for g in range(N_KV_B)], axis=0)
    blk_lane = lax.broadcasted_iota(jnp.int32, sel.shape, 1)

    @pl.when(p == 0)
    def _():
        m_ref[...] = jnp.full(m_ref.shape, NEG_BIG, F32)
        l_ref[...] = jnp.zeros(l_ref.shape, F32)
        acc_ref[...] = jnp.zeros(acc_ref.shape, F32)

    def masked_scores(ref, first_pos):
        first_blk = first_pos // SLC_BLOCK
        hit = jnp.zeros((rows, page), F32)
        for j in range(bpp):
            flag = jnp.sum(jnp.where(blk_lane == first_blk + j, sel, 0.0), axis=-1, keepdims=True)
            hit = jnp.where(lane // SLC_BLOCK == j, flag, hit)
        mask = (hit > 0.5) & (first_pos + lane <= past + t_of_row)
        s = jnp.concatenate(
            [_dot_nt(qg[g], ref[pl.ds(2 * N_KV_B + g, page, stride=NSA_REC), :].astype(BF16))
             for g in range(N_KV_B)], axis=0) * SCALE
        return jnp.where(mask, s, NEG_BIG), mask

    def visit(blocks):
        scored = [masked_scores(ref, pos) for ref, pos in blocks]
        m_old = m_ref[...]
        m_new = m_old
        for s, _ in scored:
            m_new = jnp.maximum(m_new, jnp.max(s, axis=-1, keepdims=True))
        alpha = jnp.exp(m_old - m_new)
        l = alpha * l_ref[...]
        acc = alpha * acc_ref[...]
        for (ref, _), (s, mask) in zip(blocks, scored):
            e = jnp.where(mask, jnp.exp(s - m_new), 0.0)
            l = l + jnp.sum(e, axis=-1, keepdims=True)
            eb = e.astype(BF16)
            acc = acc + jnp.concatenate(
                [_dot(eb[g * r:(g + 1) * r], ref[pl.ds(3 * N_KV_B + g, page, stride=NSA_REC), :].astype(BF16))
                 for g in range(N_KV_B)], axis=0)
        m_ref[...] = m_new
        l_ref[...] = l
        acc_ref[...] = acc

    @pl.when(p < n_steps)
    def _():
        visit([(pg, (p * pps + k) * page) for k, pg in enumerate(pages)])

    @pl.when(p == n_steps)
    def _():
        visit([(new_ref, past)])
        o_ref[...] = acc_ref[...] / jnp.maximum(l_ref[...], 1e-30)


def _nsa_slc_decode(q_groups, sel, new_rows, cache_rows, page_table, past, page, n_new):
    bsz = q_groups.shape[0]
    n_pages = page_table.shape[1]
    pps = SLC_PAGES_PER_STEP
    steps = n_pages // pps
    r = HPG * n_new
    rows = N_KV_B * r
    width = sel.shape[-1]
    blk = page * NSA_REC
    page_specs = [
        pl.BlockSpec((blk, HEAD_DIM),
                     lambda b, p, pt, k=k: (pt[b, jnp.minimum(p, steps - 1) * pps + k], 0))
        for k in range(pps)]
    grid_spec = pltpu.PrefetchScalarGridSpec(
        num_scalar_prefetch=1,
        grid=(bsz, steps + 1),
        in_specs=[pl.BlockSpec((None, N_KV_B, r, HEAD_DIM), lambda b, p, pt: (b, 0, 0, 0)),
                  pl.BlockSpec((None, N_KV_B, r, width), lambda b, p, pt: (b, 0, 0, 0)),
                  pl.BlockSpec((None, blk, HEAD_DIM), lambda b, p, pt: (b, 0, 0))] + page_specs,
        out_specs=pl.BlockSpec((None, rows, HEAD_DIM), lambda b, p, pt: (b, 0, 0)),
        scratch_shapes=[pltpu.VMEM((rows, 1), F32), pltpu.VMEM((rows, 1), F32),
                        pltpu.VMEM((rows, HEAD_DIM), F32)],
    )
    return pl.pallas_call(
        functools.partial(_nsa_slc_decode_kernel, n_new=n_new, past=past, page=page),
        grid_spec=grid_spec,
        out_shape=jax.ShapeDtypeStruct((bsz, rows, HEAD_DIM), F32),
        compiler_params=_params("parallel", "arbitrary"),
        name="nsa_slc_decode",
    )(page_table, q_groups, sel, new_rows, *([cache_rows] * pps))


def _nsa_merge_decode_kernel(q_ref, g_ref, oc_ref, os_ref, st_ref, new_ref, o_ref, nw_ref,
                             *, n_new, past, n_buf):
    r = HPG * n_new
    n_pad = new_ref.shape[0]
    t_of_row = lax.broadcasted_iota(jnp.int32, (r, 1), 0) % n_new
    pos_r = past + t_of_row
    buf_pos = past - n_buf + lax.broadcasted_iota(jnp.int32, (r, n_buf), 1)
    new_pos = past + lax.broadcasted_iota(jnp.int32, (r, n_pad), 1)

    def in_window(kpos):
        dlt = pos_r - kpos
        return (dlt >= 0) & (dlt < WINDOW) & (kpos >= 0)

    for g in range(N_KV_B):
        qs = jnp.concatenate(
            [q_ref[:, (g * HPG + h) * HEAD_DIM:(g * HPG + h + 1) * HEAD_DIM] for h in range(HPG)],
            axis=0).astype(BF16)
        kcol = slice(g * HEAD_DIM, (g + 1) * HEAD_DIM)
        vcol = slice((N_KV_B + g) * HEAD_DIM, (N_KV_B + g + 1) * HEAD_DIM)
        carry = _softmax_init(r)
        carry = _softmax_step(_dot_nt(qs, st_ref[:, kcol].astype(BF16)) * SCALE, in_window(buf_pos),
                              st_ref[:, vcol].astype(BF16), carry)
        carry = _softmax_step(_dot_nt(qs, new_ref[:, kcol].astype(BF16)) * SCALE, in_window(new_pos),
                              new_ref[:, vcol].astype(BF16), carry)
        o_w = _softmax_finish(carry)
        for h in range(HPG):
            rows = slice(h * n_new, (h + 1) * n_new)
            c0 = g * LANES + 3 * h
            out = (g_ref[:, c0:c0 + 1] * oc_ref[g, rows] + g_ref[:, c0 + 1:c0 + 2] * os_ref[g, rows]
                   + g_ref[:, c0 + 2:c0 + 3] * o_w[rows])
            o_ref[:, (g * HPG + h) * HEAD_DIM:(g * HPG + h + 1) * HEAD_DIM] = out
    shifted = pltpu.roll(st_ref[...], n_buf - n_new, 0)
    placed = pltpu.roll(new_ref[...], n_pad - n_new, 0)
    tail_row = lax.broadcasted_iota(jnp.int32, (n_pad, placed.shape[1]), 0)
    nw_ref[0:n_buf - n_pad, :] = shifted[0:n_buf - n_pad]
    nw_ref[n_buf - n_pad:n_buf, :] = jnp.where(tail_row >= n_pad - n_new, placed, shifted[n_buf - n_pad:])


def _nsa_merge_decode(qb, gates, o_cmp, o_slc, state_win, win_new, past):
    bsz, n_new, _ = qb.shape
    n_buf = state_win.shape[1]
    n_pad = win_new.shape[1]
    keep = min(WINDOW, past + n_new)
    assert keep == n_buf and n_buf > n_pad >= n_new
    r = HPG * n_new
    cols = 2 * D_KV_B
    per_b = lambda shape: pl.BlockSpec((None,) + shape, lambda b: (b,) + (0,) * len(shape))
    return pl.pallas_call(
        functools.partial(_nsa_merge_decode_kernel, n_new=n_new, past=past, n_buf=n_buf),
        grid=(bsz,),
        in_specs=[per_b((n_new, D_B)), per_b((n_new, 2 * LANES)),
                  per_b((N_KV_B, r, HEAD_DIM)), per_b((N_KV_B, r, HEAD_DIM)),
                  per_b((n_buf, cols)), per_b((n_pad, cols))],
        out_specs=[per_b((n_new, D_B)), per_b((keep, cols))],
        out_shape=[jax.ShapeDtypeStruct((bsz, n_new, D_B), F32),
                   jax.ShapeDtypeStruct((bsz, keep, cols), F32)],
        compiler_params=_params("parallel"),
        name="nsa_merge_decode",
    )(qb, gates, o_cmp, o_slc, state_win, win_new)


def _outproj_kernel(x_ref, oa_ref, ob_ref, wa_ref, wb_ref, g_ref, b_ref, h_ref):
    y = _dot(oa_ref[...].astype(BF16), wa_ref[...]) + _dot(ob_ref[...].astype(BF16), wb_ref[...])
    h_ref[...] = _layer_norm(ALPHA * x_ref[...] + y, g_ref[...], b_ref[...])


def _outproj(x, o_a, o_b, wa, wb, g, b):
    m, d = x.shape
    tm = min(512, m)
    row = lambda w: pl.BlockSpec((tm, w), lambda i: (i, 0))
    full = lambda a: pl.BlockSpec(a.shape, lambda i: (0, 0))
    return pl.pallas_call(
        _outproj_kernel,
        grid=(m // tm,),
        in_specs=[row(d), row(D_A), row(D_B), full(wa), full(wb), full(g), full(b)],
        out_specs=row(d),
        out_shape=jax.ShapeDtypeStruct((m, d), F32),
        compiler_params=_params("parallel"),
        name="outproj_ln",
    )(x, o_a, o_b, wa, wb, g, b)


def _peer_query_kernel(h_ref, w_ref, q_ref):
    q_ref[...] = _dot(h_ref[...].astype(BF16), w_ref[...])


def _peer_query(h, wq):
    m, d = h.shape
    n = wq.shape[1]
    tm = min(512, m)
    return pl.pallas_call(
        _peer_query_kernel,
        grid=(m // tm,),
        in_specs=[pl.BlockSpec((tm, d), lambda i: (i, 0)), pl.BlockSpec((d, n), lambda i: (0, 0))],
        out_specs=pl.BlockSpec((tm, n), lambda i: (i, 0)),
        out_shape=jax.ShapeDtypeStruct((m, n), F32),
        compiler_params=_params("parallel"),
        name="peer_query",
    )(h, wq)


def _top_values(x, k):
    n = x.shape[0]
    row = lax.broadcasted_iota(jnp.int32, x.shape, 0).astype(F32)
    out = []
    for _ in range(k):
        m = jnp.max(x, axis=0, keepdims=True)
        first = jnp.min(jnp.where(x == m, row, float(n)), axis=0, keepdims=True)
        x = jnp.where(row == first, -jnp.inf, x)
        out.append(m)
    return out


_PEER_PAIRS = [(i, j) for i in range(PEER_TOPK) for j in range(PEER_TOPK) if (i + 1) * (j + 1) <= PEER_TOPK]
_PEER_CAND_ROWS = -(-len(_PEER_PAIRS) // 8) * 8


def _peer_route_kernel(q_ref, sk_ref, thr_ref, s2_ref, e1_ref, e2_ref, cand_ref):
    half = sk_ref.shape[2]
    cand_ref[...] = jnp.full(cand_ref.shape, -jnp.inf, F32)
    for h in range(PEER_HEADS):
        s, tops = [], []
        for c in range(2):
            col = (2 * h + c) * half
            st = _dot_nt(sk_ref[c], q_ref[:, col:col + half].astype(BF16))
            s.append(st)
            tops.append(_top_values(st, PEER_TOPK))
        for n, (i, j) in enumerate(_PEER_PAIRS):
            cand_ref[n:n + 1, :] = tops[0][i] + tops[1][j]
        best = _top_values(cand_ref[...], PEER_TOPK)
        top = best[0]
        z = jnp.ones_like(top)
        for v in best[1:]:
            z = z + jnp.exp(v - top)
        tau = best[PEER_TOPK - 1]
        thr = jnp.full(s[0].shape, jnp.inf, F32)
        for i in reversed(range(PEER_TOPK)):
            thr_i = jnp.full_like(tau, jnp.inf)
            for j in range(PEER_TOPK):
                if (i + 1) * (j + 1) <= PEER_TOPK:
                    thr_i = jnp.where(tops[0][i] + tops[1][j] >= tau, tops[1][j], thr_i)
            thr = jnp.where(s[0] == tops[0][i], thr_i, thr)
        thr_ref[h] = thr
        s2_ref[h] = s[1]
        e1_ref[h] = jnp.exp(s[0] - tops[0][0])
        e2_ref[h] = jnp.exp(s[1] - tops[1][0]) / z


def _peer_route(q, sub_keys):
    n = q.shape[0]
    tm = 128
    nk = sub_keys.shape[1]
    big = jax.ShapeDtypeStruct((PEER_HEADS, nk, n), F32)
    big_spec = pl.BlockSpec((PEER_HEADS, nk, tm), lambda i: (0, 0, i))
    return pl.pallas_call(
        _peer_route_kernel,
        grid=(n // tm,),
        in_specs=[pl.BlockSpec((tm, q.shape[1]), lambda i: (i, 0)),
                  pl.BlockSpec(sub_keys.shape, lambda i: (0, 0, 0))],
        out_specs=[big_spec] * 4,
        out_shape=[big] * 4,
        scratch_shapes=[pltpu.VMEM((_PEER_CAND_ROWS, tm), F32)],
        compiler_params=_params("parallel"),
        name="peer_route",
    )(q, sub_keys)


def _peer_dense_kernel(h_ref, u_ref, v_ref, thr_ref, s2_ref, e1_ref, e2_ref, g_ref, b_ref,
                       o_ref, hb_ref, wt_ref, *, tm, ac, sub_ac):
    c = pl.program_id(1)
    nk = s2_ref.shape[1]
    kb_rows = nk // 4

    @pl.when(c == 0)
    def _():
        o_ref[...] = jnp.zeros(o_ref.shape, F32)
        hb_ref[...] = h_ref[...].astype(BF16)

    for sub in range(ac // sub_ac):
        ex = slice(sub * sub_ac * nk, (sub + 1) * sub_ac * nk)
        act = jax.nn.gelu(_dot_nt(u_ref[ex, :], hb_ref[...]))
        for ts in range(tm // LANES):
            tok = slice(ts * LANES, (ts + 1) * LANES)
            for kb in range(nk // kb_rows):
                keys = slice(kb * kb_rows, (kb + 1) * kb_rows)
                gates = [jnp.zeros((kb_rows, LANES), F32) for _ in range(sub_ac)]
                for h in range(PEER_HEADS):
                    s2 = s2_ref[h, keys, tok]
                    e2 = e2_ref[h, keys, tok]
                    for k in range(sub_ac):
                        aa = sub * sub_ac + k
                        chosen = s2 >= thr_ref[h, aa:aa + 1, tok]
                        gates[k] = gates[k] + jnp.where(chosen, e1_ref[h, aa:aa + 1, tok] * e2, 0.0)
                for k in range(sub_ac):
                    rows = slice(k * nk + kb * kb_rows, k * nk + (kb + 1) * kb_rows)
                    wt_ref[sub * sub_ac * nk + rows.start:sub * sub_ac * nk + rows.stop, tok] = (
                        gates[k] * act[rows, tok]).astype(BF16)
    o_ref[...] += _dot_tn(wt_ref[...], v_ref[...])

    @pl.when(c == pl.num_programs(1) - 1)
    def _():
        o_ref[...] = _layer_norm(ALPHA * h_ref[...] + o_ref[...], g_ref[...], b_ref[...])


def _peer_dense(h, u, v, route, g, b):
    n, d = h.shape
    thr, s2, e1, e2 = route
    nk = s2.shape[1]
    tm = min(512, n)
    ac = 8
    big_spec = pl.BlockSpec((PEER_HEADS, nk, tm), lambda i, c: (0, 0, i))
    row_spec = pl.BlockSpec((PEER_HEADS, ac, tm), lambda i, c: (0, c, i))
    tab_spec = pl.BlockSpec((ac * nk, d), lambda i, c: (c, 0))
    full = lambda a: pl.BlockSpec(a.shape, lambda i, c: (0, 0))
    return pl.pallas_call(
        functools.partial(_peer_dense_kernel, tm=tm, ac=ac, sub_ac=4),
        grid=(n // tm, nk // ac),
        in_specs=[pl.BlockSpec((tm, d), lambda i, c: (i, 0)), tab_spec, tab_spec,
                  row_spec, big_spec, row_spec, big_spec, full(g), full(b)],
        out_specs=pl.BlockSpec((tm, d), lambda i, c: (i, 0)),
        out_shape=jax.ShapeDtypeStruct((n, d), F32),
        scratch_shapes=[pltpu.VMEM((tm, d), BF16), pltpu.VMEM((ac * nk, tm), BF16)],
        compiler_params=_params("parallel", "arbitrary"),
        name="peer_dense",
    )(h, u, v, thr, s2, e1, e2, g, b)


def _post_block(x, o_a, o_b, wts):
    h = _outproj(x, o_a, o_b, wts["wo_a"], wts["wo_b"], wts["ln1_g"], wts["ln1_b"])
    q = _peer_query(h, wts["wq"])
    route = _peer_route(q, wts["sub_keys"])
    return _peer_dense(h, wts["u"], wts["v"], route, wts["ln2_g"], wts["ln2_b"])


def _pad_rows(a, rows):
    return jnp.pad(a, ((0, rows - a.shape[0]),) + ((0, 0),) * (a.ndim - 1))


def kernel(x_prompt, x_sample, cache_a, cache_nsa, state_win, page_table, w_in, cmp_pe_k, cmp_w1_k,
           cmp_w2_k, cmp_pe_v, cmp_w1_v, cmp_w2_v, w_out, ln1_g, ln1_b, peer_w_query, peer_sub_keys,
           peer_u, peer_v, ln2_g, ln2_b):
    assert w_in.shape[0] == DEPTH == 1
    bsz, t, d = x_prompt.shape
    dbsz, n_new, _ = x_sample.shape
    n_pool, page = cache_a.shape[1], cache_a.shape[2]
    past = page_table.shape[1] * page

    w = w_in[0]
    o = np.cumsum((0, D_A, 2 * D_A, D_B, 4 * D_KV_B, 2 * D_KV_B, 3 * N_HEADS_B))
    wg = w[:, o[5]:o[6]]
    per_g = 3 * HPG
    gate_w = jnp.concatenate(
        [jnp.pad(wg[:, g * per_g:(g + 1) * per_g], ((0, 0), (0, LANES - per_g))) for g in range(N_KV_B)],
        axis=1)
    wparts = {"qa": w[:, o[0]:o[1]], "kva": w[:, o[1]:o[2]], "qb": w[:, o[2]:o[3]],
              "nsa": w[:, o[3]:o[4]], "win": w[:, o[4]:o[5]], "gates": gate_w}
    wparts = {k: v.astype(BF16) for k, v in wparts.items()}
    row2 = lambda a: a[0].reshape(1, -1)
    wts = {"wo_a": w_out[0, :D_A].astype(BF16), "wo_b": w_out[0, D_A:].astype(BF16),
           "ln1_g": row2(ln1_g), "ln1_b": row2(ln1_b), "ln2_g": row2(ln2_g), "ln2_b": row2(ln2_b),
           "wq": peer_w_query[0].astype(BF16), "sub_keys": peer_sub_keys[0].astype(BF16),
           "u": peer_u[0].astype(BF16), "v": peer_v[0].astype(BF16)}
    pe = jnp.stack([cmp_pe_k[0], cmp_pe_v[0]])
    w1 = jnp.stack([cmp_w1_k[0], cmp_w1_v[0]]).astype(BF16)
    w2 = jnp.stack([cmp_w2_k[0], cmp_w2_v[0]]).astype(BF16)

    xp = x_prompt.reshape(bsz * t, d)
    tabs_p = _rope_tables(jnp.arange(t, dtype=jnp.int32))
    qa, kva, qb, nsa, win, gates = _project_all(xp.astype(BF16), wparts, tabs_p)
    o_a = _sb_prompt(qa, kva, bsz, t)
    cmp_kv = _compress_prompt(nsa, pe, w1, w2, bsz, t)
    o_b = _nsa_prompt(qb, gates, cmp_kv, nsa, win, bsz, t)
    y_prompt = _post_block(xp, o_a, o_b, wts).reshape(bsz, t, d)
    new_a_p = kva.reshape(1, bsz, t, 2, N_HEADS_A, HEAD_DIM)
    new_nsa_p = nsa.reshape(1, bsz, t, 4, N_KV_B, HEAD_DIM)
    keep_p = min(WINDOW, t)
    new_win_p = win.reshape(bsz, t, 2, N_KV_B, HEAD_DIM)[None, :, t - keep_p:]

    ns = dbsz * n_new
    xs = x_sample.reshape(ns, d)
    pos_s = past + jnp.arange(n_new, dtype=jnp.int32)
    tabs_s = tuple(jnp.tile(tb, (dbsz, 1)) for tb in _rope_tables(pos_s))
    qa_s, kva_s, qb_s, nsa_s, win_s, gates_s = _project_all(xs.astype(BF16), wparts, tabs_s)
    by_b = lambda a: a.reshape(dbsz, n_new, a.shape[-1])
    pad_page = lambda a: jnp.pad(by_b(a), ((0, 0), (0, page - n_new), (0, 0)))
    as_records = lambda a, rec: jnp.pad(a.reshape(dbsz, n_new * rec, HEAD_DIM),
                                        ((0, 0), (0, (page - n_new) * rec), (0, 0)))
    q_heads = jnp.pad(qa_s.reshape(dbsz, n_new, N_HEADS_A, HEAD_DIM).transpose(0, 2, 1, 3),
                      ((0, 0), (0, 0), (0, PAD_T - n_new), (0, 0)))
    o_a_h = _sb_decode(q_heads, as_records(kva_s, 2 * N_HEADS_A),
                       cache_a.reshape(n_pool * page * 2 * N_HEADS_A, HEAD_DIM), page_table, page)
    o_a_s = o_a_h.reshape(dbsz, N_HEADS_A, PAD_T, HEAD_DIM)[:, :, :n_new].transpose(0, 2, 1, 3)
    cache_n = cache_nsa.reshape(n_pool * page * NSA_REC, HEAD_DIM)
    part_a, part_b = _compress_pages(cache_n, page_table, pe, w1, page)
    o_cmp, sel = _nsa_select(by_b(qb_s), part_a, part_b, w2, past)
    q_groups = qb_s.reshape(dbsz, n_new, N_KV_B, HPG, HEAD_DIM).transpose(0, 2, 3, 1, 4).reshape(
        dbsz, N_KV_B, HPG * n_new, HEAD_DIM)
    o_slc = _nsa_slc_decode(q_groups, sel, as_records(nsa_s, NSA_REC), cache_n, page_table, past, page,
                            n_new).reshape(dbsz, N_KV_B, HPG * n_new, HEAD_DIM)
    n_buf = state_win.shape[2]
    o_b_s, new_win = _nsa_merge_decode(by_b(qb_s), by_b(gates_s), o_cmp, o_slc,
                                       state_win[0].reshape(dbsz, n_buf, 2 * D_KV_B), pad_page(win_s), past)
    rows_s = -(-ns // LANES) * LANES
    y_s = _post_block(_pad_rows(xs, rows_s), _pad_rows(o_a_s.reshape(ns, D_A), rows_s),
                      _pad_rows(o_b_s.reshape(ns, D_B), rows_s), wts)
    y_sample = y_s[:ns].reshape(dbsz, n_new, d)
    new_a_s = kva_s.reshape(1, dbsz, n_new, 2, N_HEADS_A, HEAD_DIM)
    new_nsa_s = nsa_s.reshape(1, dbsz, n_new, 4, N_KV_B, HEAD_DIM)
    new_win_s = new_win.reshape(1, dbsz, new_win.shape[1], 2, N_KV_B, HEAD_DIM)
    return (y_prompt, y_sample, new_a_p, new_nsa_p, new_win_p, new_a_s, new_nsa_s, new_win_s)
```

```python
import functools
import math

import numpy as np
import jax
import jax.numpy as jnp
from jax import lax
from jax.experimental import pallas as pl
from jax.experimental.pallas import tpu as pltpu

F32 = jnp.float32
BF16 = jnp.bfloat16

LANES = 128
SUBLANES = 8
VMEM_LIMIT = 56 * 1024 * 1024

HEAD_DIM = 128
N_HEADS_A = 8
N_HEADS_B = 8
N_KV_B = 2
HPG = N_HEADS_B // N_KV_B
D_A = N_HEADS_A * HEAD_DIM
D_B = N_HEADS_B * HEAD_DIM
D_KV_B = N_KV_B * HEAD_DIM
ROPE_DIM = HEAD_DIM // 4
ROPE_THETA = 500000.0
CMP_LEN = 32
CMP_STRIDE = 16
SLC_BLOCK = 64
N_SELECT = 16
WINDOW = 512
FORCE_SCORE = 1.0e4
PEER_HEADS = 8
PEER_NKEYS = 128
PEER_TOPK = 16
DEPTH = 1
ALPHA = (2 * DEPTH) ** 0.25
LN_EPS = 1e-5
NEG_BIG = -1e30
SCALE = 1.0 / math.sqrt(HEAD_DIM)


def _dot(a, b):
    return jnp.dot(a, b, preferred_element_type=F32)


def _dot_nt(a, b):
    return lax.dot_general(a, b, (((1,), (1,)), ((), ())), preferred_element_type=F32)


def _dot_tn(a, b):
    return lax.dot_general(a, b, (((0,), (0,)), ((), ())), preferred_element_type=F32)


def _dot_split(x, w):
    hi = x.astype(BF16)
    lo = (x - hi.astype(F32)).astype(BF16)
    return _dot(hi, w) + _dot(lo, w)


def _params(*sem):
    return pltpu.CompilerParams(dimension_semantics=sem, vmem_limit_bytes=VMEM_LIMIT)


def _layer_norm(r, g, b):
    mu = jnp.mean(r, axis=-1, keepdims=True)
    d = r - mu
    var = jnp.mean(d * d, axis=-1, keepdims=True)
    return d * lax.rsqrt(var + LN_EPS) * g + b


def _inproj_kernel(x_ref, w_ref, c_ref, sa_ref, sb_ref, o_ref, *, rope_flags, sigmoid):
    acc = _dot(x_ref[...], w_ref[...])
    for j, flag in enumerate(rope_flags):
        blk = acc[:, j * LANES:(j + 1) * LANES]
        if flag:
            blk = (blk * c_ref[...]
                   + pltpu.roll(blk, LANES - ROPE_DIM // 2, 1) * sa_ref[...]
                   + pltpu.roll(blk, ROPE_DIM // 2, 1) * sb_ref[...])
        if sigmoid:
            blk = jax.nn.sigmoid(blk)
        o_ref[:, j * LANES:(j + 1) * LANES] = blk


def _inproj(xb, w, tabs, rope_flags, sigmoid=False):
    m, k = xb.shape
    n = w.shape[1]
    c, sa, sb = tabs
    tm = min(1024, m)
    nt = c.shape[0] // tm
    tab_spec = pl.BlockSpec((tm, LANES), lambda i: (i % nt, 0))
    return pl.pallas_call(
        functools.partial(_inproj_kernel, rope_flags=tuple(rope_flags), sigmoid=sigmoid),
        grid=(m // tm,),
        in_specs=[pl.BlockSpec((tm, k), lambda i: (i, 0)),
                  pl.BlockSpec((k, n), lambda i: (0, 0)),
                  tab_spec, tab_spec, tab_spec],
        out_specs=pl.BlockSpec((tm, n), lambda i: (i, 0)),
        out_shape=jax.ShapeDtypeStruct((m, n), F32),
        compiler_params=_params("parallel"),
        name="inproj",
    )(xb, w, c, sa, sb)


def _rope_tables(pos):
    half = ROPE_DIM // 2
    inv = ROPE_THETA ** (-jnp.arange(half, dtype=F32) / half)
    ang = pos.astype(F32)[:, None] * inv[None, :]
    cos, sin = jnp.cos(ang), jnp.sin(ang)
    t = pos.shape[0]
    ones = jnp.ones((t, LANES - ROPE_DIM), F32)
    zeros = jnp.zeros((t, LANES - half), F32)
    c = jnp.concatenate([cos, cos, ones], axis=1)
    sa = jnp.concatenate([-sin, zeros], axis=1)
    sb = jnp.concatenate([jnp.zeros((t, half), F32), sin, jnp.zeros((t, LANES - ROPE_DIM), F32)], axis=1)
    return c, sa, sb


def _project_all(xb, wparts, tabs):
    qa = _inproj(xb, wparts["qa"], tabs, [0] * 8)
    kva = _inproj(xb, wparts["kva"], tabs, [0] * 16)
    qb = _inproj(xb, wparts["qb"], tabs, [1] * 8)
    nsa = _inproj(xb, wparts["nsa"], tabs, [1, 1, 0, 0, 1, 1, 0, 0])
    win = _inproj(xb, wparts["win"], tabs, [1, 1, 0, 0])
    gates = _inproj(xb, wparts["gates"], tabs, [0, 0], sigmoid=True)
    return qa, kva, qb, nsa, win, gates


def _sb_block(q, k, v, mask, c, tri):
    ls, lr = _sb_logs(_dot_nt(q, k), mask)
    return _sb_weights(ls, lr, mask, c, tri), c + jnp.sum(lr, axis=1, keepdims=True)


def _sb_logs(qk, mask):
    z = qk * SCALE
    ls = jnp.minimum(z, 0.0) - jnp.log(1.0 + jnp.exp(-jnp.abs(z)))
    lr = ls - z
    return ls, (lr if mask is None else jnp.where(mask, lr, 0.0))


def _sb_weights(ls, lr, mask, c, tri):
    w = jnp.exp(ls + _dot_split(lr, tri) + c)
    return w if mask is None else jnp.where(mask, w, 0.0)


def _tri(n):
    row = lax.broadcasted_iota(jnp.int32, (n, n), 0)
    col = lax.broadcasted_iota(jnp.int32, (n, n), 1)
    return (row > col).astype(BF16)


def _sb_prompt_kernel(q_ref, k_ref, v_ref, o_ref, *, tq, tk):
    i = pl.program_id(2)
    q = q_ref[...].astype(BF16)
    tri = _tri(tk)
    per_q = tq // tk
    row = i * tq + lax.broadcasted_iota(jnp.int32, (tq, tk), 0)
    col = lax.broadcasted_iota(jnp.int32, (tq, tk), 1)

    def block(j, carry, masked):
        c, acc = carry
        off = pl.multiple_of(j * tk, tk)
        k = k_ref[pl.ds(off, tk), :].astype(BF16)
        v = v_ref[pl.ds(off, tk), :].astype(BF16)
        w, c = _sb_block(q, k, v, (off + col < row) if masked else None, c, tri)
        return c, acc + _dot(w.astype(BF16), v)

    carry = (jnp.zeros((tq, 1), F32), jnp.zeros((tq, HEAD_DIM), F32))
    for d in range(per_q):
        carry = block((i + 1) * per_q - 1 - d, carry, True)
    _, acc = lax.fori_loop(0, i * per_q, lambda s, cr: block(i * per_q - 1 - s, cr, False), carry)
    o_ref[...] = acc


def _sb_prompt(qa, kva, bsz, t):
    tq, tk = 512, 256
    nq = t // tq
    return pl.pallas_call(
        functools.partial(_sb_prompt_kernel, tq=tq, tk=tk),
        grid=(bsz, N_HEADS_A, nq),
        in_specs=[pl.BlockSpec((tq, HEAD_DIM), lambda b, h, i: (b * nq + i, h)),
                  pl.BlockSpec((t, HEAD_DIM), lambda b, h, i: (b, h)),
                  pl.BlockSpec((t, HEAD_DIM), lambda b, h, i: (b, N_HEADS_A + h))],
        out_specs=pl.BlockSpec((tq, HEAD_DIM), lambda b, h, i: (b * nq + i, h)),
        out_shape=jax.ShapeDtypeStruct((bsz * t, D_A), F32),
        compiler_params=_params("parallel", "parallel", "parallel"),
        name="sb_prompt",
    )(qa, kva, kva)


SB_PAGES_PER_STEP = 8
PAD_T = SUBLANES


def _sb_decode_kernel(pt_ref, q_ref, new_ref, *refs, page):
    pages = refs[:SB_PAGES_PER_STEP]
    o_ref, c_ref, acc_ref = refs[SB_PAGES_PER_STEP:]
    p = pl.program_id(1)
    rows = N_HEADS_A * PAD_T
    rec = 2 * N_HEADS_A
    tri = _tri(page)
    qh = [q_ref[h].astype(BF16) for h in range(N_HEADS_A)]

    def scores(ref):
        return jnp.concatenate(
            [_dot_nt(qh[h], ref[pl.ds(h, page, stride=rec), :].astype(BF16)) for h in range(N_HEADS_A)],
            axis=0)

    def values(ref, w):
        wb = w.astype(BF16)
        return [_dot(wb[h * PAD_T:(h + 1) * PAD_T],
                     ref[pl.ds(N_HEADS_A + h, page, stride=rec), :].astype(BF16)) for h in range(N_HEADS_A)]

    @pl.when(p == 0)
    def _():
        t_of_row = lax.broadcasted_iota(jnp.int32, (rows, page), 0) % PAD_T
        mask = lax.broadcasted_iota(jnp.int32, (rows, page), 1) < t_of_row
        ls, lr = _sb_logs(scores(new_ref), mask)
        w = _sb_weights(ls, lr, mask, jnp.zeros((rows, 1), F32), tri)
        acc_ref[...] = jnp.concatenate(values(new_ref, w), axis=0)
        c_ref[...] = jnp.sum(lr, axis=1, keepdims=True)

    @pl.when(p > 0)
    def _():
        logs = [_sb_logs(scores(pg), None) for pg in pages]
        c = c_ref[...]
        acc = None
        for pg, (ls, lr) in zip(pages, logs):
            part = values(pg, _sb_weights(ls, lr, None, c, tri))
            acc = part if acc is None else [x + y for x, y in zip(acc, part)]
            c = c + jnp.sum(lr, axis=1, keepdims=True)
        c_ref[...] = c
        acc_ref[...] += jnp.concatenate(acc, axis=0)

    @pl.when(p == pl.num_programs(1) - 1)
    def _():
        o_ref[...] = acc_ref[...]


def _sb_decode(q_heads, new_rows, cache_rows, page_table, page):
    bsz = q_heads.shape[0]
    n_pages = page_table.shape[1]
    pps = SB_PAGES_PER_STEP
    steps = n_pages // pps
    rows = N_HEADS_A * PAD_T
    blk = page * 2 * N_HEADS_A
    page_specs = [
        pl.BlockSpec((blk, HEAD_DIM),
                     lambda b, p, pt, k=k: (pt[b, n_pages - 1 - (jnp.maximum(p - 1, 0) * pps + k)], 0))
        for k in range(pps)]
    grid_spec = pltpu.PrefetchScalarGridSpec(
        num_scalar_prefetch=1,
        grid=(bsz, steps + 1),
        in_specs=[pl.BlockSpec((None, N_HEADS_A, PAD_T, HEAD_DIM), lambda b, p, pt: (b, 0, 0, 0)),
                  pl.BlockSpec((None, blk, HEAD_DIM), lambda b, p, pt: (b, 0, 0))] + page_specs,
        out_specs=pl.BlockSpec((None, rows, HEAD_DIM), lambda b, p, pt: (b, 0, 0)),
        scratch_shapes=[pltpu.VMEM((rows, 1), F32), pltpu.VMEM((rows, HEAD_DIM), F32)],
    )
    return pl.pallas_call(
        functools.partial(_sb_decode_kernel, page=page),
        grid_spec=grid_spec,
        out_shape=jax.ShapeDtypeStruct((bsz, rows, HEAD_DIM), F32),
        compiler_params=_params("parallel", "arbitrary"),
        name="sb_decode",
    )(page_table, q_heads, new_rows, *([cache_rows] * pps))


def _compress_partial(load_rows, pe_ref, w1_ref, nchunk):
    half = CMP_LEN // 2
    a = jnp.zeros((nchunk, HEAD_DIM), F32)
    b = jnp.zeros((nchunk, HEAD_DIM), F32)
    for l in range(half):
        rows = load_rows(l)
        a = a + _dot((rows + pe_ref[l:l + 1, :]).astype(BF16), w1_ref[l])
        b = b + _dot((rows + pe_ref[half + l:half + l + 1, :]).astype(BF16), w1_ref[half + l])
    return a, b


def _compress_finish(a, b, w2):
    n = a.shape[0]
    pre = a + pltpu.roll(b, n - 1, 0)
    return _dot(jax.nn.gelu(pre).astype(BF16), w2)


def _compress_prompt_kernel(r_ref, pe_ref, w1_ref, w2_ref, o_ref, *, nchunk):
    load = lambda l: r_ref[pl.ds(l, nchunk, stride=CMP_STRIDE), :]
    a, b = _compress_partial(load, pe_ref, w1_ref, nchunk)
    o_ref[...] = _compress_finish(a, b, w2_ref[...])


def _compress_prompt(nsa, pe, w1, w2, bsz, t):
    nchunk = t // CMP_STRIDE
    return pl.pallas_call(
        functools.partial(_compress_prompt_kernel, nchunk=nchunk),
        grid=(bsz, 4),
        in_specs=[pl.BlockSpec((t, HEAD_DIM), lambda b, s: (b, s)),
                  pl.BlockSpec((None, CMP_LEN, HEAD_DIM), lambda b, s: (s // 2, 0, 0)),
                  pl.BlockSpec((None, CMP_LEN, HEAD_DIM, HEAD_DIM), lambda b, s: (s // 2, 0, 0, 0)),
                  pl.BlockSpec((None, HEAD_DIM, HEAD_DIM), lambda b, s: (s // 2, 0, 0))],
        out_specs=pl.BlockSpec((None, None, nchunk, HEAD_DIM), lambda b, s: (b, s, 0, 0)),
        out_shape=jax.ShapeDtypeStruct((bsz, 4, nchunk, HEAD_DIM), F32),
        compiler_params=_params("parallel", "parallel"),
        name="compress_prompt",
    )(nsa, pe, w1, w2)


PAGES_PER_STEP = 16


NSA_REC = 4 * N_KV_B


def _compress_pages_kernel(pt_ref, pe_ref, w1_ref, *refs, chunks_per_page):
    pages = refs[:PAGES_PER_STEP]
    a_ref, b_ref = refs[PAGES_PER_STEP:]
    nchunk = PAGES_PER_STEP * chunks_per_page
    for s in range(2 * N_KV_B):
        load = lambda l, s=s: jnp.concatenate(
            [pg[pl.ds(l * NSA_REC + s, chunks_per_page, stride=CMP_STRIDE * NSA_REC), :] for pg in pages],
            axis=0)
        a, b = _compress_partial(load, pe_ref.at[s // N_KV_B], w1_ref.at[s // N_KV_B], nchunk)
        a_ref[s] = a
        b_ref[s] = b


def _compress_pages(cache_rows, page_table, pe, w1, page):
    bsz, n_pages = page_table.shape
    cpp = page // CMP_STRIDE
    nchunk = PAGES_PER_STEP * cpp
    steps = n_pages // PAGES_PER_STEP
    page_specs = [
        pl.BlockSpec((page * NSA_REC, HEAD_DIM),
                     lambda b, s, pt, k=k: (pt[b, s * PAGES_PER_STEP + k], 0))
        for k in range(PAGES_PER_STEP)]
    out_spec = pl.BlockSpec((None, 2 * N_KV_B, nchunk, HEAD_DIM), lambda b, s, pt: (b, 0, s, 0))
    grid_spec = pltpu.PrefetchScalarGridSpec(
        num_scalar_prefetch=1,
        grid=(bsz, steps),
        in_specs=[pl.BlockSpec((2, CMP_LEN, HEAD_DIM), lambda b, s, pt: (0, 0, 0)),
                  pl.BlockSpec((2, CMP_LEN, HEAD_DIM, HEAD_DIM), lambda b, s, pt: (0, 0, 0, 0))]
                 + page_specs,
        out_specs=[out_spec, out_spec],
    )
    shape = jax.ShapeDtypeStruct((bsz, 2 * N_KV_B, n_pages * cpp, HEAD_DIM), F32)
    return pl.pallas_call(
        functools.partial(_compress_pages_kernel, chunks_per_page=cpp),
        grid_spec=grid_spec,
        out_shape=[shape, shape],
        compiler_params=_params("parallel", "parallel"),
        name="compress_pages",
    )(page_table, pe, w1, *([cache_rows] * PAGES_PER_STEP))


def _cmp_branch(qs, kcmp, vcmp, pos_rows):
    n = kcmp.shape[0]
    s = _dot_nt(qs, kcmp.astype(BF16)) * SCALE
    cmp_end = lax.broadcasted_iota(jnp.int32, (1, n), 1) * CMP_STRIDE + (CMP_LEN - 1)
    mask = cmp_end <= pos_rows
    s = jnp.where(mask, s, NEG_BIG)
    m = jnp.max(s, axis=-1, keepdims=True)
    e = jnp.where(mask, jnp.exp(s - m), 0.0)
    p = e / jnp.maximum(jnp.sum(e, axis=-1, keepdims=True), 1e-30)
    return _dot(p.astype(BF16), vcmp.astype(BF16)), p


def _select_blocks(imp, pos_rows, n_sel):
    r, width = imp.shape
    lane = lax.broadcasted_iota(jnp.int32, (r, width), 1)
    qblk = pos_rows // SLC_BLOCK
    forced = (lane == 0) | (lane == qblk) | (lane == qblk - 1)
    valid = lane * SLC_BLOCK <= pos_rows
    imp = jnp.where(forced, FORCE_SCORE, jnp.where(valid, imp, -FORCE_SCORE))
    imp = jnp.where(lane < n_sel, imp, -jnp.inf)
    rank = jnp.zeros((r, width), F32)
    for i in range(n_sel):
        ci = imp[:, i:i + 1]
        better = (ci > imp) | ((ci == imp) & (lane > i))
        rank = rank + jnp.where(better, 1.0, 0.0)
    return jnp.where((rank < float(min(N_SELECT, n_sel))) & (lane < n_sel), 1.0, 0.0)


def _softmax_step(s, mask, v, carry):
    m, l, acc = carry
    s = jnp.where(mask, s, NEG_BIG)
    m_new = jnp.maximum(m, jnp.max(s, axis=-1, keepdims=True))
    alpha = jnp.exp(m - m_new)
    e = jnp.where(mask, jnp.exp(s - m_new), 0.0)
    l = alpha * l + jnp.sum(e, axis=-1, keepdims=True)
    acc = alpha * acc + _dot(e.astype(BF16), v)
    return m_new, l, acc


def _softmax_init(r):
    return (jnp.full((r, 1), NEG_BIG, F32), jnp.zeros((r, 1), F32), jnp.zeros((r, HEAD_DIM), F32))


def _softmax_finish(carry):
    _, l, acc = carry
    return acc / jnp.maximum(l, 1e-30)


def _select_blocks_t(imp, pos, n_sel):
    rows, t = imp.shape
    blk = lax.broadcasted_iota(jnp.int32, (rows, t), 0)
    qblk = pos // SLC_BLOCK
    forced = (blk == 0) | (blk == qblk) | (blk == qblk - 1)
    valid = blk * SLC_BLOCK <= pos
    imp = jnp.where(forced, FORCE_SCORE, jnp.where(valid, imp, -FORCE_SCORE))
    imp = jnp.where(blk < n_sel, imp, -jnp.inf)
    rank = jnp.zeros((rows, t), F32)
    for i in range(n_sel):
        ri = imp[i:i + 1, :]
        better = (ri > imp) | ((ri == imp) & (blk > i))
        rank = rank + jnp.where(better, 1.0, 0.0)
    return jnp.where((rank < float(min(N_SELECT, n_sel))) & (blk < n_sel), 1.0, 0.0)


def _nsa_prompt_kernel(q_ref, g_ref, kc_ref, vc_ref, ks_ref, vs_ref, kw_ref, vw_ref,
                       mt_ref, e_ref, o_ref, *, tq, n_sel):
    i = pl.program_id(2)
    r = HPG * tq
    qs = jnp.concatenate([q_ref[:, h * HEAD_DIM:(h + 1) * HEAD_DIM] for h in range(HPG)],
                         axis=0).astype(BF16)
    pos_t = i * tq + lax.broadcasted_iota(jnp.int32, (tq, 1), 0)
    pos_r = jnp.concatenate([pos_t] * HPG, axis=0)

    o_c, p = _cmp_branch(qs, kc_ref[...], vc_ref[...], pos_r)
    psum = p[0:tq]
    for h in range(1, HPG):
        psum = psum + p[h * tq:(h + 1) * tq]
    hi = psum.astype(BF16)
    lo = (psum - hi.astype(F32)).astype(BF16)
    imp_t = _dot_nt(mt_ref[...], hi) + _dot_nt(mt_ref[...], lo)
    sel_rows = -(-n_sel // 8) * 8
    pos_lane = i * tq + lax.broadcasted_iota(jnp.int32, (1, tq), 1)
    sel_t = _select_blocks_t(imp_t[0:sel_rows], pos_lane, n_sel)
    sel = jnp.concatenate([sel_t, jnp.zeros((imp_t.shape[0] - sel_rows, tq), F32)], axis=0).T.astype(BF16)

    lane = lax.broadcasted_iota(jnp.int32, (r, tq), 1)

    def slc_body(c, carry):
        off = pl.multiple_of(c * tq, tq)
        k = ks_ref[pl.ds(off, tq), :].astype(BF16)
        v = vs_ref[pl.ds(off, tq), :].astype(BF16)
        hit = _dot(sel, e_ref[:, pl.ds(off, tq)])
        mask = (jnp.concatenate([hit] * HPG, axis=0) > 0.5) & (off + lane <= pos_r)
        return _softmax_step(_dot_nt(qs, k) * SCALE, mask, v, carry)

    o_s = _softmax_finish(lax.fori_loop(0, i + 1, slc_body, _softmax_init(r)))

    carry = _softmax_init(r)
    for d in range(WINDOW // tq, -1, -1):
        c = i - d
        off = pl.multiple_of(jnp.maximum(c, 0) * tq, tq)
        k = kw_ref[pl.ds(off, tq), :].astype(BF16)
        v = vw_ref[pl.ds(off, tq), :].astype(BF16)
        kpos = c * tq + lane
        dlt = pos_r - kpos
        mask = (dlt >= 0) & (dlt < WINDOW) & (kpos >= 0)
        carry = _softmax_step(_dot_nt(qs, k) * SCALE, mask, v, carry)
    o_w = _softmax_finish(carry)

    for h in range(HPG):
        rows = slice(h * tq, (h + 1) * tq)
        gc = g_ref[:, 3 * h:3 * h + 1]
        gs = g_ref[:, 3 * h + 1:3 * h + 2]
        gw = g_ref[:, 3 * h + 2:3 * h + 3]
        o_ref[:, h * HEAD_DIM:(h + 1) * HEAD_DIM] = gc * o_c[rows] + gs * o_s[rows] + gw * o_w[rows]


def _cmp_to_slc(n_cmp_pad, n_cmp, n_sel, width):
    i = np.arange(n_cmp_pad)[:, None]
    j = np.arange(width)[None, :]
    lo = np.maximum(i * CMP_STRIDE, j * SLC_BLOCK)
    hi = np.minimum(i * CMP_STRIDE + CMP_LEN, (j + 1) * SLC_BLOCK)
    m = np.maximum(hi - lo, 0) / CMP_STRIDE
    m = np.where((i < n_cmp) & (j < n_sel), m, 0.0)
    return jnp.asarray(m, dtype=BF16)


def _nsa_prompt(qb, gates, cmp_kv, nsa, win, bsz, t):
    tq = 256
    nq = t // tq
    n_cmp = (t - CMP_LEN) // CMP_STRIDE + 1
    n_cmp_pad = cmp_kv.shape[2]
    n_sel = -(-t // SLC_BLOCK)
    assert n_sel <= LANES
    m_mat = _cmp_to_slc(n_cmp_pad, n_cmp, n_sel, LANES).T
    e_mat = jnp.asarray(np.arange(LANES)[:, None] == (np.arange(t) // SLC_BLOCK)[None, :], dtype=BF16)
    full = lambda shape: pl.BlockSpec(shape, lambda b, g, i: (0,) * len(shape))
    rows = lambda col: pl.BlockSpec((t, HEAD_DIM), lambda b, g, i, col=col: (b, col + g))
    return pl.pallas_call(
        functools.partial(_nsa_prompt_kernel, tq=tq, n_sel=n_sel),
        grid=(bsz, N_KV_B, nq),
        in_specs=[pl.BlockSpec((tq, HPG * HEAD_DIM), lambda b, g, i: (b * nq + i, g)),
                  pl.BlockSpec((tq, LANES), lambda b, g, i: (b * nq + i, g)),
                  pl.BlockSpec((None, None, n_cmp_pad, HEAD_DIM), lambda b, g, i: (b, g, 0, 0)),
                  pl.BlockSpec((None, None, n_cmp_pad, HEAD_DIM), lambda b, g, i: (b, 2 + g, 0, 0)),
                  rows(4), rows(6), rows(0), rows(2),
                  full(m_mat.shape), full(e_mat.shape)],
        out_specs=pl.BlockSpec((tq, HPG * HEAD_DIM), lambda b, g, i: (b * nq + i, g)),
        out_shape=jax.ShapeDtypeStruct((bsz * t, D_B), F32),
        compiler_params=_params("parallel", "parallel", "parallel"),
        name="nsa_prompt",
    )(qb, gates, cmp_kv, cmp_kv, nsa, nsa, win, win, m_mat, e_mat)


def _nsa_select_kernel(q_ref, ak_ref, av_ref, bk_ref, bv_ref, w2_ref, m_ref, oc_ref, sel_ref,
                       *, n_new, past, n_sel):
    qs = jnp.concatenate([q_ref[:, h * HEAD_DIM:(h + 1) * HEAD_DIM] for h in range(HPG)],
                         axis=0).astype(BF16)
    pos_t = past + lax.broadcasted_iota(jnp.int32, (n_new, 1), 0)
    pos_r = jnp.concatenate([pos_t] * HPG, axis=0)
    kcmp = _compress_finish(ak_ref[...], bk_ref[...], w2_ref[0])
    vcmp = _compress_finish(av_ref[...], bv_ref[...], w2_ref[1])
    o_c, p = _cmp_branch(qs, kcmp, vcmp, pos_r)
    oc_ref[...] = o_c
    psum = p[0:n_new]
    for h in range(1, HPG):
        psum = psum + p[h * n_new:(h + 1) * n_new]
    sel = _select_blocks(_dot_split(psum, m_ref[...]), pos_t, n_sel)
    sel_ref[...] = jnp.concatenate([sel] * HPG, axis=0)


def _nsa_select(qb, part_a, part_b, w2, past):
    bsz, n_new, _ = qb.shape
    n_cmp_pad = part_a.shape[2]
    total = past + n_new
    n_cmp = (total - CMP_LEN) // CMP_STRIDE + 1
    n_sel = -(-total // SLC_BLOCK)
    width = -(-n_sel // LANES) * LANES
    m_mat = _cmp_to_slc(n_cmp_pad, n_cmp, n_sel, width)
    assert n_cmp == n_cmp_pad - 1, "compressed tokens must come from the paged rows only"
    r = HPG * n_new
    k_spec = pl.BlockSpec((None, None, n_cmp_pad, HEAD_DIM), lambda b, g: (b, g, 0, 0))
    v_spec = pl.BlockSpec((None, None, n_cmp_pad, HEAD_DIM), lambda b, g: (b, N_KV_B + g, 0, 0))
    return pl.pallas_call(
        functools.partial(_nsa_select_kernel, n_new=n_new, past=past, n_sel=n_sel),
        grid=(bsz, N_KV_B),
        in_specs=[pl.BlockSpec((None, n_new, HPG * HEAD_DIM), lambda b, g: (b, 0, g)),
                  k_spec, v_spec, k_spec, v_spec,
                  pl.BlockSpec((2, HEAD_DIM, HEAD_DIM), lambda b, g: (0, 0, 0)),
                  pl.BlockSpec(m_mat.shape, lambda b, g: (0, 0))],
        out_specs=[pl.BlockSpec((None, None, r, HEAD_DIM), lambda b, g: (b, g, 0, 0)),
                   pl.BlockSpec((None, None, r, width), lambda b, g: (b, g, 0, 0))],
        out_shape=[jax.ShapeDtypeStruct((bsz, N_KV_B, r, HEAD_DIM), F32),
                   jax.ShapeDtypeStruct((bsz, N_KV_B, r, width), F32)],
        compiler_params=_params("parallel", "parallel"),
        name="nsa_select",
    )(qb, part_a, part_a, part_b, part_b, w2, m_mat)


SLC_PAGES_PER_STEP = 8


def _nsa_slc_decode_kernel(pt_ref, q_ref, sel_ref, new_ref, *refs, n_new, past, page):
    pps = SLC_PAGES_PER_STEP
    pages = refs[:pps]
    o_ref, m_ref, l_ref, acc_ref = refs[pps:]
    p = pl.program_id(1)
    n_steps = pl.num_programs(1) - 1
    r = HPG * n_new
    rows = N_KV_B * r
    t_of_row = lax.broadcasted_iota(jnp.int32, (rows, 1), 0) % n_new
    lane = lax.broadcasted_iota(jnp.int32, (rows, page), 1)
    bpp = page // SLC_BLOCK
    qg = [q_ref[g].astype(BF16) for g in range(N_KV_B)]
    sel = jnp.concatenate([sel_ref[g] for g in range(N_KV_B)], axis=0)
    blk_lane = lax.broadcasted_iota(jnp.int32, sel.shape, 1)

    @pl.when(p == 0)
    def _():
        m_ref[...] = jnp.full(m_ref.shape, NEG_BIG, F32)
        l_ref[...] = jnp.zeros(l_ref.shape, F32)
        acc_ref[...] = jnp.zeros(acc_ref.shape, F32)

    def masked_scores(ref, first_pos):
        first_blk = first_pos // SLC_BLOCK
        hit = jnp.zeros((rows, page), F32)
        for j in range(bpp):
            flag = jnp.sum(jnp.where(blk_lane == first_blk + j, sel, 0.0), axis=-1, keepdims=True)
            hit = jnp.where(lane // SLC_BLOCK == j, flag, hit)
        mask = (hit > 0.5) & (first_pos + lane <= past + t_of_row)
        s = jnp.concatenate(
            [_dot_nt(qg[g], ref[pl.ds(2 * N_KV_B + g, page, stride=NSA_REC), :].astype(BF16))
             for g in range(N_KV_B)], axis=0) * SCALE
        return jnp.where(mask, s, NEG_BIG), mask

    def visit(blocks):
        scored = [masked_scores(ref, pos) for ref, pos in blocks]
        m_old = m_ref[...]
        m_new = m_old
        for s, _ in scored:
            m_new = jnp.maximum(m_new, jnp.max(s, axis=-1, keepdims=True))
        alpha = jnp.exp(m_old - m_new)
        l = alpha * l_ref[...]
        acc = alpha * acc_ref[...]
        for (ref, _), (s, mask) in zip(blocks, scored):
            e = jnp.where(mask, jnp.exp(s - m_new), 0.0)
            l = l + jnp.sum(e, axis=-1, keepdims=True)
            eb = e.astype(BF16)
            acc = acc + jnp.concatenate(
                [_dot(eb[g * r:(g + 1) * r], ref[pl.ds(3 * N_KV_B + g, page, stride=NSA_REC), :].astype(BF16))
                 for g in range(N_KV_B)], axis=0)
        m_ref[...] = m_new
        l_ref[...] = l
        acc_ref[...] = acc

    @pl.when(p < n_steps)
    def _():
        visit([(pg, (p * pps + k) * page) for k, pg in enumerate(pages)])

    @pl.when(p == n_steps)
    def _():
        visit([(new_ref, past)])
        o_ref[...] = acc_ref[...] / jnp.maximum(l_ref[...], 1e-30)


def _nsa_slc_decode(q_groups, sel, new_rows, cache_rows, page_table, past, page, n_new):
    bsz = q_groups.shape[0]
    n_pages = page_table.shape[1]
    pps = SLC_PAGES_PER_STEP
    steps = n_pages // pps
    r = HPG * n_new
    rows = N_KV_B * r
    width = sel.shape[-1]
    blk = page * NSA_REC
    page_specs = [
        pl.BlockSpec((blk, HEAD_DIM),
                     lambda b, p, pt, k=k: (pt[b, jnp.minimum(p, steps - 1) * pps + k], 0))
        for k in range(pps)]
    grid_spec = pltpu.PrefetchScalarGridSpec(
        num_scalar_prefetch=1,
        grid=(bsz, steps + 1),
        in_specs=[pl.BlockSpec((None, N_KV_B, r, HEAD_DIM), lambda b, p, pt: (b, 0, 0, 0)),
                  pl.BlockSpec((None, N_KV_B, r, width), lambda b, p, pt: (b, 0, 0, 0)),
                  pl.BlockSpec((None, blk, HEAD_DIM), lambda b, p, pt: (b, 0, 0))] + page_specs,
        out_specs=pl.BlockSpec((None, rows, HEAD_DIM), lambda b, p, pt: (b, 0, 0)),
        scratch_shapes=[pltpu.VMEM((rows, 1), F32), pltpu.VMEM((rows, 1), F32),
                        pltpu.VMEM((rows, HEAD_DIM), F32)],
    )
    return pl.pallas_call(
        functools.partial(_nsa_slc_decode_kernel, n_new=n_new, past=past, page=page),
        grid_spec=grid_spec,
        out_shape=jax.ShapeDtypeStruct((bsz, rows, HEAD_DIM), F32),
        compiler_params=_params("parallel", "arbitrary"),
        name="nsa_slc_decode",
    )(page_table, q_groups, sel, new_rows, *([cache_rows] * pps))


def _nsa_merge_decode_kernel(q_ref, g_ref, oc_ref, os_ref, st_ref, new_ref, o_ref, nw_ref,
                             *, n_new, past, n_buf):
    r = HPG * n_new
    n_pad = new_ref.shape[0]
    t_of_row = lax.broadcasted_iota(jnp.int32, (r, 1), 0) % n_new
    pos_r = past + t_of_row
    buf_pos = past - n_buf + lax.broadcasted_iota(jnp.int32, (r, n_buf), 1)
    new_pos = past + lax.broadcasted_iota(jnp.int32, (r, n_pad), 1)

    def in_window(kpos):
        dlt = pos_r - kpos
        return (dlt >= 0) & (dlt < WINDOW) & (kpos >= 0)

    for g in range(N_KV_B):
        qs = jnp.concatenate(
            [q_ref[:, (g * HPG + h) * HEAD_DIM:(g * HPG + h + 1) * HEAD_DIM] for h in range(HPG)],
            axis=0).astype(BF16)
        kcol = slice(g * HEAD_DIM, (g + 1) * HEAD_DIM)
        vcol = slice((N_KV_B + g) * HEAD_DIM, (N_KV_B + g + 1) * HEAD_DIM)
        carry = _softmax_init(r)
        carry = _softmax_step(_dot_nt(qs, st_ref[:, kcol].astype(BF16)) * SCALE, in_window(buf_pos),
                              st_ref[:, vcol].astype(BF16), carry)
        carry = _softmax_step(_dot_nt(qs, new_ref[:, kcol].astype(BF16)) * SCALE, in_window(new_pos),
                              new_ref[:, vcol].astype(BF16), carry)
        o_w = _softmax_finish(carry)
        for h in range(HPG):
            rows = slice(h * n_new, (h + 1) * n_new)
            c0 = g * LANES + 3 * h
            out = (g_ref[:, c0:c0 + 1] * oc_ref[g, rows] + g_ref[:, c0 + 1:c0 + 2] * os_ref[g, rows]
                   + g_ref[:, c0 + 2:c0 + 3] * o_w[rows])
            o_ref[:, (g * HPG + h) * HEAD_DIM:(g * HPG + h + 1) * HEAD_DIM] = out
    shifted = pltpu.roll(st_ref[...], n_buf - n_new, 0)
    placed = pltpu.roll(new_ref[...], n_pad - n_new, 0)
    tail_row = lax.broadcasted_iota(jnp.int32, (n_pad, placed.shape[1]), 0)
    nw_ref[0:n_buf - n_pad, :] = shifted[0:n_buf - n_pad]
    nw_ref[n_buf - n_pad:n_buf, :] = jnp.where(tail_row >= n_pad - n_new, placed, shifted[n_buf - n_pad:])


def _nsa_merge_decode(qb, gates, o_cmp, o_slc, state_win, win_new, past):
    bsz, n_new, _ = qb.shape
    n_buf = state_win.shape[1]
    n_pad = win_new.shape[1]
    keep = min(WINDOW, past + n_new)
    assert keep == n_buf and n_buf > n_pad >= n_new
    r = HPG * n_new
    cols = 2 * D_KV_B
    per_b = lambda shape: pl.BlockSpec((None,) + shape, lambda b: (b,) + (0,) * len(shape))
    return pl.pallas_call(
        functools.partial(_nsa_merge_decode_kernel, n_new=n_new, past=past, n_buf=n_buf),
        grid=(bsz,),
        in_specs=[per_b((n_new, D_B)), per_b((n_new, 2 * LANES)),
                  per_b((N_KV_B, r, HEAD_DIM)), per_b((N_KV_B, r, HEAD_DIM)),
                  per_b((n_buf, cols)), per_b((n_pad, cols))],
        out_specs=[per_b((n_new, D_B)), per_b((keep, cols))],
        out_shape=[jax.ShapeDtypeStruct((bsz, n_new, D_B), F32),
                   jax.ShapeDtypeStruct((bsz, keep, cols), F32)],
        compiler_params=_params("parallel"),
        name="nsa_merge_decode",
    )(qb, gates, o_cmp, o_slc, state_win, win_new)


def _outproj_kernel(x_ref, oa_ref, ob_ref, wa_ref, wb_ref, g_ref, b_ref, h_ref):
    y = _dot(oa_ref[...].astype(BF16), wa_ref[...]) + _dot(ob_ref[...].astype(BF16), wb_ref[...])
    h_ref[...] = _layer_norm(ALPHA * x_ref[...] + y, g_ref[...], b_ref[...])


def _outproj(x, o_a, o_b, wa, wb, g, b):
    m, d = x.shape
    tm = min(512, m)
    row = lambda w: pl.BlockSpec((tm, w), lambda i: (i, 0))
    full = lambda a: pl.BlockSpec(a.shape, lambda i: (0, 0))
    return pl.pallas_call(
        _outproj_kernel,
        grid=(m // tm,),
        in_specs=[row(d), row(D_A), row(D_B), full(wa), full(wb), full(g), full(b)],
        out_specs=row(d),
        out_shape=jax.ShapeDtypeStruct((m, d), F32),
        compiler_params=_params("parallel"),
        name="outproj_ln",
    )(x, o_a, o_b, wa, wb, g, b)


def _peer_query_kernel(h_ref, w_ref, q_ref):
    q_ref[...] = _dot(h_ref[...].astype(BF16), w_ref[...])


def _peer_query(h, wq):
    m, d = h.shape
    n = wq.shape[1]
    tm = min(512, m)
    return pl.pallas_call(
        _peer_query_kernel,
        grid=(m // tm,),
        in_specs=[pl.BlockSpec((tm, d), lambda i: (i, 0)), pl.BlockSpec((d, n), lambda i: (0, 0))],
        out_specs=pl.BlockSpec((tm, n), lambda i: (i, 0)),
        out_shape=jax.ShapeDtypeStruct((m, n), F32),
        compiler_params=_params("parallel"),
        name="peer_query",
    )(h, wq)


def _compare_exchange(v, i, j, descending):
    hi, lo = jnp.maximum(v[i], v[j]), jnp.minimum(v[i], v[j])
    v[i], v[j] = (hi, lo) if descending else (lo, hi)


def _bitonic_merge_desc(v):
    j = len(v) // 2
    while j >= 1:
        for i in range(len(v)):
            if i ^ j > i:
                _compare_exchange(v, i, i ^ j, True)
        j //= 2


def _bitonic_sort_desc(v):
    n = len(v)
    k = 2
    while k <= n:
        j = k // 2
        while j >= 1:
            for i in range(n):
                if i ^ j > i:
                    _compare_exchange(v, i, i ^ j, (i & k) == 0)
            j //= 2
        k *= 2


def _top_sorted(x, k):
    v = [x[i:i + SUBLANES] for i in range(0, x.shape[0], SUBLANES)]
    _bitonic_sort_desc(v)
    d = 1
    while d < SUBLANES:
        other = [pltpu.roll(a, d, 0) for a in v]
        if 2 * len(v) <= k:
            v = v + other[::-1]
        else:
            v = [jnp.maximum(v[i], other[k - 1 - i]) for i in range(k)]
        _bitonic_merge_desc(v)
        d *= 2
    return [a[0:1] for a in v]


_PEER_PAIRS = [(i, j) for i in range(PEER_TOPK) for j in range(PEER_TOPK) if (i + 1) * (j + 1) <= PEER_TOPK]
_PEER_CAND_ROWS = SUBLANES * (1 << (-(-len(_PEER_PAIRS) // SUBLANES) - 1).bit_length())


def _peer_route_kernel(q_ref, sk_ref, thr_ref, s2_ref, e1_ref, e2_ref, cand_ref):
    half = sk_ref.shape[2]
    cand_ref[...] = jnp.full(cand_ref.shape, -jnp.inf, F32)
    for h in range(PEER_HEADS):
        s, tops = [], []
        for c in range(2):
            col = (2 * h + c) * half
            st = _dot_nt(sk_ref[c], q_ref[:, col:col + half].astype(BF16))
            s.append(st)
            tops.append(_top_sorted(st, PEER_TOPK))
        for n, (i, j) in enumerate(_PEER_PAIRS):
            cand_ref[n:n + 1, :] = tops[0][i] + tops[1][j]
        best = _top_sorted(cand_ref[...], PEER_TOPK)
        top = best[0]
        z = jnp.ones_like(top)
        for v in best[1:]:
            z = z + jnp.exp(v - top)
        tau = best[PEER_TOPK - 1]
        thr = jnp.full(s[0].shape, jnp.inf, F32)
        for i in reversed(range(PEER_TOPK)):
            thr_i = jnp.full_like(tau, jnp.inf)
            for j in range(PEER_TOPK):
                if (i + 1) * (j + 1) <= PEER_TOPK:
                    thr_i = jnp.where(tops[0][i] + tops[1][j] >= tau, tops[1][j], thr_i)
            thr = jnp.where(s[0] == tops[0][i], thr_i, thr)
        thr_ref[h] = thr
        s2_ref[h] = s[1]
        e1_ref[h] = jnp.exp(s[0] - tops[0][0])
        e2_ref[h] = jnp.exp(s[1] - tops[1][0]) / z


def _peer_route(q, sub_keys):
    n = q.shape[0]
    tm = 128
    nk = sub_keys.shape[1]
    big = jax.ShapeDtypeStruct((PEER_HEADS, nk, n), F32)
    big_spec = pl.BlockSpec((PEER_HEADS, nk, tm), lambda i: (0, 0, i))
    return pl.pallas_call(
        _peer_route_kernel,
        grid=(n // tm,),
        in_specs=[pl.BlockSpec((tm, q.shape[1]), lambda i: (i, 0)),
                  pl.BlockSpec(sub_keys.shape, lambda i: (0, 0, 0))],
        out_specs=[big_spec] * 4,
        out_shape=[big] * 4,
        scratch_shapes=[pltpu.VMEM((_PEER_CAND_ROWS, tm), F32)],
        compiler_params=_params("parallel"),
        name="peer_route",
    )(q, sub_keys)


def _peer_dense_kernel(h_ref, u_ref, v_ref, thr_ref, s2_ref, e1_ref, e2_ref, g_ref, b_ref,
                       o_ref, hb_ref, wt_ref, *, tm, ac, sub_ac):
    c = pl.program_id(1)
    nk = s2_ref.shape[1]
    kb_rows = nk // 4

    @pl.when(c == 0)
    def _():
        o_ref[...] = jnp.zeros(o_ref.shape, F32)
        hb_ref[...] = h_ref[...].astype(BF16)

    for sub in range(ac // sub_ac):
        ex = slice(sub * sub_ac * nk, (sub + 1) * sub_ac * nk)
        act = jax.nn.gelu(_dot_nt(u_ref[ex, :], hb_ref[...]))
        for ts in range(tm // LANES):
            tok = slice(ts * LANES, (ts + 1) * LANES)
            for kb in range(nk // kb_rows):
                keys = slice(kb * kb_rows, (kb + 1) * kb_rows)
                gates = [jnp.zeros((kb_rows, LANES), F32) for _ in range(sub_ac)]
                for h in range(PEER_HEADS):
                    s2 = s2_ref[h, keys, tok]
                    e2 = e2_ref[h, keys, tok]
                    for k in range(sub_ac):
                        aa = sub * sub_ac + k
                        chosen = s2 >= thr_ref[h, aa:aa + 1, tok]
                        gates[k] = gates[k] + jnp.where(chosen, e1_ref[h, aa:aa + 1, tok] * e2, 0.0)
                for k in range(sub_ac):
                    rows = slice(k * nk + kb * kb_rows, k * nk + (kb + 1) * kb_rows)
                    wt_ref[sub * sub_ac * nk + rows.start:sub * sub_ac * nk + rows.stop, tok] = (
                        gates[k] * act[rows, tok]).astype(BF16)
    o_ref[...] += _dot_tn(wt_ref[...], v_ref[...])

    @pl.when(c == pl.num_programs(1) - 1)
    def _():
        o_ref[...] = _layer_norm(ALPHA * h_ref[...] + o_ref[...], g_ref[...], b_ref[...])


def _peer_dense(h, u, v, route, g, b):
    n, d = h.shape
    thr, s2, e1, e2 = route
    nk = s2.shape[1]
    tm = min(512, n)
    ac = 8
    big_spec = pl.BlockSpec((PEER_HEADS, nk, tm), lambda i, c: (0, 0, i))
    row_spec = pl.BlockSpec((PEER_HEADS, ac, tm), lambda i, c: (0, c, i))
    tab_spec = pl.BlockSpec((ac * nk, d), lambda i, c: (c, 0))
    full = lambda a: pl.BlockSpec(a.shape, lambda i, c: (0, 0))
    return pl.pallas_call(
        functools.partial(_peer_dense_kernel, tm=tm, ac=ac, sub_ac=4),
        grid=(n // tm, nk // ac),
        in_specs=[pl.BlockSpec((tm, d), lambda i, c: (i, 0)), tab_spec, tab_spec,
                  row_spec, big_spec, row_spec, big_spec, full(g), full(b)],
        out_specs=pl.BlockSpec((tm, d), lambda i, c: (i, 0)),
        out_shape=jax.ShapeDtypeStruct((n, d), F32),
        scratch_shapes=[pltpu.VMEM((tm, d), BF16), pltpu.VMEM((ac * nk, tm), BF16)],
        compiler_params=_params("parallel", "arbitrary"),
        name="peer_dense",
    )(h, u, v, thr, s2, e1, e2, g, b)


def _post_block(x, o_a, o_b, wts):
    h = _outproj(x, o_a, o_b, wts["wo_a"], wts["wo_b"], wts["ln1_g"], wts["ln1_b"])
    q = _peer_query(h, wts["wq"])
    route = _peer_route(q, wts["sub_keys"])
    return _peer_dense(h, wts["u"], wts["v"], route, wts["ln2_g"], wts["ln2_b"])


def _pad_rows(a, rows):
    return jnp.pad(a, ((0, rows - a.shape[0]),) + ((0, 0),) * (a.ndim - 1))


def kernel(x_prompt, x_sample, cache_a, cache_nsa, state_win, page_table, w_in, cmp_pe_k, cmp_w1_k,
           cmp_w2_k, cmp_pe_v, cmp_w1_v, cmp_w2_v, w_out, ln1_g, ln1_b, peer_w_query, peer_sub_keys,
           peer_u, peer_v, ln2_g, ln2_b):
    assert w_in.shape[0] == DEPTH == 1
    bsz, t, d = x_prompt.shape
    dbsz, n_new, _ = x_sample.shape
    n_pool, page = cache_a.shape[1], cache_a.shape[2]
    past = page_table.shape[1] * page

    w = w_in[0]
    o = np.cumsum((0, D_A, 2 * D_A, D_B, 4 * D_KV_B, 2 * D_KV_B, 3 * N_HEADS_B))
    wg = w[:, o[5]:o[6]]
    per_g = 3 * HPG
    gate_w = jnp.concatenate(
        [jnp.pad(wg[:, g * per_g:(g + 1) * per_g], ((0, 0), (0, LANES - per_g))) for g in range(N_KV_B)],
        axis=1)
    wparts = {"qa": w[:, o[0]:o[1]], "kva": w[:, o[1]:o[2]], "qb": w[:, o[2]:o[3]],
              "nsa": w[:, o[3]:o[4]], "win": w[:, o[4]:o[5]], "gates": gate_w}
    wparts = {k: v.astype(BF16) for k, v in wparts.items()}
    row2 = lambda a: a[0].reshape(1, -1)
    wts = {"wo_a": w_out[0, :D_A].astype(BF16), "wo_b": w_out[0, D_A:].astype(BF16),
           "ln1_g": row2(ln1_g), "ln1_b": row2(ln1_b), "ln2_g": row2(ln2_g), "ln2_b": row2(ln2_b),
           "wq": peer_w_query[0].astype(BF16), "sub_keys": peer_sub_keys[0].astype(BF16),
           "u": peer_u[0].astype(BF16), "v": peer_v[0].astype(BF16)}
    pe = jnp.stack([cmp_pe_k[0], cmp_pe_v[0]])
    w1 = jnp.stack([cmp_w1_k[0], cmp_w1_v[0]]).astype(BF16)
    w2 = jnp.stack([cmp_w2_k[0], cmp_w2_v[0]]).astype(BF16)

    xp = x_prompt.reshape(bsz * t, d)
    tabs_p = _rope_tables(jnp.arange(t, dtype=jnp.int32))
    qa, kva, qb, nsa, win, gates = _project_all(xp.astype(BF16), wparts, tabs_p)
    o_a = _sb_prompt(qa, kva, bsz, t)
    cmp_kv = _compress_prompt(nsa, pe, w1, w2, bsz, t)
    o_b = _nsa_prompt(qb, gates, cmp_kv, nsa, win, bsz, t)
    y_prompt = _post_block(xp, o_a, o_b, wts).reshape(bsz, t, d)
    new_a_p = kva.reshape(1, bsz, t, 2, N_HEADS_A, HEAD_DIM)
    new_nsa_p = nsa.reshape(1, bsz, t, 4, N_KV_B, HEAD_DIM)
    keep_p = min(WINDOW, t)
    new_win_p = win.reshape(bsz, t, 2, N_KV_B, HEAD_DIM)[None, :, t - keep_p:]

    ns = dbsz * n_new
    xs = x_sample.reshape(ns, d)
    pos_s = past + jnp.arange(n_new, dtype=jnp.int32)
    tabs_s = tuple(jnp.tile(tb, (dbsz, 1)) for tb in _rope_tables(pos_s))
    qa_s, kva_s, qb_s, nsa_s, win_s, gates_s = _project_all(xs.astype(BF16), wparts, tabs_s)
    by_b = lambda a: a.reshape(dbsz, n_new, a.shape[-1])
    pad_page = lambda a: jnp.pad(by_b(a), ((0, 0), (0, page - n_new), (0, 0)))
    as_records = lambda a, rec: jnp.pad(a.reshape(dbsz, n_new * rec, HEAD_DIM),
                                        ((0, 0), (0, (page - n_new) * rec), (0, 0)))
    q_heads = jnp.pad(qa_s.reshape(dbsz, n_new, N_HEADS_A, HEAD_DIM).transpose(0, 2, 1, 3),
                      ((0, 0), (0, 0), (0, PAD_T - n_new), (0, 0)))
    o_a_h = _sb_decode(q_heads, as_records(kva_s, 2 * N_HEADS_A),
                       cache_a.reshape(n_pool * page * 2 * N_HEADS_A, HEAD_DIM), page_table, page)
    o_a_s = o_a_h.reshape(dbsz, N_HEADS_A, PAD_T, HEAD_DIM)[:, :, :n_new].transpose(0, 2, 1, 3)
    cache_n = cache_nsa.reshape(n_pool * page * NSA_REC, HEAD_DIM)
    part_a, part_b = _compress_pages(cache_n, page_table, pe, w1, page)
    o_cmp, sel = _nsa_select(by_b(qb_s), part_a, part_b, w2, past)
    q_groups = qb_s.reshape(dbsz, n_new, N_KV_B, HPG, HEAD_DIM).transpose(0, 2, 3, 1, 4).reshape(
        dbsz, N_KV_B, HPG * n_new, HEAD_DIM)
    o_slc = _nsa_slc_decode(q_groups, sel, as_records(nsa_s, NSA_REC), cache_n, page_table, past, page,
                            n_new).reshape(dbsz, N_KV_B, HPG * n_new, HEAD_DIM)
    n_buf = state_win.shape[2]
    o_b_s, new_win = _nsa_merge_decode(by_b(qb_s), by_b(gates_s), o_cmp, o_slc,
                                       state_win[0].reshape(dbsz, n_buf, 2 * D_KV_B), pad_page(win_s), past)
    rows_s = -(-ns // LANES) * LANES
    y_s = _post_block(_pad_rows(xs, rows_s), _pad_rows(o_a_s.reshape(ns, D_A), rows_s),
                      _pad_rows(o_b_s.reshape(ns, D_B), rows_s), wts)
    y_sample = y_s[:ns].reshape(dbsz, n_new, d)
    new_a_s = kva_s.reshape(1, dbsz, n_new, 2, N_HEADS_A, HEAD_DIM)
    new_nsa_s = nsa_s.reshape(1, dbsz, n_new, 4, N_KV_B, HEAD_DIM)
    new_win_s = new_win.reshape(1, dbsz, new_win.shape[1], 2, N_KV_B, HEAD_DIM)
    return (y_prompt, y_sample, new_a_p, new_nsa_p, new_win_p, new_a_s, new_nsa_s, new_win_s)
```

```python
import functools
import math

import numpy as np
import jax
import jax.numpy as jnp
from jax import lax
from jax.experimental import pallas as pl
from jax.experimental.pallas import tpu as pltpu

F32 = jnp.float32
BF16 = jnp.bfloat16

LANES = 128
SUBLANES = 8
VMEM_LIMIT = 56 * 1024 * 1024

HEAD_DIM = 128
N_HEADS_A = 8
N_HEADS_B = 8
N_KV_B = 2
HPG = N_HEADS_B // N_KV_B
D_A = N_HEADS_A * HEAD_DIM
D_B = N_HEADS_B * HEAD_DIM
D_KV_B = N_KV_B * HEAD_DIM
ROPE_DIM = HEAD_DIM // 4
ROPE_THETA = 500000.0
CMP_LEN = 32
CMP_STRIDE = 16
SLC_BLOCK = 64
N_SELECT = 16
WINDOW = 512
FORCE_SCORE = 1.0e4
PEER_HEADS = 8
PEER_NKEYS = 128
PEER_TOPK = 16
DEPTH = 1
ALPHA = (2 * DEPTH) ** 0.25
LN_EPS = 1e-5
NEG_BIG = -1e30
SCALE = 1.0 / math.sqrt(HEAD_DIM)


def _dot(a, b):
    return jnp.dot(a, b, preferred_element_type=F32)


def _dot_nt(a, b):
    return lax.dot_general(a, b, (((1,), (1,)), ((), ())), preferred_element_type=F32)


def _dot_tn(a, b):
    return lax.dot_general(a, b, (((0,), (0,)), ((), ())), preferred_element_type=F32)


def _dot_split(x, w):
    hi = x.astype(BF16)
    lo = (x - hi.astype(F32)).astype(BF16)
    return _dot(hi, w) + _dot(lo, w)


def _params(*sem):
    return pltpu.CompilerParams(dimension_semantics=sem, vmem_limit_bytes=VMEM_LIMIT)


def _layer_norm(r, g, b):
    mu = jnp.mean(r, axis=-1, keepdims=True)
    d = r - mu
    var = jnp.mean(d * d, axis=-1, keepdims=True)
    return d * lax.rsqrt(var + LN_EPS) * g + b


def _inproj_kernel(x_ref, w_ref, c_ref, sa_ref, sb_ref, o_ref, *, rope_flags, sigmoid):
    acc = _dot(x_ref[...], w_ref[...])
    for j, flag in enumerate(rope_flags):
        blk = acc[:, j * LANES:(j + 1) * LANES]
        if flag:
            blk = (blk * c_ref[...]
                   + pltpu.roll(blk, LANES - ROPE_DIM // 2, 1) * sa_ref[...]
                   + pltpu.roll(blk, ROPE_DIM // 2, 1) * sb_ref[...])
        if sigmoid:
            blk = jax.nn.sigmoid(blk)
        o_ref[:, j * LANES:(j + 1) * LANES] = blk


def _inproj(xb, w, tabs, rope_flags, sigmoid=False):
    m, k = xb.shape
    n = w.shape[1]
    c, sa, sb = tabs
    tm = min(1024, m)
    nt = c.shape[0] // tm
    tab_spec = pl.BlockSpec((tm, LANES), lambda i: (i % nt, 0))
    return pl.pallas_call(
        functools.partial(_inproj_kernel, rope_flags=tuple(rope_flags), sigmoid=sigmoid),
        grid=(m // tm,),
        in_specs=[pl.BlockSpec((tm, k), lambda i: (i, 0)),
                  pl.BlockSpec((k, n), lambda i: (0, 0)),
                  tab_spec, tab_spec, tab_spec],
        out_specs=pl.BlockSpec((tm, n), lambda i: (i, 0)),
        out_shape=jax.ShapeDtypeStruct((m, n), F32),
        compiler_params=_params("parallel"),
        name="inproj",
    )(xb, w, c, sa, sb)


def _rope_tables(pos):
    half = ROPE_DIM // 2
    inv = ROPE_THETA ** (-jnp.arange(half, dtype=F32) / half)
    ang = pos.astype(F32)[:, None] * inv[None, :]
    cos, sin = jnp.cos(ang), jnp.sin(ang)
    t = pos.shape[0]
    ones = jnp.ones((t, LANES - ROPE_DIM), F32)
    zeros = jnp.zeros((t, LANES - half), F32)
    c = jnp.concatenate([cos, cos, ones], axis=1)
    sa = jnp.concatenate([-sin, zeros], axis=1)
    sb = jnp.concatenate([jnp.zeros((t, half), F32), sin, jnp.zeros((t, LANES - ROPE_DIM), F32)], axis=1)
    return c, sa, sb


def _project_all(xb, wparts, tabs):
    qa = _inproj(xb, wparts["qa"], tabs, [0] * 8)
    kva = _inproj(xb, wparts["kva"], tabs, [0] * 16)
    qb = _inproj(xb, wparts["qb"], tabs, [1] * 8)
    nsa = _inproj(xb, wparts["nsa"], tabs, [1, 1, 0, 0, 1, 1, 0, 0])
    win = _inproj(xb, wparts["win"], tabs, [1, 1, 0, 0])
    gates = _inproj(xb, wparts["gates"], tabs, [0, 0], sigmoid=True)
    return qa, kva, qb, nsa, win, gates


def _sb_block(q, k, v, mask, c, tri):
    ls, lr = _sb_logs(_dot_nt(q, k), mask)
    return _sb_weights(ls, lr, mask, c, tri), c + jnp.sum(lr, axis=1, keepdims=True)


def _sb_logs(qk, mask):
    z = qk * SCALE
    ls = jnp.minimum(z, 0.0) - jnp.log(1.0 + jnp.exp(-jnp.abs(z)))
    lr = ls - z
    return ls, (lr if mask is None else jnp.where(mask, lr, 0.0))


def _sb_weights(ls, lr, mask, c, tri):
    w = jnp.exp(ls + _dot_split(lr, tri) + c)
    return w if mask is None else jnp.where(mask, w, 0.0)


def _tri(n):
    row = lax.broadcasted_iota(jnp.int32, (n, n), 0)
    col = lax.broadcasted_iota(jnp.int32, (n, n), 1)
    return (row > col).astype(BF16)


def _sb_prompt_kernel(q_ref, k_ref, v_ref, o_ref, *, tq, tk):
    i = pl.program_id(2)
    q = q_ref[...].astype(BF16)
    tri = _tri(tk)
    per_q = tq // tk
    row = i * tq + lax.broadcasted_iota(jnp.int32, (tq, tk), 0)
    col = lax.broadcasted_iota(jnp.int32, (tq, tk), 1)

    def block(j, carry, masked):
        c, acc = carry
        off = pl.multiple_of(j * tk, tk)
        k = k_ref[pl.ds(off, tk), :].astype(BF16)
        v = v_ref[pl.ds(off, tk), :].astype(BF16)
        w, c = _sb_block(q, k, v, (off + col < row) if masked else None, c, tri)
        return c, acc + _dot(w.astype(BF16), v)

    carry = (jnp.zeros((tq, 1), F32), jnp.zeros((tq, HEAD_DIM), F32))
    for d in range(per_q):
        carry = block((i + 1) * per_q - 1 - d, carry, True)
    _, acc = lax.fori_loop(0, i * per_q, lambda s, cr: block(i * per_q - 1 - s, cr, False), carry)
    o_ref[...] = acc


def _sb_prompt(qa, kva, bsz, t):
    tq, tk = min(1024, t), 256
    nq = t // tq
    return pl.pallas_call(
        functools.partial(_sb_prompt_kernel, tq=tq, tk=tk),
        grid=(bsz, N_HEADS_A, nq),
        in_specs=[pl.BlockSpec((tq, HEAD_DIM), lambda b, h, i: (b * nq + i, h)),
                  pl.BlockSpec((t, HEAD_DIM), lambda b, h, i: (b, h)),
                  pl.BlockSpec((t, HEAD_DIM), lambda b, h, i: (b, N_HEADS_A + h))],
        out_specs=pl.BlockSpec((tq, HEAD_DIM), lambda b, h, i: (b * nq + i, h)),
        out_shape=jax.ShapeDtypeStruct((bsz * t, D_A), F32),
        compiler_params=_params("parallel", "parallel", "parallel"),
        name="sb_prompt",
    )(qa, kva, kva)


SB_PAGES_PER_STEP = 8
PAD_T = SUBLANES


def _sb_decode_kernel(pt_ref, q_ref, new_ref, *refs, page):
    pages = refs[:SB_PAGES_PER_STEP]
    o_ref, c_ref, acc_ref = refs[SB_PAGES_PER_STEP:]
    p = pl.program_id(1)
    rows = N_HEADS_A * PAD_T
    rec = 2 * N_HEADS_A
    tri = _tri(page)
    qh = [q_ref[h].astype(BF16) for h in range(N_HEADS_A)]

    def scores(ref):
        return jnp.concatenate(
            [_dot_nt(qh[h], ref[pl.ds(h, page, stride=rec), :].astype(BF16)) for h in range(N_HEADS_A)],
            axis=0)

    def values(ref, w):
        wb = w.astype(BF16)
        return [_dot(wb[h * PAD_T:(h + 1) * PAD_T],
                     ref[pl.ds(N_HEADS_A + h, page, stride=rec), :].astype(BF16)) for h in range(N_HEADS_A)]

    @pl.when(p == 0)
    def _():
        t_of_row = lax.broadcasted_iota(jnp.int32, (rows, page), 0) % PAD_T
        mask = lax.broadcasted_iota(jnp.int32, (rows, page), 1) < t_of_row
        ls, lr = _sb_logs(scores(new_ref), mask)
        w = _sb_weights(ls, lr, mask, jnp.zeros((rows, 1), F32), tri)
        acc_ref[...] = jnp.concatenate(values(new_ref, w), axis=0)
        c_ref[...] = jnp.sum(lr, axis=1, keepdims=True)

    @pl.when(p > 0)
    def _():
        logs = [_sb_logs(scores(pg), None) for pg in pages]
        c = c_ref[...]
        acc = None
        for pg, (ls, lr) in zip(pages, logs):
            part = values(pg, _sb_weights(ls, lr, None, c, tri))
            acc = part if acc is None else [x + y for x, y in zip(acc, part)]
            c = c + jnp.sum(lr, axis=1, keepdims=True)
        c_ref[...] = c
        acc_ref[...] += jnp.concatenate(acc, axis=0)

    @pl.when(p == pl.num_programs(1) - 1)
    def _():
        o_ref[...] = acc_ref[...]


def _sb_decode(q_heads, new_rows, cache_rows, page_table, page):
    bsz = q_heads.shape[0]
    n_pages = page_table.shape[1]
    pps = SB_PAGES_PER_STEP
    steps = n_pages // pps
    rows = N_HEADS_A * PAD_T
    blk = page * 2 * N_HEADS_A
    page_specs = [
        pl.BlockSpec((blk, HEAD_DIM),
                     lambda b, p, pt, k=k: (pt[b, n_pages - 1 - (jnp.maximum(p - 1, 0) * pps + k)], 0))
        for k in range(pps)]
    grid_spec = pltpu.PrefetchScalarGridSpec(
        num_scalar_prefetch=1,
        grid=(bsz, steps + 1),
        in_specs=[pl.BlockSpec((None, N_HEADS_A, PAD_T, HEAD_DIM), lambda b, p, pt: (b, 0, 0, 0)),
                  pl.BlockSpec((None, blk, HEAD_DIM), lambda b, p, pt: (b, 0, 0))] + page_specs,
        out_specs=pl.BlockSpec((None, rows, HEAD_DIM), lambda b, p, pt: (b, 0, 0)),
        scratch_shapes=[pltpu.VMEM((rows, 1), F32), pltpu.VMEM((rows, HEAD_DIM), F32)],
    )
    return pl.pallas_call(
        functools.partial(_sb_decode_kernel, page=page),
        grid_spec=grid_spec,
        out_shape=jax.ShapeDtypeStruct((bsz, rows, HEAD_DIM), F32),
        compiler_params=_params("parallel", "arbitrary"),
        name="sb_decode",
    )(page_table, q_heads, new_rows, *([cache_rows] * pps))


def _compress_partial(load_rows, pe_ref, w1_ref, nchunk):
    half = CMP_LEN // 2
    a = jnp.zeros((nchunk, HEAD_DIM), F32)
    b = jnp.zeros((nchunk, HEAD_DIM), F32)
    for l in range(half):
        rows = load_rows(l)
        a = a + _dot((rows + pe_ref[l:l + 1, :]).astype(BF16), w1_ref[l])
        b = b + _dot((rows + pe_ref[half + l:half + l + 1, :]).astype(BF16), w1_ref[half + l])
    return a, b


def _compress_finish(a, b, w2):
    n = a.shape[0]
    pre = a + pltpu.roll(b, n - 1, 0)
    return _dot(jax.nn.gelu(pre).astype(BF16), w2)


def _compress_prompt_kernel(r_ref, pe_ref, w1_ref, w2_ref, o_ref, *, nchunk):
    load = lambda l: r_ref[pl.ds(l, nchunk, stride=CMP_STRIDE), :]
    a, b = _compress_partial(load, pe_ref, w1_ref, nchunk)
    o_ref[...] = _compress_finish(a, b, w2_ref[...])


def _compress_prompt(nsa, pe, w1, w2, bsz, t):
    nchunk = t // CMP_STRIDE
    return pl.pallas_call(
        functools.partial(_compress_prompt_kernel, nchunk=nchunk),
        grid=(bsz, 4),
        in_specs=[pl.BlockSpec((t, HEAD_DIM), lambda b, s: (b, s)),
                  pl.BlockSpec((None, CMP_LEN, HEAD_DIM), lambda b, s: (s // 2, 0, 0)),
                  pl.BlockSpec((None, CMP_LEN, HEAD_DIM, HEAD_DIM), lambda b, s: (s // 2, 0, 0, 0)),
                  pl.BlockSpec((None, HEAD_DIM, HEAD_DIM), lambda b, s: (s // 2, 0, 0))],
        out_specs=pl.BlockSpec((None, None, nchunk, HEAD_DIM), lambda b, s: (b, s, 0, 0)),
        out_shape=jax.ShapeDtypeStruct((bsz, 4, nchunk, HEAD_DIM), F32),
        compiler_params=_params("parallel", "parallel"),
        name="compress_prompt",
    )(nsa, pe, w1, w2)


PAGES_PER_STEP = 16


NSA_REC = 4 * N_KV_B


def _compress_pages_kernel(pt_ref, pe_ref, w1_ref, *refs, chunks_per_page):
    pages = refs[:PAGES_PER_STEP]
    a_ref, b_ref = refs[PAGES_PER_STEP:]
    nchunk = PAGES_PER_STEP * chunks_per_page
    for s in range(2 * N_KV_B):
        load = lambda l, s=s: jnp.concatenate(
            [pg[pl.ds(l * NSA_REC + s, chunks_per_page, stride=CMP_STRIDE * NSA_REC), :] for pg in pages],
            axis=0)
        a, b = _compress_partial(load, pe_ref.at[s // N_KV_B], w1_ref.at[s // N_KV_B], nchunk)
        a_ref[s] = a
        b_ref[s] = b


def _compress_pages(cache_rows, page_table, pe, w1, page):
    bsz, n_pages = page_table.shape
    cpp = page // CMP_STRIDE
    nchunk = PAGES_PER_STEP * cpp
    steps = n_pages // PAGES_PER_STEP
    page_specs = [
        pl.BlockSpec((page * NSA_REC, HEAD_DIM),
                     lambda b, s, pt, k=k: (pt[b, s * PAGES_PER_STEP + k], 0))
        for k in range(PAGES_PER_STEP)]
    out_spec = pl.BlockSpec((None, 2 * N_KV_B, nchunk, HEAD_DIM), lambda b, s, pt: (b, 0, s, 0))
    grid_spec = pltpu.PrefetchScalarGridSpec(
        num_scalar_prefetch=1,
        grid=(bsz, steps),
        in_specs=[pl.BlockSpec((2, CMP_LEN, HEAD_DIM), lambda b, s, pt: (0, 0, 0)),
                  pl.BlockSpec((2, CMP_LEN, HEAD_DIM, HEAD_DIM), lambda b, s, pt: (0, 0, 0, 0))]
                 + page_specs,
        out_specs=[out_spec, out_spec],
    )
    shape = jax.ShapeDtypeStruct((bsz, 2 * N_KV_B, n_pages * cpp, HEAD_DIM), F32)
    return pl.pallas_call(
        functools.partial(_compress_pages_kernel, chunks_per_page=cpp),
        grid_spec=grid_spec,
        out_shape=[shape, shape],
        compiler_params=_params("parallel", "parallel"),
        name="compress_pages",
    )(page_table, pe, w1, *([cache_rows] * PAGES_PER_STEP))


def _cmp_branch(qs, kcmp, vcmp, pos_rows):
    n = kcmp.shape[0]
    s = _dot_nt(qs, kcmp.astype(BF16)) * SCALE
    cmp_end = lax.broadcasted_iota(jnp.int32, (1, n), 1) * CMP_STRIDE + (CMP_LEN - 1)
    mask = cmp_end <= pos_rows
    s = jnp.where(mask, s, NEG_BIG)
    m = jnp.max(s, axis=-1, keepdims=True)
    e = jnp.where(mask, jnp.exp(s - m), 0.0)
    p = e / jnp.maximum(jnp.sum(e, axis=-1, keepdims=True), 1e-30)
    return _dot(p.astype(BF16), vcmp.astype(BF16)), p


def _select_blocks(imp, pos_rows, n_sel):
    r, width = imp.shape
    lane = lax.broadcasted_iota(jnp.int32, (r, width), 1)
    qblk = pos_rows // SLC_BLOCK
    forced = (lane == 0) | (lane == qblk) | (lane == qblk - 1)
    valid = lane * SLC_BLOCK <= pos_rows
    imp = jnp.where(forced, FORCE_SCORE, jnp.where(valid, imp, -FORCE_SCORE))
    imp = jnp.where(lane < n_sel, imp, -jnp.inf)
    rank = jnp.zeros((r, width), F32)
    for i in range(n_sel):
        ci = imp[:, i:i + 1]
        better = (ci > imp) | ((ci == imp) & (lane > i))
        rank = rank + jnp.where(better, 1.0, 0.0)
    return jnp.where((rank < float(min(N_SELECT, n_sel))) & (lane < n_sel), 1.0, 0.0)


def _apply_mask(x, mask, fill):
    n = mask.shape[0]
    if x.shape[0] == n:
        return jnp.where(mask, x, fill)
    return jnp.concatenate([jnp.where(mask, x[i:i + n], fill) for i in range(0, x.shape[0], n)], axis=0)


def _softmax_step(s, mask, v, carry, zero_masked=True):
    m, l, acc = carry
    s = _apply_mask(s, mask, NEG_BIG)
    m_new = jnp.maximum(m, jnp.max(s, axis=-1, keepdims=True))
    alpha = jnp.exp(m - m_new)
    e = jnp.exp(s - m_new)
    if zero_masked:
        e = _apply_mask(e, mask, 0.0)
    l = alpha * l + jnp.sum(e, axis=-1, keepdims=True)
    acc = alpha * acc + _dot(e.astype(BF16), v)
    return m_new, l, acc


def _softmax_init(r):
    return (jnp.full((r, 1), NEG_BIG, F32), jnp.zeros((r, 1), F32), jnp.zeros((r, HEAD_DIM), F32))


def _softmax_finish(carry):
    _, l, acc = carry
    return acc / jnp.maximum(l, 1e-30)


def _select_blocks_t(imp, pos, n_sel):
    rows, t = imp.shape
    blk = lax.broadcasted_iota(jnp.int32, (rows, t), 0)
    qblk = pos // SLC_BLOCK
    forced = (blk == 0) | (blk == qblk) | (blk == qblk - 1)
    valid = blk * SLC_BLOCK <= pos
    imp = jnp.where(forced, FORCE_SCORE, jnp.where(valid, imp, -FORCE_SCORE))
    imp = jnp.where(blk < n_sel, imp, -jnp.inf)
    rank = jnp.zeros((rows, t), F32)
    for i in range(n_sel):
        ri = imp[i:i + 1, :]
        better = (ri > imp) | ((ri == imp) & (blk > i))
        rank = rank + jnp.where(better, 1.0, 0.0)
    return jnp.where((rank < float(min(N_SELECT, n_sel))) & (blk < n_sel), 1.0, 0.0)


def _nsa_prompt_kernel(q_ref, g_ref, kc_ref, vc_ref, ks_ref, vs_ref, kw_ref, vw_ref,
                       mt_ref, e_ref, o_ref, *, tq, n_sel):
    i = pl.program_id(2)
    r = HPG * tq
    qs = jnp.concatenate([q_ref[:, h * HEAD_DIM:(h + 1) * HEAD_DIM] for h in range(HPG)],
                         axis=0).astype(BF16)
    pos_t = i * tq + lax.broadcasted_iota(jnp.int32, (tq, 1), 0)
    pos_r = jnp.concatenate([pos_t] * HPG, axis=0)

    o_c, p = _cmp_branch(qs, kc_ref[...], vc_ref[...], pos_r)
    psum = p[0:tq]
    for h in range(1, HPG):
        psum = psum + p[h * tq:(h + 1) * tq]
    hi = psum.astype(BF16)
    lo = (psum - hi.astype(F32)).astype(BF16)
    imp_t = _dot_nt(mt_ref[...], hi) + _dot_nt(mt_ref[...], lo)
    sel_rows = -(-n_sel // 8) * 8
    pos_lane = i * tq + lax.broadcasted_iota(jnp.int32, (1, tq), 1)
    sel_t = _select_blocks_t(imp_t[0:sel_rows], pos_lane, n_sel)
    sel = jnp.concatenate([sel_t, jnp.zeros((imp_t.shape[0] - sel_rows, tq), F32)], axis=0).T.astype(BF16)

    lane = lax.broadcasted_iota(jnp.int32, (tq, tq), 1)

    def slc_chunk(c, carry, diagonal):
        off = pl.multiple_of(c * tq, tq)
        k = ks_ref[pl.ds(off, tq), :].astype(BF16)
        v = vs_ref[pl.ds(off, tq), :].astype(BF16)
        hit = _dot(sel, e_ref[:, pl.ds(off, tq)])
        if diagonal:
            hit = jnp.where(off + lane <= pos_t, hit, 0.0)
        return _softmax_step(_dot_nt(qs, k) * SCALE, hit > 0.5, v, carry, zero_masked=False)

    carry = lax.fori_loop(0, i, lambda c, cr: slc_chunk(c, cr, False), _softmax_init(r))
    o_s = _softmax_finish(slc_chunk(i, carry, True))

    carry = _softmax_init(r)
    for d in range(-(-WINDOW // tq), -1, -1):
        c = i - d
        off = pl.multiple_of(jnp.maximum(c, 0) * tq, tq)
        k = kw_ref[pl.ds(off, tq), :].astype(BF16)
        v = vw_ref[pl.ds(off, tq), :].astype(BF16)
        kpos = c * tq + lane
        dlt = pos_t - kpos
        mask = (dlt >= 0) & (dlt < WINDOW) & (kpos >= 0)
        carry = _softmax_step(_dot_nt(qs, k) * SCALE, mask, v, carry, zero_masked=False)
    o_w = _softmax_finish(carry)

    for h in range(HPG):
        rows = slice(h * tq, (h + 1) * tq)
        gc = g_ref[:, 3 * h:3 * h + 1]
        gs = g_ref[:, 3 * h + 1:3 * h + 2]
        gw = g_ref[:, 3 * h + 2:3 * h + 3]
        o_ref[:, h * HEAD_DIM:(h + 1) * HEAD_DIM] = gc * o_c[rows] + gs * o_s[rows] + gw * o_w[rows]


def _cmp_to_slc(n_cmp_pad, n_cmp, n_sel, width):
    i = np.arange(n_cmp_pad)[:, None]
    j = np.arange(width)[None, :]
    lo = np.maximum(i * CMP_STRIDE, j * SLC_BLOCK)
    hi = np.minimum(i * CMP_STRIDE + CMP_LEN, (j + 1) * SLC_BLOCK)
    m = np.maximum(hi - lo, 0) / CMP_STRIDE
    m = np.where((i < n_cmp) & (j < n_sel), m, 0.0)
    return jnp.asarray(m, dtype=BF16)


def _nsa_prompt(qb, gates, cmp_kv, nsa, win, bsz, t):
    tq = min(512, t)
    nq = t // tq
    n_cmp = (t - CMP_LEN) // CMP_STRIDE + 1
    n_cmp_pad = cmp_kv.shape[2]
    n_sel = -(-t // SLC_BLOCK)
    assert n_sel <= LANES
    m_mat = _cmp_to_slc(n_cmp_pad, n_cmp, n_sel, LANES).T
    e_mat = jnp.asarray(np.arange(LANES)[:, None] == (np.arange(t) // SLC_BLOCK)[None, :], dtype=BF16)
    full = lambda shape: pl.BlockSpec(shape, lambda b, g, i: (0,) * len(shape))
    rows = lambda col: pl.BlockSpec((t, HEAD_DIM), lambda b, g, i, col=col: (b, col + g))
    return pl.pallas_call(
        functools.partial(_nsa_prompt_kernel, tq=tq, n_sel=n_sel),
        grid=(bsz, N_KV_B, nq),
        in_specs=[pl.BlockSpec((tq, HPG * HEAD_DIM), lambda b, g, i: (b * nq + i, g)),
                  pl.BlockSpec((tq, LANES), lambda b, g, i: (b * nq + i, g)),
                  pl.BlockSpec((None, None, n_cmp_pad, HEAD_DIM), lambda b, g, i: (b, g, 0, 0)),
                  pl.BlockSpec((None, None, n_cmp_pad, HEAD_DIM), lambda b, g, i: (b, 2 + g, 0, 0)),
                  rows(4), rows(6), rows(0), rows(2),
                  full(m_mat.shape), full(e_mat.shape)],
        out_specs=pl.BlockSpec((tq, HPG * HEAD_DIM), lambda b, g, i: (b * nq + i, g)),
        out_shape=jax.ShapeDtypeStruct((bsz * t, D_B), F32),
        compiler_params=_params("parallel", "parallel", "parallel"),
        name="nsa_prompt",
    )(qb, gates, cmp_kv, cmp_kv, nsa, nsa, win, win, m_mat, e_mat)


def _nsa_select_kernel(q_ref, ak_ref, av_ref, bk_ref, bv_ref, w2_ref, m_ref, oc_ref, sel_ref,
                       *, n_new, past, n_sel):
    qs = jnp.concatenate([q_ref[:, h * HEAD_DIM:(h + 1) * HEAD_DIM] for h in range(HPG)],
                         axis=0).astype(BF16)
    pos_t = past + lax.broadcasted_iota(jnp.int32, (n_new, 1), 0)
    pos_r = jnp.concatenate([pos_t] * HPG, axis=0)
    kcmp = _compress_finish(ak_ref[...], bk_ref[...], w2_ref[0])
    vcmp = _compress_finish(av_ref[...], bv_ref[...], w2_ref[1])
    o_c, p = _cmp_branch(qs, kcmp, vcmp, pos_r)
    oc_ref[...] = o_c
    psum = p[0:n_new]
    for h in range(1, HPG):
        psum = psum + p[h * n_new:(h + 1) * n_new]
    sel = _select_blocks(_dot_split(psum, m_ref[...]), pos_t, n_sel)
    sel_ref[...] = jnp.concatenate([sel] * HPG, axis=0)


def _nsa_select(qb, part_a, part_b, w2, past):
    bsz, n_new, _ = qb.shape
    n_cmp_pad = part_a.shape[2]
    total = past + n_new
    n_cmp = (total - CMP_LEN) // CMP_STRIDE + 1
    n_sel = -(-total // SLC_BLOCK)
    width = -(-n_sel // LANES) * LANES
    m_mat = _cmp_to_slc(n_cmp_pad, n_cmp, n_sel, width)
    assert n_cmp == n_cmp_pad - 1, "compressed tokens must come from the paged rows only"
    r = HPG * n_new
    k_spec = pl.BlockSpec((None, None, n_cmp_pad, HEAD_DIM), lambda b, g: (b, g, 0, 0))
    v_spec = pl.BlockSpec((None, None, n_cmp_pad, HEAD_DIM), lambda b, g: (b, N_KV_B + g, 0, 0))
    return pl.pallas_call(
        functools.partial(_nsa_select_kernel, n_new=n_new, past=past, n_sel=n_sel),
        grid=(bsz, N_KV_B),
        in_specs=[pl.BlockSpec((None, n_new, HPG * HEAD_DIM), lambda b, g: (b, 0, g)),
                  k_spec, v_spec, k_spec, v_spec,
                  pl.BlockSpec((2, HEAD_DIM, HEAD_DIM), lambda b, g: (0, 0, 0)),
                  pl.BlockSpec(m_mat.shape, lambda b, g: (0, 0))],
        out_specs=[pl.BlockSpec((None, None, r, HEAD_DIM), lambda b, g: (b, g, 0, 0)),
                   pl.BlockSpec((None, None, r, width), lambda b, g: (b, g, 0, 0))],
        out_shape=[jax.ShapeDtypeStruct((bsz, N_KV_B, r, HEAD_DIM), F32),
                   jax.ShapeDtypeStruct((bsz, N_KV_B, r, width), F32)],
        compiler_params=_params("parallel", "parallel"),
        name="nsa_select",
    )(qb, part_a, part_a, part_b, part_b, w2, m_mat)


SLC_PAGES_PER_STEP = 8


def _nsa_slc_decode_kernel(pt_ref, q_ref, sel_ref, new_ref, *refs, n_new, past, page):
    pps = SLC_PAGES_PER_STEP
    pages = refs[:pps]
    o_ref, m_ref, l_ref, acc_ref = refs[pps:]
    p = pl.program_id(1)
    n_steps = pl.num_programs(1) - 1
    r = HPG * n_new
    rows = N_KV_B * r
    t_of_row = lax.broadcasted_iota(jnp.int32, (rows, 1), 0) % n_new
    lane = lax.broadcasted_iota(jnp.int32, (rows, page), 1)
    bpp = page // SLC_BLOCK
    qg = [q_ref[g].astype(BF16) for g in range(N_KV_B)]
    sel = jnp.concatenate([sel_ref[g] for g in range(N_KV_B)], axis=0)
    blk_lane = lax.broadcasted_iota(jnp.int32, sel.shape, 1)

    @pl.when(p == 0)
    def _():
        m_ref[...] = jnp.full(m_ref.shape, NEG_BIG, F32)
        l_ref[...] = jnp.zeros(l_ref.shape, F32)
        acc_ref[...] = jnp.zeros(acc_ref.shape, F32)

    def masked_scores(ref, first_pos):
        first_blk = first_pos // SLC_BLOCK
        hit = jnp.zeros((rows, page), F32)
        for j in range(bpp):
            flag = jnp.sum(jnp.where(blk_lane == first_blk + j, sel, 0.0), axis=-1, keepdims=True)
            hit = jnp.where(lane // SLC_BLOCK == j, flag, hit)
        mask = (hit > 0.5) & (first_pos + lane <= past + t_of_row)
        s = jnp.concatenate(
            [_dot_nt(qg[g], ref[pl.ds(2 * N_KV_B + g, page, stride=NSA_REC), :].astype(BF16))
             for g in range(N_KV_B)], axis=0) * SCALE
        return jnp.where(mask, s, NEG_BIG), mask

    def visit(blocks):
        scored = [masked_scores(ref, pos) for ref, pos in blocks]
        m_old = m_ref[...]
        m_new = m_old
        for s, _ in scored:
            m_new = jnp.maximum(m_new, jnp.max(s, axis=-1, keepdims=True))
        alpha = jnp.exp(m_old - m_new)
        l = alpha * l_ref[...]
        acc = alpha * acc_ref[...]
        for (ref, _), (s, mask) in zip(blocks, scored):
            e = jnp.where(mask, jnp.exp(s - m_new), 0.0)
            l = l + jnp.sum(e, axis=-1, keepdims=True)
            eb = e.astype(BF16)
            acc = acc + jnp.concatenate(
                [_dot(eb[g * r:(g + 1) * r], ref[pl.ds(3 * N_KV_B + g, page, stride=NSA_REC), :].astype(BF16))
                 for g in range(N_KV_B)], axis=0)
        m_ref[...] = m_new
        l_ref[...] = l
        acc_ref[...] = acc

    @pl.when(p < n_steps)
    def _():
        visit([(pg, (p * pps + k) * page) for k, pg in enumerate(pages)])

    @pl.when(p == n_steps)
    def _():
        visit([(new_ref, past)])
        o_ref[...] = acc_ref[...] / jnp.maximum(l_ref[...], 1e-30)


def _nsa_slc_decode(q_groups, sel, new_rows, cache_rows, page_table, past, page, n_new):
    bsz = q_groups.shape[0]
    n_pages = page_table.shape[1]
    pps = SLC_PAGES_PER_STEP
    steps = n_pages // pps
    r = HPG * n_new
    rows = N_KV_B * r
    width = sel.shape[-1]
    blk = page * NSA_REC
    page_specs = [
        pl.BlockSpec((blk, HEAD_DIM),
                     lambda b, p, pt, k=k: (pt[b, jnp.minimum(p, steps - 1) * pps + k], 0))
        for k in range(pps)]
    grid_spec = pltpu.PrefetchScalarGridSpec(
        num_scalar_prefetch=1,
        grid=(bsz, steps + 1),
        in_specs=[pl.BlockSpec((None, N_KV_B, r, HEAD_DIM), lambda b, p, pt: (b, 0, 0, 0)),
                  pl.BlockSpec((None, N_KV_B, r, width), lambda b, p, pt: (b, 0, 0, 0)),
                  pl.BlockSpec((None, blk, HEAD_DIM), lambda b, p, pt: (b, 0, 0))] + page_specs,
        out_specs=pl.BlockSpec((None, rows, HEAD_DIM), lambda b, p, pt: (b, 0, 0)),
        scratch_shapes=[pltpu.VMEM((rows, 1), F32), pltpu.VMEM((rows, 1), F32),
                        pltpu.VMEM((rows, HEAD_DIM), F32)],
    )
    return pl.pallas_call(
        functools.partial(_nsa_slc_decode_kernel, n_new=n_new, past=past, page=page),
        grid_spec=grid_spec,
        out_shape=jax.ShapeDtypeStruct((bsz, rows, HEAD_DIM), F32),
        compiler_params=_params("parallel", "arbitrary"),
        name="nsa_slc_decode",
    )(page_table, q_groups, sel, new_rows, *([cache_rows] * pps))


def _nsa_merge_decode_kernel(q_ref, g_ref, oc_ref, os_ref, st_ref, new_ref, o_ref, nw_ref,
                             *, n_new, past, n_buf):
    r = HPG * n_new
    n_pad = new_ref.shape[0]
    t_of_row = lax.broadcasted_iota(jnp.int32, (r, 1), 0) % n_new
    pos_r = past + t_of_row
    buf_pos = past - n_buf + lax.broadcasted_iota(jnp.int32, (r, n_buf), 1)
    new_pos = past + lax.broadcasted_iota(jnp.int32, (r, n_pad), 1)

    def in_window(kpos):
        dlt = pos_r - kpos
        return (dlt >= 0) & (dlt < WINDOW) & (kpos >= 0)

    for g in range(N_KV_B):
        qs = jnp.concatenate(
            [q_ref[:, (g * HPG + h) * HEAD_DIM:(g * HPG + h + 1) * HEAD_DIM] for h in range(HPG)],
            axis=0).astype(BF16)
        kcol = slice(g * HEAD_DIM, (g + 1) * HEAD_DIM)
        vcol = slice((N_KV_B + g) * HEAD_DIM, (N_KV_B + g + 1) * HEAD_DIM)
        carry = _softmax_init(r)
        carry = _softmax_step(_dot_nt(qs, st_ref[:, kcol].astype(BF16)) * SCALE, in_window(buf_pos),
                              st_ref[:, vcol].astype(BF16), carry)
        carry = _softmax_step(_dot_nt(qs, new_ref[:, kcol].astype(BF16)) * SCALE, in_window(new_pos),
                              new_ref[:, vcol].astype(BF16), carry)
        o_w = _softmax_finish(carry)
        for h in range(HPG):
            rows = slice(h * n_new, (h + 1) * n_new)
            c0 = g * LANES + 3 * h
            out = (g_ref[:, c0:c0 + 1] * oc_ref[g, rows] + g_ref[:, c0 + 1:c0 + 2] * os_ref[g, rows]
                   + g_ref[:, c0 + 2:c0 + 3] * o_w[rows])
            o_ref[:, (g * HPG + h) * HEAD_DIM:(g * HPG + h + 1) * HEAD_DIM] = out
    shifted = pltpu.roll(st_ref[...], n_buf - n_new, 0)
    placed = pltpu.roll(new_ref[...], n_pad - n_new, 0)
    tail_row = lax.broadcasted_iota(jnp.int32, (n_pad, placed.shape[1]), 0)
    nw_ref[0:n_buf - n_pad, :] = shifted[0:n_buf - n_pad]
    nw_ref[n_buf - n_pad:n_buf, :] = jnp.where(tail_row >= n_pad - n_new, placed, shifted[n_buf - n_pad:])


def _nsa_merge_decode(qb, gates, o_cmp, o_slc, state_win, win_new, past):
    bsz, n_new, _ = qb.shape
    n_buf = state_win.shape[1]
    n_pad = win_new.shape[1]
    keep = min(WINDOW, past + n_new)
    assert keep == n_buf and n_buf > n_pad >= n_new
    r = HPG * n_new
    cols = 2 * D_KV_B
    per_b = lambda shape: pl.BlockSpec((None,) + shape, lambda b: (b,) + (0,) * len(shape))
    return pl.pallas_call(
        functools.partial(_nsa_merge_decode_kernel, n_new=n_new, past=past, n_buf=n_buf),
        grid=(bsz,),
        in_specs=[per_b((n_new, D_B)), per_b((n_new, 2 * LANES)),
                  per_b((N_KV_B, r, HEAD_DIM)), per_b((N_KV_B, r, HEAD_DIM)),
                  per_b((n_buf, cols)), per_b((n_pad, cols))],
        out_specs=[per_b((n_new, D_B)), per_b((keep, cols))],
        out_shape=[jax.ShapeDtypeStruct((bsz, n_new, D_B), F32),
                   jax.ShapeDtypeStruct((bsz, keep, cols), F32)],
        compiler_params=_params("parallel"),
        name="nsa_merge_decode",
    )(qb, gates, o_cmp, o_slc, state_win, win_new)


def _outproj_kernel(x_ref, oa_ref, ob_ref, wa_ref, wb_ref, g_ref, b_ref, h_ref):
    y = _dot(oa_ref[...].astype(BF16), wa_ref[...]) + _dot(ob_ref[...].astype(BF16), wb_ref[...])
    h_ref[...] = _layer_norm(ALPHA * x_ref[...] + y, g_ref[...], b_ref[...])


def _outproj(x, o_a, o_b, wa, wb, g, b):
    m, d = x.shape
    tm = min(512, m)
    row = lambda w: pl.BlockSpec((tm, w), lambda i: (i, 0))
    full = lambda a: pl.BlockSpec(a.shape, lambda i: (0, 0))
    return pl.pallas_call(
        _outproj_kernel,
        grid=(m // tm,),
        in_specs=[row(d), row(D_A), row(D_B), full(wa), full(wb), full(g), full(b)],
        out_specs=row(d),
        out_shape=jax.ShapeDtypeStruct((m, d), F32),
        compiler_params=_params("parallel"),
        name="outproj_ln",
    )(x, o_a, o_b, wa, wb, g, b)


def _peer_query_kernel(h_ref, w_ref, q_ref):
    q_ref[...] = _dot(h_ref[...].astype(BF16), w_ref[...])


def _peer_query(h, wq):
    m, d = h.shape
    n = wq.shape[1]
    tm = min(512, m)
    return pl.pallas_call(
        _peer_query_kernel,
        grid=(m // tm,),
        in_specs=[pl.BlockSpec((tm, d), lambda i: (i, 0)), pl.BlockSpec((d, n), lambda i: (0, 0))],
        out_specs=pl.BlockSpec((tm, n), lambda i: (i, 0)),
        out_shape=jax.ShapeDtypeStruct((m, n), F32),
        compiler_params=_params("parallel"),
        name="peer_query",
    )(h, wq)


def _compare_exchange(v, i, j, descending):
    hi, lo = jnp.maximum(v[i], v[j]), jnp.minimum(v[i], v[j])
    v[i], v[j] = (hi, lo) if descending else (lo, hi)


def _bitonic_merge_desc(v):
    j = len(v) // 2
    while j >= 1:
        for i in range(len(v)):
            if i ^ j > i:
                _compare_exchange(v, i, i ^ j, True)
        j //= 2


def _bitonic_sort_desc(v):
    n = len(v)
    k = 2
    while k <= n:
        j = k // 2
        while j >= 1:
            for i in range(n):
                if i ^ j > i:
                    _compare_exchange(v, i, i ^ j, (i & k) == 0)
            j //= 2
        k *= 2


def _top_sorted(x, k):
    v = [x[i:i + SUBLANES] for i in range(0, x.shape[0], SUBLANES)]
    _bitonic_sort_desc(v)
    d = 1
    while d < SUBLANES:
        other = [pltpu.roll(a, d, 0) for a in v]
        if 2 * len(v) <= k:
            v = v + other[::-1]
        else:
            v = [jnp.maximum(v[i], other[k - 1 - i]) for i in range(k)]
        _bitonic_merge_desc(v)
        d *= 2
    return [a[0:1] for a in v]


_PEER_PAIRS = [(i, j) for i in range(PEER_TOPK) for j in range(PEER_TOPK) if (i + 1) * (j + 1) <= PEER_TOPK]
_PEER_CAND_ROWS = SUBLANES * (1 << (-(-len(_PEER_PAIRS) // SUBLANES) - 1).bit_length())


def _peer_route_kernel(q_ref, sk_ref, thr_ref, s2_ref, e1_ref, e2_ref, cand_ref):
    half = sk_ref.shape[2]
    cand_ref[...] = jnp.full(cand_ref.shape, -jnp.inf, F32)
    for h in range(PEER_HEADS):
        s, tops = [], []
        for c in range(2):
            col = (2 * h + c) * half
            st = _dot_nt(sk_ref[c], q_ref[:, col:col + half].astype(BF16))
            s.append(st)
            tops.append(_top_sorted(st, PEER_TOPK))
        for n, (i, j) in enumerate(_PEER_PAIRS):
            cand_ref[n:n + 1, :] = tops[0][i] + tops[1][j]
        best = _top_sorted(cand_ref[...], PEER_TOPK)
        top = best[0]
        z = jnp.ones_like(top)
        for v in best[1:]:
            z = z + jnp.exp(v - top)
        tau = best[PEER_TOPK - 1]
        thr = jnp.full(s[0].shape, jnp.inf, F32)
        for i in reversed(range(PEER_TOPK)):
            thr_i = jnp.full_like(tau, jnp.inf)
            for j in range(PEER_TOPK):
                if (i + 1) * (j + 1) <= PEER_TOPK:
                    thr_i = jnp.where(tops[0][i] + tops[1][j] >= tau, tops[1][j], thr_i)
            thr = jnp.where(s[0] == tops[0][i], thr_i, thr)
        thr_ref[h] = thr
        s2_ref[h] = s[1]
        e1_ref[h] = jnp.exp(s[0] - tops[0][0])
        e2_ref[h] = jnp.exp(s[1] - tops[1][0]) / z


def _peer_route(q, sub_keys):
    n = q.shape[0]
    tm = 128
    nk = sub_keys.shape[1]
    big = jax.ShapeDtypeStruct((PEER_HEADS, nk, n), F32)
    big_spec = pl.BlockSpec((PEER_HEADS, nk, tm), lambda i: (0, 0, i))
    return pl.pallas_call(
        _peer_route_kernel,
        grid=(n // tm,),
        in_specs=[pl.BlockSpec((tm, q.shape[1]), lambda i: (i, 0)),
                  pl.BlockSpec(sub_keys.shape, lambda i: (0, 0, 0))],
        out_specs=[big_spec] * 4,
        out_shape=[big] * 4,
        scratch_shapes=[pltpu.VMEM((_PEER_CAND_ROWS, tm), F32)],
        compiler_params=_params("parallel"),
        name="peer_route",
    )(q, sub_keys)


def _peer_dense_kernel(h_ref, u_ref, v_ref, thr_ref, s2_ref, e1_ref, e2_ref, g_ref, b_ref,
                       o_ref, hb_ref, wt_ref, *, tm, ac, sub_ac):
    c = pl.program_id(1)
    nk = s2_ref.shape[1]
    kb_rows = nk // 4

    @pl.when(c == 0)
    def _():
        o_ref[...] = jnp.zeros(o_ref.shape, F32)
        hb_ref[...] = h_ref[...].astype(BF16)

    for sub in range(ac // sub_ac):
        ex = slice(sub * sub_ac * nk, (sub + 1) * sub_ac * nk)
        act = jax.nn.gelu(_dot_nt(u_ref[ex, :], hb_ref[...]))
        for ts in range(tm // LANES):
            tok = slice(ts * LANES, (ts + 1) * LANES)
            for kb in range(nk // kb_rows):
                keys = slice(kb * kb_rows, (kb + 1) * kb_rows)
                gates = [jnp.zeros((kb_rows, LANES), F32) for _ in range(sub_ac)]
                for h in range(PEER_HEADS):
                    s2 = s2_ref[h, keys, tok]
                    e2 = e2_ref[h, keys, tok]
                    for k in range(sub_ac):
                        aa = sub * sub_ac + k
                        chosen = s2 >= thr_ref[h, aa:aa + 1, tok]
                        gates[k] = gates[k] + jnp.where(chosen, e1_ref[h, aa:aa + 1, tok] * e2, 0.0)
                for k in range(sub_ac):
                    rows = slice(k * nk + kb * kb_rows, k * nk + (kb + 1) * kb_rows)
                    wt_ref[sub * sub_ac * nk + rows.start:sub * sub_ac * nk + rows.stop, tok] = (
                        gates[k] * act[rows, tok]).astype(BF16)
    o_ref[...] += _dot_tn(wt_ref[...], v_ref[...])

    @pl.when(c == pl.num_programs(1) - 1)
    def _():
        o_ref[...] = _layer_norm(ALPHA * h_ref[...] + o_ref[...], g_ref[...], b_ref[...])


def _peer_dense(h, u, v, route, g, b):
    n, d = h.shape
    thr, s2, e1, e2 = route
    nk = s2.shape[1]
    tm = min(512, n)
    ac = 8
    big_spec = pl.BlockSpec((PEER_HEADS, nk, tm), lambda i, c: (0, 0, i))
    row_spec = pl.BlockSpec((PEER_HEADS, ac, tm), lambda i, c: (0, c, i))
    tab_spec = pl.BlockSpec((ac * nk, d), lambda i, c: (c, 0))
    full = lambda a: pl.BlockSpec(a.shape, lambda i, c: (0, 0))
    return pl.pallas_call(
        functools.partial(_peer_dense_kernel, tm=tm, ac=ac, sub_ac=4),
        grid=(n // tm, nk // ac),
        in_specs=[pl.BlockSpec((tm, d), lambda i, c: (i, 0)), tab_spec, tab_spec,
                  row_spec, big_spec, row_spec, big_spec, full(g), full(b)],
        out_specs=pl.BlockSpec((tm, d), lambda i, c: (i, 0)),
        out_shape=jax.ShapeDtypeStruct((n, d), F32),
        scratch_shapes=[pltpu.VMEM((tm, d), BF16), pltpu.VMEM((ac * nk, tm), BF16)],
        compiler_params=_params("parallel", "arbitrary"),
        name="peer_dense",
    )(h, u, v, thr, s2, e1, e2, g, b)


def _post_block(x, o_a, o_b, wts):
    h = _outproj(x, o_a, o_b, wts["wo_a"], wts["wo_b"], wts["ln1_g"], wts["ln1_b"])
    q = _peer_query(h, wts["wq"])
    route = _peer_route(q, wts["sub_keys"])
    return _peer_dense(h, wts["u"], wts["v"], route, wts["ln2_g"], wts["ln2_b"])


def _pad_rows(a, rows):
    return jnp.pad(a, ((0, rows - a.shape[0]),) + ((0, 0),) * (a.ndim - 1))


def kernel(x_prompt, x_sample, cache_a, cache_nsa, state_win, page_table, w_in, cmp_pe_k, cmp_w1_k,
           cmp_w2_k, cmp_pe_v, cmp_w1_v, cmp_w2_v, w_out, ln1_g, ln1_b, peer_w_query, peer_sub_keys,
           peer_u, peer_v, ln2_g, ln2_b):
    assert w_in.shape[0] == DEPTH == 1
    bsz, t, d = x_prompt.shape
    dbsz, n_new, _ = x_sample.shape
    n_pool, page = cache_a.shape[1], cache_a.shape[2]
    past = page_table.shape[1] * page

    w = w_in[0]
    o = np.cumsum((0, D_A, 2 * D_A, D_B, 4 * D_KV_B, 2 * D_KV_B, 3 * N_HEADS_B))
    wg = w[:, o[5]:o[6]]
    per_g = 3 * HPG
    gate_w = jnp.concatenate(
        [jnp.pad(wg[:, g * per_g:(g + 1) * per_g], ((0, 0), (0, LANES - per_g))) for g in range(N_KV_B)],
        axis=1)
    wparts = {"qa": w[:, o[0]:o[1]], "kva": w[:, o[1]:o[2]], "qb": w[:, o[2]:o[3]],
              "nsa": w[:, o[3]:o[4]], "win": w[:, o[4]:o[5]], "gates": gate_w}
    wparts = {k: v.astype(BF16) for k, v in wparts.items()}
    row2 = lambda a: a[0].reshape(1, -1)
    wts = {"wo_a": w_out[0, :D_A].astype(BF16), "wo_b": w_out[0, D_A:].astype(BF16),
           "ln1_g": row2(ln1_g), "ln1_b": row2(ln1_b), "ln2_g": row2(ln2_g), "ln2_b": row2(ln2_b),
           "wq": peer_w_query[0].astype(BF16), "sub_keys": peer_sub_keys[0].astype(BF16),
           "u": peer_u[0].astype(BF16), "v": peer_v[0].astype(BF16)}
    pe = jnp.stack([cmp_pe_k[0], cmp_pe_v[0]])
    w1 = jnp.stack([cmp_w1_k[0], cmp_w1_v[0]]).astype(BF16)
    w2 = jnp.stack([cmp_w2_k[0], cmp_w2_v[0]]).astype(BF16)

    xp = x_prompt.reshape(bsz * t, d)
    tabs_p = _rope_tables(jnp.arange(t, dtype=jnp.int32))
    qa, kva, qb, nsa, win, gates = _project_all(xp.astype(BF16), wparts, tabs_p)
    o_a = _sb_prompt(qa, kva, bsz, t)
    cmp_kv = _compress_prompt(nsa, pe, w1, w2, bsz, t)
    o_b = _nsa_prompt(qb, gates, cmp_kv, nsa, win, bsz, t)
    y_prompt = _post_block(xp, o_a, o_b, wts).reshape(bsz, t, d)
    new_a_p = kva.reshape(1, bsz, t, 2, N_HEADS_A, HEAD_DIM)
    new_nsa_p = nsa.reshape(1, bsz, t, 4, N_KV_B, HEAD_DIM)
    keep_p = min(WINDOW, t)
    new_win_p = win.reshape(bsz, t, 2, N_KV_B, HEAD_DIM)[None, :, t - keep_p:]

    ns = dbsz * n_new
    xs = x_sample.reshape(ns, d)
    pos_s = past + jnp.arange(n_new, dtype=jnp.int32)
    tabs_s = tuple(jnp.tile(tb, (dbsz, 1)) for tb in _rope_tables(pos_s))
    qa_s, kva_s, qb_s, nsa_s, win_s, gates_s = _project_all(xs.astype(BF16), wparts, tabs_s)
    by_b = lambda a: a.reshape(dbsz, n_new, a.shape[-1])
    pad_page = lambda a: jnp.pad(by_b(a), ((0, 0), (0, page - n_new), (0, 0)))
    as_records = lambda a, rec: jnp.pad(a.reshape(dbsz, n_new * rec, HEAD_DIM),
                                        ((0, 0), (0, (page - n_new) * rec), (0, 0)))
    q_heads = jnp.pad(qa_s.reshape(dbsz, n_new, N_HEADS_A, HEAD_DIM).transpose(0, 2, 1, 3),
                      ((0, 0), (0, 0), (0, PAD_T - n_new), (0, 0)))
    o_a_h = _sb_decode(q_heads, as_records(kva_s, 2 * N_HEADS_A),
                       cache_a.reshape(n_pool * page * 2 * N_HEADS_A, HEAD_DIM), page_table, page)
    o_a_s = o_a_h.reshape(dbsz, N_HEADS_A, PAD_T, HEAD_DIM)[:, :, :n_new].transpose(0, 2, 1, 3)
    cache_n = cache_nsa.reshape(n_pool * page * NSA_REC, HEAD_DIM)
    part_a, part_b = _compress_pages(cache_n, page_table, pe, w1, page)
    o_cmp, sel = _nsa_select(by_b(qb_s), part_a, part_b, w2, past)
    q_groups = qb_s.reshape(dbsz, n_new, N_KV_B, HPG, HEAD_DIM).transpose(0, 2, 3, 1, 4).reshape(
        dbsz, N_KV_B, HPG * n_new, HEAD_DIM)
    o_slc = _nsa_slc_decode(q_groups, sel, as_records(nsa_s, NSA_REC), cache_n, page_table, past, page,
                            n_new).reshape(dbsz, N_KV_B, HPG * n_new, HEAD_DIM)
    n_buf = state_win.shape[2]
    o_b_s, new_win = _nsa_merge_decode(by_b(qb_s), by_b(gates_s), o_cmp, o_slc,
                                       state_win[0].reshape(dbsz, n_buf, 2 * D_KV_B), pad_page(win_s), past)
    rows_s = -(-ns // LANES) * LANES
    y_s = _post_block(_pad_rows(xs, rows_s), _pad_rows(o_a_s.reshape(ns, D_A), rows_s),
                      _pad_rows(o_b_s.reshape(ns, D_B), rows_s), wts)
    y_sample = y_s[:ns].reshape(dbsz, n_new, d)
    new_a_s = kva_s.reshape(1, dbsz, n_new, 2, N_HEADS_A, HEAD_DIM)
    new_nsa_s = nsa_s.reshape(1, dbsz, n_new, 4, N_KV_B, HEAD_DIM)
    new_win_s = new_win.reshape(1, dbsz, new_win.shape[1], 2, N_KV_B, HEAD_DIM)
    return (y_prompt, y_sample, new_a_p, new_nsa_p, new_win_p, new_a_s, new_nsa_s, new_win_s)
```

```python
import functools
import math

import numpy as np
import jax
import jax.numpy as jnp
from jax import lax
from jax.experimental import pallas as pl
from jax.experimental.pallas import tpu as pltpu

F32 = jnp.float32
BF16 = jnp.bfloat16

LANES = 128
SUBLANES = 8
VMEM_LIMIT = 56 * 1024 * 1024

HEAD_DIM = 128
N_HEADS_A = 8
N_HEADS_B = 8
N_KV_B = 2
HPG = N_HEADS_B // N_KV_B
D_A = N_HEADS_A * HEAD_DIM
D_B = N_HEADS_B * HEAD_DIM
D_KV_B = N_KV_B * HEAD_DIM
ROPE_DIM = HEAD_DIM // 4
ROPE_THETA = 500000.0
CMP_LEN = 32
CMP_STRIDE = 16
SLC_BLOCK = 64
N_SELECT = 16
WINDOW = 512
FORCE_SCORE = 1.0e4
PEER_HEADS = 8
PEER_NKEYS = 128
PEER_TOPK = 16
DEPTH = 1
ALPHA = (2 * DEPTH) ** 0.25
LN_EPS = 1e-5
NEG_BIG = -1e30
SCALE = 1.0 / math.sqrt(HEAD_DIM)


def _dot(a, b):
    return jnp.dot(a, b, preferred_element_type=F32)


def _dot_nt(a, b):
    return lax.dot_general(a, b, (((1,), (1,)), ((), ())), preferred_element_type=F32)


def _dot_tn(a, b):
    return lax.dot_general(a, b, (((0,), (0,)), ((), ())), preferred_element_type=F32)


def _dot_split(x, w):
    hi = x.astype(BF16)
    lo = (x - hi.astype(F32)).astype(BF16)
    return _dot(hi, w) + _dot(lo, w)


def _params(*sem):
    return pltpu.CompilerParams(dimension_semantics=sem, vmem_limit_bytes=VMEM_LIMIT)


def _layer_norm(r, g, b):
    mu = jnp.mean(r, axis=-1, keepdims=True)
    d = r - mu
    var = jnp.mean(d * d, axis=-1, keepdims=True)
    return d * lax.rsqrt(var + LN_EPS) * g + b


def _inproj_kernel(x_ref, w_ref, c_ref, sa_ref, sb_ref, o_ref, *, rope_flags, sigmoid):
    acc = _dot(x_ref[...], w_ref[...])
    for j, flag in enumerate(rope_flags):
        blk = acc[:, j * LANES:(j + 1) * LANES]
        if flag:
            blk = (blk * c_ref[...]
                   + pltpu.roll(blk, LANES - ROPE_DIM // 2, 1) * sa_ref[...]
                   + pltpu.roll(blk, ROPE_DIM // 2, 1) * sb_ref[...])
        if sigmoid:
            blk = jax.nn.sigmoid(blk)
        o_ref[:, j * LANES:(j + 1) * LANES] = blk


def _inproj(xb, w, tabs, rope_flags, sigmoid=False):
    m, k = xb.shape
    n = w.shape[1]
    c, sa, sb = tabs
    tm = min(1024, m)
    nt = c.shape[0] // tm
    tab_spec = pl.BlockSpec((tm, LANES), lambda i: (i % nt, 0))
    return pl.pallas_call(
        functools.partial(_inproj_kernel, rope_flags=tuple(rope_flags), sigmoid=sigmoid),
        grid=(m // tm,),
        in_specs=[pl.BlockSpec((tm, k), lambda i: (i, 0)),
                  pl.BlockSpec((k, n), lambda i: (0, 0)),
                  tab_spec, tab_spec, tab_spec],
        out_specs=pl.BlockSpec((tm, n), lambda i: (i, 0)),
        out_shape=jax.ShapeDtypeStruct((m, n), F32),
        compiler_params=_params("parallel"),
        name="inproj",
    )(xb, w, c, sa, sb)


def _rope_tables(pos):
    half = ROPE_DIM // 2
    inv = ROPE_THETA ** (-jnp.arange(half, dtype=F32) / half)
    ang = pos.astype(F32)[:, None] * inv[None, :]
    cos, sin = jnp.cos(ang), jnp.sin(ang)
    t = pos.shape[0]
    ones = jnp.ones((t, LANES - ROPE_DIM), F32)
    zeros = jnp.zeros((t, LANES - half), F32)
    c = jnp.concatenate([cos, cos, ones], axis=1)
    sa = jnp.concatenate([-sin, zeros], axis=1)
    sb = jnp.concatenate([jnp.zeros((t, half), F32), sin, jnp.zeros((t, LANES - ROPE_DIM), F32)], axis=1)
    return c, sa, sb


def _project_all(xb, wparts, tabs):
    qa = _inproj(xb, wparts["qa"], tabs, [0] * 8)
    kva = _inproj(xb, wparts["kva"], tabs, [0] * 16)
    qb = _inproj(xb, wparts["qb"], tabs, [1] * 8)
    nsa = _inproj(xb, wparts["nsa"], tabs, [1, 1, 0, 0, 1, 1, 0, 0])
    win = _inproj(xb, wparts["win"], tabs, [1, 1, 0, 0])
    gates = _inproj(xb, wparts["gates"], tabs, [0, 0], sigmoid=True)
    return qa, kva, qb, nsa, win, gates


def _sb_block(q, k, v, mask, c, tri):
    ls, lr = _sb_logs(_dot_nt(q, k), mask)
    return _sb_weights(ls, lr, mask, c, tri), c + jnp.sum(lr, axis=1, keepdims=True)


def _sb_logs(qk, mask):
    z = qk * SCALE
    ls = jnp.minimum(z, 0.0) - jnp.log(1.0 + jnp.exp(-jnp.abs(z)))
    lr = ls - z
    return ls, (lr if mask is None else jnp.where(mask, lr, 0.0))


def _sb_weights(ls, lr, mask, c, tri):
    w = jnp.exp(ls + _dot_split(lr, tri) + c)
    return w if mask is None else jnp.where(mask, w, 0.0)


def _tri(n):
    row = lax.broadcasted_iota(jnp.int32, (n, n), 0)
    col = lax.broadcasted_iota(jnp.int32, (n, n), 1)
    return (row > col).astype(BF16)


def _sb_prompt_kernel(q_ref, k_ref, v_ref, o_ref, *, tq, tk):
    i = pl.program_id(2)
    q = q_ref[...].astype(BF16)
    tri = _tri(tk)
    per_q = tq // tk
    row = i * tq + lax.broadcasted_iota(jnp.int32, (tq, tk), 0)
    col = lax.broadcasted_iota(jnp.int32, (tq, tk), 1)

    def block(j, carry, masked):
        c, acc = carry
        off = pl.multiple_of(j * tk, tk)
        k = k_ref[pl.ds(off, tk), :].astype(BF16)
        v = v_ref[pl.ds(off, tk), :].astype(BF16)
        w, c = _sb_block(q, k, v, (off + col < row) if masked else None, c, tri)
        return c, acc + _dot(w.astype(BF16), v)

    carry = (jnp.zeros((tq, 1), F32), jnp.zeros((tq, HEAD_DIM), F32))
    for d in range(per_q):
        carry = block((i + 1) * per_q - 1 - d, carry, True)
    _, acc = lax.fori_loop(0, i * per_q, lambda s, cr: block(i * per_q - 1 - s, cr, False), carry)
    o_ref[...] = acc


def _sb_prompt(qa, kva, bsz, t):
    tq, tk = min(1024, t), 256
    nq = t // tq
    return pl.pallas_call(
        functools.partial(_sb_prompt_kernel, tq=tq, tk=tk),
        grid=(bsz, N_HEADS_A, nq),
        in_specs=[pl.BlockSpec((tq, HEAD_DIM), lambda b, h, i: (b * nq + i, h)),
                  pl.BlockSpec((t, HEAD_DIM), lambda b, h, i: (b, h)),
                  pl.BlockSpec((t, HEAD_DIM), lambda b, h, i: (b, N_HEADS_A + h))],
        out_specs=pl.BlockSpec((tq, HEAD_DIM), lambda b, h, i: (b * nq + i, h)),
        out_shape=jax.ShapeDtypeStruct((bsz * t, D_A), F32),
        compiler_params=_params("parallel", "parallel", "parallel"),
        name="sb_prompt",
    )(qa, kva, kva)


SB_PAGES_PER_STEP = 8
PAD_T = SUBLANES


def _sb_decode_kernel(pt_ref, q_ref, new_ref, *refs, page):
    pages = refs[:SB_PAGES_PER_STEP]
    o_ref, c_ref, acc_ref = refs[SB_PAGES_PER_STEP:]
    p = pl.program_id(1)
    rows = N_HEADS_A * PAD_T
    rec = 2 * N_HEADS_A
    tri = _tri(page)
    qh = [q_ref[h].astype(BF16) for h in range(N_HEADS_A)]

    def scores(ref):
        return jnp.concatenate(
            [_dot_nt(qh[h], ref[pl.ds(h, page, stride=rec), :].astype(BF16)) for h in range(N_HEADS_A)],
            axis=0)

    def values(ref, w):
        wb = w.astype(BF16)
        return [_dot(wb[h * PAD_T:(h + 1) * PAD_T],
                     ref[pl.ds(N_HEADS_A + h, page, stride=rec), :].astype(BF16)) for h in range(N_HEADS_A)]

    @pl.when(p == 0)
    def _():
        t_of_row = lax.broadcasted_iota(jnp.int32, (rows, page), 0) % PAD_T
        mask = lax.broadcasted_iota(jnp.int32, (rows, page), 1) < t_of_row
        ls, lr = _sb_logs(scores(new_ref), mask)
        w = _sb_weights(ls, lr, mask, jnp.zeros((rows, 1), F32), tri)
        acc_ref[...] = jnp.concatenate(values(new_ref, w), axis=0)
        c_ref[...] = jnp.sum(lr, axis=1, keepdims=True)

    @pl.when(p > 0)
    def _():
        logs = [_sb_logs(scores(pg), None) for pg in pages]
        c = c_ref[...]
        acc = None
        for pg, (ls, lr) in zip(pages, logs):
            part = values(pg, _sb_weights(ls, lr, None, c, tri))
            acc = part if acc is None else [x + y for x, y in zip(acc, part)]
            c = c + jnp.sum(lr, axis=1, keepdims=True)
        c_ref[...] = c
        acc_ref[...] += jnp.concatenate(acc, axis=0)

    @pl.when(p == pl.num_programs(1) - 1)
    def _():
        o_ref[...] = acc_ref[...]


def _sb_decode(q_heads, new_rows, cache_rows, page_table, page):
    bsz = q_heads.shape[0]
    n_pages = page_table.shape[1]
    pps = SB_PAGES_PER_STEP
    steps = n_pages // pps
    rows = N_HEADS_A * PAD_T
    blk = page * 2 * N_HEADS_A
    page_specs = [
        pl.BlockSpec((blk, HEAD_DIM),
                     lambda b, p, pt, k=k: (pt[b, n_pages - 1 - (jnp.maximum(p - 1, 0) * pps + k)], 0))
        for k in range(pps)]
    grid_spec = pltpu.PrefetchScalarGridSpec(
        num_scalar_prefetch=1,
        grid=(bsz, steps + 1),
        in_specs=[pl.BlockSpec((None, N_HEADS_A, PAD_T, HEAD_DIM), lambda b, p, pt: (b, 0, 0, 0)),
                  pl.BlockSpec((None, blk, HEAD_DIM), lambda b, p, pt: (b, 0, 0))] + page_specs,
        out_specs=pl.BlockSpec((None, rows, HEAD_DIM), lambda b, p, pt: (b, 0, 0)),
        scratch_shapes=[pltpu.VMEM((rows, 1), F32), pltpu.VMEM((rows, HEAD_DIM), F32)],
    )
    return pl.pallas_call(
        functools.partial(_sb_decode_kernel, page=page),
        grid_spec=grid_spec,
        out_shape=jax.ShapeDtypeStruct((bsz, rows, HEAD_DIM), F32),
        compiler_params=_params("parallel", "arbitrary"),
        name="sb_decode",
    )(page_table, q_heads, new_rows, *([cache_rows] * pps))


def _compress_partial(load_rows, pe_ref, w1_ref, nchunk):
    half = CMP_LEN // 2
    a = jnp.zeros((nchunk, HEAD_DIM), F32)
    b = jnp.zeros((nchunk, HEAD_DIM), F32)
    for l in range(half):
        rows = load_rows(l)
        a = a + _dot((rows + pe_ref[l:l + 1, :]).astype(BF16), w1_ref[l])
        b = b + _dot((rows + pe_ref[half + l:half + l + 1, :]).astype(BF16), w1_ref[half + l])
    return a, b


def _compress_finish(a, b, w2):
    n = a.shape[0]
    pre = a + pltpu.roll(b, n - 1, 0)
    return _dot(jax.nn.gelu(pre).astype(BF16), w2)


def _compress_prompt_kernel(r_ref, pe_ref, w1_ref, w2_ref, o_ref, *, nchunk):
    load = lambda l: r_ref[pl.ds(l, nchunk, stride=CMP_STRIDE), :]
    a, b = _compress_partial(load, pe_ref, w1_ref, nchunk)
    o_ref[...] = _compress_finish(a, b, w2_ref[...])


def _compress_prompt(nsa, pe, w1, w2, bsz, t):
    nchunk = t // CMP_STRIDE
    return pl.pallas_call(
        functools.partial(_compress_prompt_kernel, nchunk=nchunk),
        grid=(bsz, 4),
        in_specs=[pl.BlockSpec((t, HEAD_DIM), lambda b, s: (b, s)),
                  pl.BlockSpec((None, CMP_LEN, HEAD_DIM), lambda b, s: (s // 2, 0, 0)),
                  pl.BlockSpec((None, CMP_LEN, HEAD_DIM, HEAD_DIM), lambda b, s: (s // 2, 0, 0, 0)),
                  pl.BlockSpec((None, HEAD_DIM, HEAD_DIM), lambda b, s: (s // 2, 0, 0))],
        out_specs=pl.BlockSpec((None, None, nchunk, HEAD_DIM), lambda b, s: (b, s, 0, 0)),
        out_shape=jax.ShapeDtypeStruct((bsz, 4, nchunk, HEAD_DIM), F32),
        compiler_params=_params("parallel", "parallel"),
        name="compress_prompt",
    )(nsa, pe, w1, w2)


PAGES_PER_STEP = 16


NSA_REC = 4 * N_KV_B


def _compress_pages_kernel(pt_ref, pe_ref, w1_ref, *refs, chunks_per_page):
    pages = refs[:PAGES_PER_STEP]
    a_ref, b_ref = refs[PAGES_PER_STEP:]
    nchunk = PAGES_PER_STEP * chunks_per_page
    for s in range(2 * N_KV_B):
        load = lambda l, s=s: jnp.concatenate(
            [pg[pl.ds(l * NSA_REC + s, chunks_per_page, stride=CMP_STRIDE * NSA_REC), :] for pg in pages],
            axis=0)
        a, b = _compress_partial(load, pe_ref.at[s // N_KV_B], w1_ref.at[s // N_KV_B], nchunk)
        a_ref[s] = a
        b_ref[s] = b


def _compress_pages(cache_rows, page_table, pe, w1, page):
    bsz, n_pages = page_table.shape
    cpp = page // CMP_STRIDE
    nchunk = PAGES_PER_STEP * cpp
    steps = n_pages // PAGES_PER_STEP
    page_specs = [
        pl.BlockSpec((page * NSA_REC, HEAD_DIM),
                     lambda b, s, pt, k=k: (pt[b, s * PAGES_PER_STEP + k], 0))
        for k in range(PAGES_PER_STEP)]
    out_spec = pl.BlockSpec((None, 2 * N_KV_B, nchunk, HEAD_DIM), lambda b, s, pt: (b, 0, s, 0))
    grid_spec = pltpu.PrefetchScalarGridSpec(
        num_scalar_prefetch=1,
        grid=(bsz, steps),
        in_specs=[pl.BlockSpec((2, CMP_LEN, HEAD_DIM), lambda b, s, pt: (0, 0, 0)),
                  pl.BlockSpec((2, CMP_LEN, HEAD_DIM, HEAD_DIM), lambda b, s, pt: (0, 0, 0, 0))]
                 + page_specs,
        out_specs=[out_spec, out_spec],
    )
    shape = jax.ShapeDtypeStruct((bsz, 2 * N_KV_B, n_pages * cpp, HEAD_DIM), F32)
    return pl.pallas_call(
        functools.partial(_compress_pages_kernel, chunks_per_page=cpp),
        grid_spec=grid_spec,
        out_shape=[shape, shape],
        compiler_params=_params("parallel", "parallel"),
        name="compress_pages",
    )(page_table, pe, w1, *([cache_rows] * PAGES_PER_STEP))


def _cmp_branch(qs, kcmp, vcmp, pos_rows):
    n = kcmp.shape[0]
    s = _dot_nt(qs, kcmp.astype(BF16)) * SCALE
    cmp_end = lax.broadcasted_iota(jnp.int32, (1, n), 1) * CMP_STRIDE + (CMP_LEN - 1)
    mask = cmp_end <= pos_rows
    s = jnp.where(mask, s, NEG_BIG)
    m = jnp.max(s, axis=-1, keepdims=True)
    e = jnp.where(mask, jnp.exp(s - m), 0.0)
    p = e / jnp.maximum(jnp.sum(e, axis=-1, keepdims=True), 1e-30)
    return _dot(p.astype(BF16), vcmp.astype(BF16)), p


def _select_blocks(imp, pos_rows, n_sel):
    r, width = imp.shape
    lane = lax.broadcasted_iota(jnp.int32, (r, width), 1)
    qblk = pos_rows // SLC_BLOCK
    forced = (lane == 0) | (lane == qblk) | (lane == qblk - 1)
    valid = lane * SLC_BLOCK <= pos_rows
    imp = jnp.where(forced, FORCE_SCORE, jnp.where(valid, imp, -FORCE_SCORE))
    imp = jnp.where(lane < n_sel, imp, -jnp.inf)
    rank = jnp.zeros((r, width), F32)
    for i in range(n_sel):
        ci = imp[:, i:i + 1]
        better = (ci > imp) | ((ci == imp) & (lane > i))
        rank = rank + jnp.where(better, 1.0, 0.0)
    return jnp.where((rank < float(min(N_SELECT, n_sel))) & (lane < n_sel), 1.0, 0.0)


def _apply_mask(x, mask, fill):
    n = mask.shape[0]
    if x.shape[0] == n:
        return jnp.where(mask, x, fill)
    return jnp.concatenate([jnp.where(mask, x[i:i + n], fill) for i in range(0, x.shape[0], n)], axis=0)


def _softmax_step(s, mask, v, carry, zero_masked=True):
    m, l, acc = carry
    s = _apply_mask(s, mask, NEG_BIG)
    m_new = jnp.maximum(m, jnp.max(s, axis=-1, keepdims=True))
    alpha = jnp.exp(m - m_new)
    e = jnp.exp(s - m_new)
    if zero_masked:
        e = _apply_mask(e, mask, 0.0)
    l = alpha * l + jnp.sum(e, axis=-1, keepdims=True)
    acc = alpha * acc + _dot(e.astype(BF16), v)
    return m_new, l, acc


def _softmax_init(r):
    return (jnp.full((r, 1), NEG_BIG, F32), jnp.zeros((r, 1), F32), jnp.zeros((r, HEAD_DIM), F32))


def _softmax_finish(carry):
    _, l, acc = carry
    return acc / jnp.maximum(l, 1e-30)


def _select_blocks_t(imp, pos, n_sel):
    rows, t = imp.shape
    blk = lax.broadcasted_iota(jnp.int32, (rows, t), 0)
    qblk = pos // SLC_BLOCK
    forced = (blk == 0) | (blk == qblk) | (blk == qblk - 1)
    valid = blk * SLC_BLOCK <= pos
    imp = jnp.where(forced, FORCE_SCORE, jnp.where(valid, imp, -FORCE_SCORE))
    imp = jnp.where(blk < n_sel, imp, -jnp.inf)
    rank = jnp.zeros((rows, t), F32)
    for i in range(n_sel):
        ri = imp[i:i + 1, :]
        better = (ri > imp) | ((ri == imp) & (blk > i))
        rank = rank + jnp.where(better, 1.0, 0.0)
    return jnp.where((rank < float(min(N_SELECT, n_sel))) & (blk < n_sel), 1.0, 0.0)


def _nsa_prompt_kernel(q_ref, g_ref, kc_ref, vc_ref, ks_ref, vs_ref, kw_ref, vw_ref,
                       mt_ref, e_ref, o_ref, *, tq, n_sel):
    i = pl.program_id(2)
    r = HPG * tq
    qs = jnp.concatenate([q_ref[:, h * HEAD_DIM:(h + 1) * HEAD_DIM] for h in range(HPG)],
                         axis=0).astype(BF16)
    pos_t = i * tq + lax.broadcasted_iota(jnp.int32, (tq, 1), 0)
    pos_r = jnp.concatenate([pos_t] * HPG, axis=0)

    o_c, p = _cmp_branch(qs, kc_ref[...], vc_ref[...], pos_r)
    psum = p[0:tq]
    for h in range(1, HPG):
        psum = psum + p[h * tq:(h + 1) * tq]
    hi = psum.astype(BF16)
    lo = (psum - hi.astype(F32)).astype(BF16)
    imp_t = _dot_nt(mt_ref[...], hi) + _dot_nt(mt_ref[...], lo)
    sel_rows = -(-n_sel // 8) * 8
    pos_lane = i * tq + lax.broadcasted_iota(jnp.int32, (1, tq), 1)
    sel_t = _select_blocks_t(imp_t[0:sel_rows], pos_lane, n_sel)
    sel = jnp.concatenate([sel_t, jnp.zeros((imp_t.shape[0] - sel_rows, tq), F32)], axis=0).T.astype(BF16)

    lane = lax.broadcasted_iota(jnp.int32, (tq, tq), 1)

    def slc_chunk(c, carry, diagonal):
        off = pl.multiple_of(c * tq, tq)
        k = ks_ref[pl.ds(off, tq), :].astype(BF16)
        v = vs_ref[pl.ds(off, tq), :].astype(BF16)
        hit = _dot(sel, e_ref[:, pl.ds(off, tq)])
        if diagonal:
            hit = jnp.where(off + lane <= pos_t, hit, 0.0)
        return _softmax_step(_dot_nt(qs, k) * SCALE, hit > 0.5, v, carry, zero_masked=False)

    carry = lax.fori_loop(0, i, lambda c, cr: slc_chunk(c, cr, False), _softmax_init(r))
    o_s = _softmax_finish(slc_chunk(i, carry, True))

    carry = _softmax_init(r)
    for d in range(-(-WINDOW // tq), -1, -1):
        c = i - d
        off = pl.multiple_of(jnp.maximum(c, 0) * tq, tq)
        k = kw_ref[pl.ds(off, tq), :].astype(BF16)
        v = vw_ref[pl.ds(off, tq), :].astype(BF16)
        kpos = c * tq + lane
        dlt = pos_t - kpos
        mask = (dlt >= 0) & (dlt < WINDOW) & (kpos >= 0)
        carry = _softmax_step(_dot_nt(qs, k) * SCALE, mask, v, carry, zero_masked=False)
    o_w = _softmax_finish(carry)

    for h in range(HPG):
        rows = slice(h * tq, (h + 1) * tq)
        gc = g_ref[:, 3 * h:3 * h + 1]
        gs = g_ref[:, 3 * h + 1:3 * h + 2]
        gw = g_ref[:, 3 * h + 2:3 * h + 3]
        o_ref[:, h * HEAD_DIM:(h + 1) * HEAD_DIM] = gc * o_c[rows] + gs * o_s[rows] + gw * o_w[rows]


def _cmp_to_slc(n_cmp_pad, n_cmp, n_sel, width):
    i = np.arange(n_cmp_pad)[:, None]
    j = np.arange(width)[None, :]
    lo = np.maximum(i * CMP_STRIDE, j * SLC_BLOCK)
    hi = np.minimum(i * CMP_STRIDE + CMP_LEN, (j + 1) * SLC_BLOCK)
    m = np.maximum(hi - lo, 0) / CMP_STRIDE
    m = np.where((i < n_cmp) & (j < n_sel), m, 0.0)
    return jnp.asarray(m, dtype=BF16)


def _nsa_prompt(qb, gates, cmp_kv, nsa, win, bsz, t):
    tq = min(512, t)
    nq = t // tq
    n_cmp = (t - CMP_LEN) // CMP_STRIDE + 1
    n_cmp_pad = cmp_kv.shape[2]
    n_sel = -(-t // SLC_BLOCK)
    assert n_sel <= LANES
    m_mat = _cmp_to_slc(n_cmp_pad, n_cmp, n_sel, LANES).T
    e_mat = jnp.asarray(np.arange(LANES)[:, None] == (np.arange(t) // SLC_BLOCK)[None, :], dtype=BF16)
    full = lambda shape: pl.BlockSpec(shape, lambda b, g, i: (0,) * len(shape))
    rows = lambda col: pl.BlockSpec((t, HEAD_DIM), lambda b, g, i, col=col: (b, col + g))
    return pl.pallas_call(
        functools.partial(_nsa_prompt_kernel, tq=tq, n_sel=n_sel),
        grid=(bsz, N_KV_B, nq),
        in_specs=[pl.BlockSpec((tq, HPG * HEAD_DIM), lambda b, g, i: (b * nq + i, g)),
                  pl.BlockSpec((tq, LANES), lambda b, g, i: (b * nq + i, g)),
                  pl.BlockSpec((None, None, n_cmp_pad, HEAD_DIM), lambda b, g, i: (b, g, 0, 0)),
                  pl.BlockSpec((None, None, n_cmp_pad, HEAD_DIM), lambda b, g, i: (b, 2 + g, 0, 0)),
                  rows(4), rows(6), rows(0), rows(2),
                  full(m_mat.shape), full(e_mat.shape)],
        out_specs=pl.BlockSpec((tq, HPG * HEAD_DIM), lambda b, g, i: (b * nq + i, g)),
        out_shape=jax.ShapeDtypeStruct((bsz * t, D_B), F32),
        compiler_params=_params("parallel", "parallel", "parallel"),
        name="nsa_prompt",
    )(qb, gates, cmp_kv, cmp_kv, nsa, nsa, win, win, m_mat, e_mat)


def _nsa_select_kernel(q_ref, ak_ref, av_ref, bk_ref, bv_ref, w2_ref, m_ref, oc_ref, sel_ref,
                       *, n_new, past, n_sel):
    qs = jnp.concatenate([q_ref[:, h * HEAD_DIM:(h + 1) * HEAD_DIM] for h in range(HPG)],
                         axis=0).astype(BF16)
    pos_t = past + lax.broadcasted_iota(jnp.int32, (n_new, 1), 0)
    pos_r = jnp.concatenate([pos_t] * HPG, axis=0)
    kcmp = _compress_finish(ak_ref[...], bk_ref[...], w2_ref[0])
    vcmp = _compress_finish(av_ref[...], bv_ref[...], w2_ref[1])
    o_c, p = _cmp_branch(qs, kcmp, vcmp, pos_r)
    oc_ref[...] = o_c
    psum = p[0:n_new]
    for h in range(1, HPG):
        psum = psum + p[h * n_new:(h + 1) * n_new]
    sel = _select_blocks(_dot_split(psum, m_ref[...]), pos_t, n_sel)
    sel_ref[...] = jnp.concatenate([sel] * HPG, axis=0)


def _nsa_select(qb, part_a, part_b, w2, past):
    bsz, n_new, _ = qb.shape
    n_cmp_pad = part_a.shape[2]
    total = past + n_new
    n_cmp = (total - CMP_LEN) // CMP_STRIDE + 1
    n_sel = -(-total // SLC_BLOCK)
    width = -(-n_sel // LANES) * LANES
    m_mat = _cmp_to_slc(n_cmp_pad, n_cmp, n_sel, width)
    assert n_cmp == n_cmp_pad - 1, "compressed tokens must come from the paged rows only"
    r = HPG * n_new
    k_spec = pl.BlockSpec((None, None, n_cmp_pad, HEAD_DIM), lambda b, g: (b, g, 0, 0))
    v_spec = pl.BlockSpec((None, None, n_cmp_pad, HEAD_DIM), lambda b, g: (b, N_KV_B + g, 0, 0))
    return pl.pallas_call(
        functools.partial(_nsa_select_kernel, n_new=n_new, past=past, n_sel=n_sel),
        grid=(bsz, N_KV_B),
        in_specs=[pl.BlockSpec((None, n_new, HPG * HEAD_DIM), lambda b, g: (b, 0, g)),
                  k_spec, v_spec, k_spec, v_spec,
                  pl.BlockSpec((2, HEAD_DIM, HEAD_DIM), lambda b, g: (0, 0, 0)),
                  pl.BlockSpec(m_mat.shape, lambda b, g: (0, 0))],
        out_specs=[pl.BlockSpec((None, None, r, HEAD_DIM), lambda b, g: (b, g, 0, 0)),
                   pl.BlockSpec((None, None, r, width), lambda b, g: (b, g, 0, 0))],
        out_shape=[jax.ShapeDtypeStruct((bsz, N_KV_B, r, HEAD_DIM), F32),
                   jax.ShapeDtypeStruct((bsz, N_KV_B, r, width), F32)],
        compiler_params=_params("parallel", "parallel"),
        name="nsa_select",
    )(qb, part_a, part_a, part_b, part_b, w2, m_mat)


SLC_PAGES_PER_STEP = 16


def _nsa_slc_decode_kernel(pt_ref, q_ref, sel_ref, new_ref, *refs, n_new, past, page):
    pps = SLC_PAGES_PER_STEP
    pages = refs[:pps]
    o_ref, m_ref, l_ref, acc_ref = refs[pps:]
    p = pl.program_id(1)
    n_steps = pl.num_programs(1) - 1
    r = HPG * n_new
    rows = N_KV_B * r
    t_of_row = lax.broadcasted_iota(jnp.int32, (rows, 1), 0) % n_new
    lane = lax.broadcasted_iota(jnp.int32, (rows, page), 1)
    bpp = page // SLC_BLOCK
    qg = [q_ref[g].astype(BF16) for g in range(N_KV_B)]
    sel = jnp.concatenate([sel_ref[g] for g in range(N_KV_B)], axis=0)
    blk_lane = lax.broadcasted_iota(jnp.int32, sel.shape, 1)

    @pl.when(p == 0)
    def _():
        m_ref[...] = jnp.full(m_ref.shape, NEG_BIG, F32)
        l_ref[...] = jnp.zeros(l_ref.shape, F32)
        acc_ref[...] = jnp.zeros(acc_ref.shape, F32)

    def masked_scores(ref, first_pos):
        first_blk = first_pos // SLC_BLOCK
        hit = jnp.zeros((rows, page), F32)
        for j in range(bpp):
            flag = jnp.sum(jnp.where(blk_lane == first_blk + j, sel, 0.0), axis=-1, keepdims=True)
            hit = jnp.where(lane // SLC_BLOCK == j, flag, hit)
        mask = (hit > 0.5) & (first_pos + lane <= past + t_of_row)
        s = jnp.concatenate(
            [_dot_nt(qg[g], ref[pl.ds(2 * N_KV_B + g, page, stride=NSA_REC), :].astype(BF16))
             for g in range(N_KV_B)], axis=0) * SCALE
        return jnp.where(mask, s, NEG_BIG), mask

    def visit(blocks):
        scored = [masked_scores(ref, pos) for ref, pos in blocks]
        m_old = m_ref[...]
        m_new = m_old
        for s, _ in scored:
            m_new = jnp.maximum(m_new, jnp.max(s, axis=-1, keepdims=True))
        alpha = jnp.exp(m_old - m_new)
        l = alpha * l_ref[...]
        acc = alpha * acc_ref[...]
        for (ref, _), (s, mask) in zip(blocks, scored):
            e = jnp.where(mask, jnp.exp(s - m_new), 0.0)
            l = l + jnp.sum(e, axis=-1, keepdims=True)
            eb = e.astype(BF16)
            acc = acc + jnp.concatenate(
                [_dot(eb[g * r:(g + 1) * r], ref[pl.ds(3 * N_KV_B + g, page, stride=NSA_REC), :].astype(BF16))
                 for g in range(N_KV_B)], axis=0)
        m_ref[...] = m_new
        l_ref[...] = l
        acc_ref[...] = acc

    @pl.when(p < n_steps)
    def _():
        visit([(pg, (p * pps + k) * page) for k, pg in enumerate(pages)])

    @pl.when(p == n_steps)
    def _():
        visit([(new_ref, past)])
        o_ref[...] = acc_ref[...] / jnp.maximum(l_ref[...], 1e-30)


def _nsa_slc_decode(q_groups, sel, new_rows, cache_rows, page_table, past, page, n_new):
    bsz = q_groups.shape[0]
    n_pages = page_table.shape[1]
    pps = SLC_PAGES_PER_STEP
    steps = n_pages // pps
    r = HPG * n_new
    rows = N_KV_B * r
    width = sel.shape[-1]
    blk = page * NSA_REC
    page_specs = [
        pl.BlockSpec((blk, HEAD_DIM),
                     lambda b, p, pt, k=k: (pt[b, jnp.minimum(p, steps - 1) * pps + k], 0))
        for k in range(pps)]
    grid_spec = pltpu.PrefetchScalarGridSpec(
        num_scalar_prefetch=1,
        grid=(bsz, steps + 1),
        in_specs=[pl.BlockSpec((None, N_KV_B, r, HEAD_DIM), lambda b, p, pt: (b, 0, 0, 0)),
                  pl.BlockSpec((None, N_KV_B, r, width), lambda b, p, pt: (b, 0, 0, 0)),
                  pl.BlockSpec((None, blk, HEAD_DIM), lambda b, p, pt: (b, 0, 0))] + page_specs,
        out_specs=pl.BlockSpec((None, rows, HEAD_DIM), lambda b, p, pt: (b, 0, 0)),
        scratch_shapes=[pltpu.VMEM((rows, 1), F32), pltpu.VMEM((rows, 1), F32),
                        pltpu.VMEM((rows, HEAD_DIM), F32)],
    )
    return pl.pallas_call(
        functools.partial(_nsa_slc_decode_kernel, n_new=n_new, past=past, page=page),
        grid_spec=grid_spec,
        out_shape=jax.ShapeDtypeStruct((bsz, rows, HEAD_DIM), F32),
        compiler_params=_params("parallel", "arbitrary"),
        name="nsa_slc_decode",
    )(page_table, q_groups, sel, new_rows, *([cache_rows] * pps))


def _nsa_merge_decode_kernel(q_ref, g_ref, oc_ref, os_ref, st_ref, new_ref, o_ref, nw_ref,
                             *, n_new, past, n_buf):
    r = HPG * n_new
    n_pad = new_ref.shape[0]
    t_of_row = lax.broadcasted_iota(jnp.int32, (r, 1), 0) % n_new
    pos_r = past + t_of_row
    buf_pos = past - n_buf + lax.broadcasted_iota(jnp.int32, (r, n_buf), 1)
    new_pos = past + lax.broadcasted_iota(jnp.int32, (r, n_pad), 1)

    def in_window(kpos):
        dlt = pos_r - kpos
        return (dlt >= 0) & (dlt < WINDOW) & (kpos >= 0)

    for g in range(N_KV_B):
        qs = jnp.concatenate(
            [q_ref[:, (g * HPG + h) * HEAD_DIM:(g * HPG + h + 1) * HEAD_DIM] for h in range(HPG)],
            axis=0).astype(BF16)
        kcol = slice(g * HEAD_DIM, (g + 1) * HEAD_DIM)
        vcol = slice((N_KV_B + g) * HEAD_DIM, (N_KV_B + g + 1) * HEAD_DIM)
        carry = _softmax_init(r)
        carry = _softmax_step(_dot_nt(qs, st_ref[:, kcol].astype(BF16)) * SCALE, in_window(buf_pos),
                              st_ref[:, vcol].astype(BF16), carry)
        carry = _softmax_step(_dot_nt(qs, new_ref[:, kcol].astype(BF16)) * SCALE, in_window(new_pos),
                              new_ref[:, vcol].astype(BF16), carry)
        o_w = _softmax_finish(carry)
        for h in range(HPG):
            rows = slice(h * n_new, (h + 1) * n_new)
            c0 = g * LANES + 3 * h
            out = (g_ref[:, c0:c0 + 1] * oc_ref[g, rows] + g_ref[:, c0 + 1:c0 + 2] * os_ref[g, rows]
                   + g_ref[:, c0 + 2:c0 + 3] * o_w[rows])
            o_ref[:, (g * HPG + h) * HEAD_DIM:(g * HPG + h + 1) * HEAD_DIM] = out
    shifted = pltpu.roll(st_ref[...], n_buf - n_new, 0)
    placed = pltpu.roll(new_ref[...], n_pad - n_new, 0)
    tail_row = lax.broadcasted_iota(jnp.int32, (n_pad, placed.shape[1]), 0)
    nw_ref[0:n_buf - n_pad, :] = shifted[0:n_buf - n_pad]
    nw_ref[n_buf - n_pad:n_buf, :] = jnp.where(tail_row >= n_pad - n_new, placed, shifted[n_buf - n_pad:])


def _nsa_merge_decode(qb, gates, o_cmp, o_slc, state_win, win_new, past):
    bsz, n_new, _ = qb.shape
    n_buf = state_win.shape[1]
    n_pad = win_new.shape[1]
    keep = min(WINDOW, past + n_new)
    assert keep == n_buf and n_buf > n_pad >= n_new
    r = HPG * n_new
    cols = 2 * D_KV_B
    per_b = lambda shape: pl.BlockSpec((None,) + shape, lambda b: (b,) + (0,) * len(shape))
    return pl.pallas_call(
        functools.partial(_nsa_merge_decode_kernel, n_new=n_new, past=past, n_buf=n_buf),
        grid=(bsz,),
        in_specs=[per_b((n_new, D_B)), per_b((n_new, 2 * LANES)),
                  per_b((N_KV_B, r, HEAD_DIM)), per_b((N_KV_B, r, HEAD_DIM)),
                  per_b((n_buf, cols)), per_b((n_pad, cols))],
        out_specs=[per_b((n_new, D_B)), per_b((keep, cols))],
        out_shape=[jax.ShapeDtypeStruct((bsz, n_new, D_B), F32),
                   jax.ShapeDtypeStruct((bsz, keep, cols), F32)],
        compiler_params=_params("parallel"),
        name="nsa_merge_decode",
    )(qb, gates, o_cmp, o_slc, state_win, win_new)


def _outproj_kernel(x_ref, oa_ref, ob_ref, wa_ref, wb_ref, g_ref, b_ref, h_ref):
    y = _dot(oa_ref[...].astype(BF16), wa_ref[...]) + _dot(ob_ref[...].astype(BF16), wb_ref[...])
    h_ref[...] = _layer_norm(ALPHA * x_ref[...] + y, g_ref[...], b_ref[...])


def _outproj(x, o_a, o_b, wa, wb, g, b):
    m, d = x.shape
    tm = min(512, m)
    row = lambda w: pl.BlockSpec((tm, w), lambda i: (i, 0))
    full = lambda a: pl.BlockSpec(a.shape, lambda i: (0, 0))
    return pl.pallas_call(
        _outproj_kernel,
        grid=(m // tm,),
        in_specs=[row(d), row(D_A), row(D_B), full(wa), full(wb), full(g), full(b)],
        out_specs=row(d),
        out_shape=jax.ShapeDtypeStruct((m, d), F32),
        compiler_params=_params("parallel"),
        name="outproj_ln",
    )(x, o_a, o_b, wa, wb, g, b)


def _peer_query_kernel(h_ref, w_ref, q_ref):
    q_ref[...] = _dot(h_ref[...].astype(BF16), w_ref[...])


def _peer_query(h, wq):
    m, d = h.shape
    n = wq.shape[1]
    tm = min(512, m)
    return pl.pallas_call(
        _peer_query_kernel,
        grid=(m // tm,),
        in_specs=[pl.BlockSpec((tm, d), lambda i: (i, 0)), pl.BlockSpec((d, n), lambda i: (0, 0))],
        out_specs=pl.BlockSpec((tm, n), lambda i: (i, 0)),
        out_shape=jax.ShapeDtypeStruct((m, n), F32),
        compiler_params=_params("parallel"),
        name="peer_query",
    )(h, wq)


def _compare_exchange(v, i, j, descending):
    hi, lo = jnp.maximum(v[i], v[j]), jnp.minimum(v[i], v[j])
    v[i], v[j] = (hi, lo) if descending else (lo, hi)


def _bitonic_merge_desc(v):
    j = len(v) // 2
    while j >= 1:
        for i in range(len(v)):
            if i ^ j > i:
                _compare_exchange(v, i, i ^ j, True)
        j //= 2


def _bitonic_sort_desc(v):
    n = len(v)
    k = 2
    while k <= n:
        j = k // 2
        while j >= 1:
            for i in range(n):
                if i ^ j > i:
                    _compare_exchange(v, i, i ^ j, (i & k) == 0)
            j //= 2
        k *= 2


def _top_sorted(x, k):
    v = [x[i:i + SUBLANES] for i in range(0, x.shape[0], SUBLANES)]
    _bitonic_sort_desc(v)
    d = 1
    while d < SUBLANES:
        other = [pltpu.roll(a, d, 0) for a in v]
        if 2 * len(v) <= k:
            v = v + other[::-1]
        else:
            v = [jnp.maximum(v[i], other[k - 1 - i]) for i in range(k)]
        _bitonic_merge_desc(v)
        d *= 2
    return [a[0:1] for a in v]


_PEER_PAIRS = [(i, j) for i in range(PEER_TOPK) for j in range(PEER_TOPK) if (i + 1) * (j + 1) <= PEER_TOPK]
_PEER_CAND_ROWS = SUBLANES * (1 << (-(-len(_PEER_PAIRS) // SUBLANES) - 1).bit_length())


def _peer_route_kernel(q_ref, sk_ref, thr_ref, s2_ref, e1_ref, e2_ref, cand_ref):
    half = sk_ref.shape[2]
    cand_ref[...] = jnp.full(cand_ref.shape, -jnp.inf, F32)
    for h in range(PEER_HEADS):
        s, tops = [], []
        for c in range(2):
            col = (2 * h + c) * half
            st = _dot_nt(sk_ref[c], q_ref[:, col:col + half].astype(BF16))
            s.append(st)
            tops.append(_top_sorted(st, PEER_TOPK))
        for n, (i, j) in enumerate(_PEER_PAIRS):
            cand_ref[n:n + 1, :] = tops[0][i] + tops[1][j]
        best = _top_sorted(cand_ref[...], PEER_TOPK)
        top = best[0]
        z = jnp.ones_like(top)
        for v in best[1:]:
            z = z + jnp.exp(v - top)
        tau = best[PEER_TOPK - 1]
        thr = jnp.full(s[0].shape, jnp.inf, F32)
        for i in reversed(range(PEER_TOPK)):
            thr_i = jnp.full_like(tau, jnp.inf)
            for j in range(PEER_TOPK):
                if (i + 1) * (j + 1) <= PEER_TOPK:
                    thr_i = jnp.where(tops[0][i] + tops[1][j] >= tau, tops[1][j], thr_i)
            thr = jnp.where(s[0] == tops[0][i], thr_i, thr)
        thr_ref[h] = thr
        s2_ref[h] = s[1]
        e1_ref[h] = jnp.exp(s[0] - tops[0][0])
        e2_ref[h] = jnp.exp(s[1] - tops[1][0]) / z


def _peer_route(q, sub_keys):
    n = q.shape[0]
    tm = 128
    nk = sub_keys.shape[1]
    big = jax.ShapeDtypeStruct((PEER_HEADS, nk, n), F32)
    big_spec = pl.BlockSpec((PEER_HEADS, nk, tm), lambda i: (0, 0, i))
    return pl.pallas_call(
        _peer_route_kernel,
        grid=(n // tm,),
        in_specs=[pl.BlockSpec((tm, q.shape[1]), lambda i: (i, 0)),
                  pl.BlockSpec(sub_keys.shape, lambda i: (0, 0, 0))],
        out_specs=[big_spec] * 4,
        out_shape=[big] * 4,
        scratch_shapes=[pltpu.VMEM((_PEER_CAND_ROWS, tm), F32)],
        compiler_params=_params("parallel"),
        name="peer_route",
    )(q, sub_keys)


def _peer_dense_kernel(h_ref, u_ref, v_ref, thr_ref, s2_ref, e1_ref, e2_ref, g_ref, b_ref,
                       o_ref, hb_ref, wt_ref, *, tm, ac, sub_ac):
    c = pl.program_id(1)
    nk = s2_ref.shape[1]
    kb_rows = nk // 4

    @pl.when(c == 0)
    def _():
        o_ref[...] = jnp.zeros(o_ref.shape, F32)
        hb_ref[...] = h_ref[...].astype(BF16)

    for sub in range(ac // sub_ac):
        ex = slice(sub * sub_ac * nk, (sub + 1) * sub_ac * nk)
        act = jax.nn.gelu(_dot_nt(u_ref[ex, :], hb_ref[...]))
        for ts in range(tm // LANES):
            tok = slice(ts * LANES, (ts + 1) * LANES)
            for kb in range(nk // kb_rows):
                keys = slice(kb * kb_rows, (kb + 1) * kb_rows)
                gates = [jnp.zeros((kb_rows, LANES), F32) for _ in range(sub_ac)]
                for h in range(PEER_HEADS):
                    s2 = s2_ref[h, keys, tok]
                    e2 = e2_ref[h, keys, tok]
                    for k in range(sub_ac):
                        aa = sub * sub_ac + k
                        chosen = s2 >= thr_ref[h, aa:aa + 1, tok]
                        gates[k] = gates[k] + jnp.where(chosen, e1_ref[h, aa:aa + 1, tok] * e2, 0.0)
                for k in range(sub_ac):
                    rows = slice(k * nk + kb * kb_rows, k * nk + (kb + 1) * kb_rows)
                    wt_ref[sub * sub_ac * nk + rows.start:sub * sub_ac * nk + rows.stop, tok] = (
                        gates[k] * act[rows, tok]).astype(BF16)
    o_ref[...] += _dot_tn(wt_ref[...], v_ref[...])

    @pl.when(c == pl.num_programs(1) - 1)
    def _():
        o_ref[...] = _layer_norm(ALPHA * h_ref[...] + o_ref[...], g_ref[...], b_ref[...])


def _peer_dense(h, u, v, route, g, b):
    n, d = h.shape
    thr, s2, e1, e2 = route
    nk = s2.shape[1]
    tm = min(512, n)
    ac = 8
    big_spec = pl.BlockSpec((PEER_HEADS, nk, tm), lambda i, c: (0, 0, i))
    row_spec = pl.BlockSpec((PEER_HEADS, ac, tm), lambda i, c: (0, c, i))
    tab_spec = pl.BlockSpec((ac * nk, d), lambda i, c: (c, 0))
    full = lambda a: pl.BlockSpec(a.shape, lambda i, c: (0, 0))
    return pl.pallas_call(
        functools.partial(_peer_dense_kernel, tm=tm, ac=ac, sub_ac=4),
        grid=(n // tm, nk // ac),
        in_specs=[pl.BlockSpec((tm, d), lambda i, c: (i, 0)), tab_spec, tab_spec,
                  row_spec, big_spec, row_spec, big_spec, full(g), full(b)],
        out_specs=pl.BlockSpec((tm, d), lambda i, c: (i, 0)),
        out_shape=jax.ShapeDtypeStruct((n, d), F32),
        scratch_shapes=[pltpu.VMEM((tm, d), BF16), pltpu.VMEM((ac * nk, tm), BF16)],
        compiler_params=_params("parallel", "arbitrary"),
        name="peer_dense",
    )(h, u, v, thr, s2, e1, e2, g, b)


def _post_block(x, o_a, o_b, wts):
    h = _outproj(x, o_a, o_b, wts["wo_a"], wts["wo_b"], wts["ln1_g"], wts["ln1_b"])
    q = _peer_query(h, wts["wq"])
    route = _peer_route(q, wts["sub_keys"])
    return _peer_dense(h, wts["u"], wts["v"], route, wts["ln2_g"], wts["ln2_b"])


def _pad_rows(a, rows):
    return jnp.pad(a, ((0, rows - a.shape[0]),) + ((0, 0),) * (a.ndim - 1))


def kernel(x_prompt, x_sample, cache_a, cache_nsa, state_win, page_table, w_in, cmp_pe_k, cmp_w1_k,
           cmp_w2_k, cmp_pe_v, cmp_w1_v, cmp_w2_v, w_out, ln1_g, ln1_b, peer_w_query, peer_sub_keys,
           peer_u, peer_v, ln2_g, ln2_b):
    assert w_in.shape[0] == DEPTH == 1
    bsz, t, d = x_prompt.shape
    dbsz, n_new, _ = x_sample.shape
    n_pool, page = cache_a.shape[1], cache_a.shape[2]
    past = page_table.shape[1] * page

    w = w_in[0]
    o = np.cumsum((0, D_A, 2 * D_A, D_B, 4 * D_KV_B, 2 * D_KV_B, 3 * N_HEADS_B))
    wg = w[:, o[5]:o[6]]
    per_g = 3 * HPG
    gate_w = jnp.concatenate(
        [jnp.pad(wg[:, g * per_g:(g + 1) * per_g], ((0, 0), (0, LANES - per_g))) for g in range(N_KV_B)],
        axis=1)
    wparts = {"qa": w[:, o[0]:o[1]], "kva": w[:, o[1]:o[2]], "qb": w[:, o[2]:o[3]],
              "nsa": w[:, o[3]:o[4]], "win": w[:, o[4]:o[5]], "gates": gate_w}
    wparts = {k: v.astype(BF16) for k, v in wparts.items()}
    row2 = lambda a: a[0].reshape(1, -1)
    wts = {"wo_a": w_out[0, :D_A].astype(BF16), "wo_b": w_out[0, D_A:].astype(BF16),
           "ln1_g": row2(ln1_g), "ln1_b": row2(ln1_b), "ln2_g": row2(ln2_g), "ln2_b": row2(ln2_b),
           "wq": peer_w_query[0].astype(BF16), "sub_keys": peer_sub_keys[0].astype(BF16),
           "u": peer_u[0].astype(BF16), "v": peer_v[0].astype(BF16)}
    pe = jnp.stack([cmp_pe_k[0], cmp_pe_v[0]])
    w1 = jnp.stack([cmp_w1_k[0], cmp_w1_v[0]]).astype(BF16)
    w2 = jnp.stack([cmp_w2_k[0], cmp_w2_v[0]]).astype(BF16)

    xp = x_prompt.reshape(bsz * t, d)
    tabs_p = _rope_tables(jnp.arange(t, dtype=jnp.int32))
    qa, kva, qb, nsa, win, gates = _project_all(xp.astype(BF16), wparts, tabs_p)
    o_a = _sb_prompt(qa, kva, bsz, t)
    cmp_kv = _compress_prompt(nsa, pe, w1, w2, bsz, t)
    o_b = _nsa_prompt(qb, gates, cmp_kv, nsa, win, bsz, t)
    y_prompt = _post_block(xp, o_a, o_b, wts).reshape(bsz, t, d)
    new_a_p = kva.reshape(1, bsz, t, 2, N_HEADS_A, HEAD_DIM)
    new_nsa_p = nsa.reshape(1, bsz, t, 4, N_KV_B, HEAD_DIM)
    keep_p = min(WINDOW, t)
    new_win_p = win.reshape(bsz, t, 2, N_KV_B, HEAD_DIM)[None, :, t - keep_p:]

    ns = dbsz * n_new
    xs = x_sample.reshape(ns, d)
    pos_s = past + jnp.arange(n_new, dtype=jnp.int32)
    tabs_s = tuple(jnp.tile(tb, (dbsz, 1)) for tb in _rope_tables(pos_s))
    qa_s, kva_s, qb_s, nsa_s, win_s, gates_s = _project_all(xs.astype(BF16), wparts, tabs_s)
    by_b = lambda a: a.reshape(dbsz, n_new, a.shape[-1])
    pad_page = lambda a: jnp.pad(by_b(a), ((0, 0), (0, page - n_new), (0, 0)))
    as_records = lambda a, rec: jnp.pad(a.reshape(dbsz, n_new * rec, HEAD_DIM),
                                        ((0, 0), (0, (page - n_new) * rec), (0, 0)))
    q_heads = jnp.pad(qa_s.reshape(dbsz, n_new, N_HEADS_A, HEAD_DIM).transpose(0, 2, 1, 3),
                      ((0, 0), (0, 0), (0, PAD_T - n_new), (0, 0)))
    o_a_h = _sb_decode(q_heads, as_records(kva_s, 2 * N_HEADS_A),
                       cache_a.reshape(n_pool * page * 2 * N_HEADS_A, HEAD_DIM), page_table, page)
    o_a_s = o_a_h.reshape(dbsz, N_HEADS_A, PAD_T, HEAD_DIM)[:, :, :n_new].transpose(0, 2, 1, 3)
    cache_n = cache_nsa.reshape(n_pool * page * NSA_REC, HEAD_DIM)
    part_a, part_b = _compress_pages(cache_n, page_table, pe, w1, page)
    o_cmp, sel = _nsa_select(by_b(qb_s), part_a, part_b, w2, past)
    q_groups = qb_s.reshape(dbsz, n_new, N_KV_B, HPG, HEAD_DIM).transpose(0, 2, 3, 1, 4).reshape(
        dbsz, N_KV_B, HPG * n_new, HEAD_DIM)
    o_slc = _nsa_slc_decode(q_groups, sel, as_records(nsa_s, NSA_REC), cache_n, page_table, past, page,
                            n_new).reshape(dbsz, N_KV_B, HPG * n_new, HEAD_DIM)
    n_buf = state_win.shape[2]
    o_b_s, new_win = _nsa_merge_decode(by_b(qb_s), by_b(gates_s), o_cmp, o_slc,
                                       state_win[0].reshape(dbsz, n_buf, 2 * D_KV_B), pad_page(win_s), past)
    rows_s = -(-ns // LANES) * LANES
    y_s = _post_block(_pad_rows(xs, rows_s), _pad_rows(o_a_s.reshape(ns, D_A), rows_s),
                      _pad_rows(o_b_s.reshape(ns, D_B), rows_s), wts)
    y_sample = y_s[:ns].reshape(dbsz, n_new, d)
    new_a_s = kva_s.reshape(1, dbsz, n_new, 2, N_HEADS_A, HEAD_DIM)
    new_nsa_s = nsa_s.reshape(1, dbsz, n_new, 4, N_KV_B, HEAD_DIM)
    new_win_s = new_win.reshape(1, dbsz, new_win.shape[1], 2, N_KV_B, HEAD_DIM)
    return (y_prompt, y_sample, new_a_p, new_nsa_p, new_win_p, new_a_s, new_nsa_s, new_win_s)
```

```python
import functools
import math

import numpy as np
import jax
import jax.numpy as jnp
from jax import lax
from jax.experimental import pallas as pl
from jax.experimental.pallas import tpu as pltpu

F32 = jnp.float32
BF16 = jnp.bfloat16

LANES = 128
SUBLANES = 8
VMEM_LIMIT = 56 * 1024 * 1024

HEAD_DIM = 128
N_HEADS_A = 8
N_HEADS_B = 8
N_KV_B = 2
HPG = N_HEADS_B // N_KV_B
D_A = N_HEADS_A * HEAD_DIM
D_B = N_HEADS_B * HEAD_DIM
D_KV_B = N_KV_B * HEAD_DIM
ROPE_DIM = HEAD_DIM // 4
ROPE_THETA = 500000.0
CMP_LEN = 32
CMP_STRIDE = 16
SLC_BLOCK = 64
N_SELECT = 16
WINDOW = 512
FORCE_SCORE = 1.0e4
PEER_HEADS = 8
PEER_NKEYS = 128
PEER_TOPK = 16
DEPTH = 1
ALPHA = (2 * DEPTH) ** 0.25
LN_EPS = 1e-5
NEG_BIG = -1e30
SCALE = 1.0 / math.sqrt(HEAD_DIM)


def _dot(a, b):
    return jnp.dot(a, b, preferred_element_type=F32)


def _dot_nt(a, b):
    return lax.dot_general(a, b, (((1,), (1,)), ((), ())), preferred_element_type=F32)


def _dot_tn(a, b):
    return lax.dot_general(a, b, (((0,), (0,)), ((), ())), preferred_element_type=F32)


def _dot_split(x, w):
    hi = x.astype(BF16)
    lo = (x - hi.astype(F32)).astype(BF16)
    return _dot(hi, w) + _dot(lo, w)


def _params(*sem):
    return pltpu.CompilerParams(dimension_semantics=sem, vmem_limit_bytes=VMEM_LIMIT)


def _layer_norm(r, g, b):
    mu = jnp.mean(r, axis=-1, keepdims=True)
    d = r - mu
    var = jnp.mean(d * d, axis=-1, keepdims=True)
    return d * lax.rsqrt(var + LN_EPS) * g + b


def _inproj_kernel(x_ref, w_ref, c_ref, sa_ref, sb_ref, o_ref, *, rope_flags, sigmoid):
    acc = _dot(x_ref[...], w_ref[...])
    for j, flag in enumerate(rope_flags):
        blk = acc[:, j * LANES:(j + 1) * LANES]
        if flag:
            blk = (blk * c_ref[...]
                   + pltpu.roll(blk, LANES - ROPE_DIM // 2, 1) * sa_ref[...]
                   + pltpu.roll(blk, ROPE_DIM // 2, 1) * sb_ref[...])
        if sigmoid:
            blk = jax.nn.sigmoid(blk)
        o_ref[:, j * LANES:(j + 1) * LANES] = blk


def _inproj(xb, w, tabs, rope_flags, sigmoid=False):
    m, k = xb.shape
    n = w.shape[1]
    c, sa, sb = tabs
    tm = min(1024, m)
    nt = c.shape[0] // tm
    tab_spec = pl.BlockSpec((tm, LANES), lambda i: (i % nt, 0))
    return pl.pallas_call(
        functools.partial(_inproj_kernel, rope_flags=tuple(rope_flags), sigmoid=sigmoid),
        grid=(m // tm,),
        in_specs=[pl.BlockSpec((tm, k), lambda i: (i, 0)),
                  pl.BlockSpec((k, n), lambda i: (0, 0)),
                  tab_spec, tab_spec, tab_spec],
        out_specs=pl.BlockSpec((tm, n), lambda i: (i, 0)),
        out_shape=jax.ShapeDtypeStruct((m, n), F32),
        compiler_params=_params("parallel"),
        name="inproj",
    )(xb, w, c, sa, sb)


def _rope_tables(pos):
    half = ROPE_DIM // 2
    inv = ROPE_THETA ** (-jnp.arange(half, dtype=F32) / half)
    ang = pos.astype(F32)[:, None] * inv[None, :]
    cos, sin = jnp.cos(ang), jnp.sin(ang)
    t = pos.shape[0]
    ones = jnp.ones((t, LANES - ROPE_DIM), F32)
    zeros = jnp.zeros((t, LANES - half), F32)
    c = jnp.concatenate([cos, cos, ones], axis=1)
    sa = jnp.concatenate([-sin, zeros], axis=1)
    sb = jnp.concatenate([jnp.zeros((t, half), F32), sin, jnp.zeros((t, LANES - ROPE_DIM), F32)], axis=1)
    return c, sa, sb


def _project_all(xb, wparts, tabs):
    qa = _inproj(xb, wparts["qa"], tabs, [0] * 8)
    kva = _inproj(xb, wparts["kva"], tabs, [0] * 16)
    qb = _inproj(xb, wparts["qb"], tabs, [1] * 8)
    nsa = _inproj(xb, wparts["nsa"], tabs, [1, 1, 0, 0, 1, 1, 0, 0])
    win = _inproj(xb, wparts["win"], tabs, [1, 1, 0, 0])
    gates = _inproj(xb, wparts["gates"], tabs, [0, 0], sigmoid=True)
    return qa, kva, qb, nsa, win, gates


def _sb_block(q, k, v, mask, c, tri):
    ls, lr = _sb_logs(_dot_nt(q, k), mask)
    return _sb_weights(ls, lr, mask, c, tri), c + jnp.sum(lr, axis=1, keepdims=True)


def _sb_logs(qk, mask):
    z = qk * SCALE
    ls = jnp.minimum(z, 0.0) - jnp.log(1.0 + jnp.exp(-jnp.abs(z)))
    lr = ls - z
    return ls, (lr if mask is None else jnp.where(mask, lr, 0.0))


def _sb_weights(ls, lr, mask, c, tri):
    w = jnp.exp(ls + _dot_split(lr, tri) + c)
    return w if mask is None else jnp.where(mask, w, 0.0)


def _tri(n):
    row = lax.broadcasted_iota(jnp.int32, (n, n), 0)
    col = lax.broadcasted_iota(jnp.int32, (n, n), 1)
    return (row > col).astype(BF16)


def _sb_prompt_kernel(q_ref, k_ref, v_ref, o_ref, *, tq, tk):
    i = pl.program_id(2)
    q = q_ref[...].astype(BF16)
    tri = _tri(tk)
    per_q = tq // tk
    row = i * tq + lax.broadcasted_iota(jnp.int32, (tq, tk), 0)
    col = lax.broadcasted_iota(jnp.int32, (tq, tk), 1)

    def block(j, carry, masked):
        c, acc = carry
        off = pl.multiple_of(j * tk, tk)
        k = k_ref[pl.ds(off, tk), :].astype(BF16)
        v = v_ref[pl.ds(off, tk), :].astype(BF16)
        w, c = _sb_block(q, k, v, (off + col < row) if masked else None, c, tri)
        return c, acc + _dot(w.astype(BF16), v)

    carry = (jnp.zeros((tq, 1), F32), jnp.zeros((tq, HEAD_DIM), F32))
    for d in range(per_q):
        carry = block((i + 1) * per_q - 1 - d, carry, True)
    _, acc = lax.fori_loop(0, i * per_q, lambda s, cr: block(i * per_q - 1 - s, cr, False), carry)
    o_ref[...] = acc


def _sb_prompt(qa, kva, bsz, t):
    tq, tk = min(1024, t), 256
    nq = t // tq
    return pl.pallas_call(
        functools.partial(_sb_prompt_kernel, tq=tq, tk=tk),
        grid=(bsz, N_HEADS_A, nq),
        in_specs=[pl.BlockSpec((tq, HEAD_DIM), lambda b, h, i: (b * nq + i, h)),
                  pl.BlockSpec((t, HEAD_DIM), lambda b, h, i: (b, h)),
                  pl.BlockSpec((t, HEAD_DIM), lambda b, h, i: (b, N_HEADS_A + h))],
        out_specs=pl.BlockSpec((tq, HEAD_DIM), lambda b, h, i: (b * nq + i, h)),
        out_shape=jax.ShapeDtypeStruct((bsz * t, D_A), F32),
        compiler_params=_params("parallel", "parallel", "parallel"),
        name="sb_prompt",
    )(qa, kva, kva)


SB_PAGES_PER_STEP = 8
PAD_T = SUBLANES


def _sb_decode_kernel(pt_ref, q_ref, new_ref, *refs, page):
    pages = refs[:SB_PAGES_PER_STEP]
    o_ref, c_ref, acc_ref = refs[SB_PAGES_PER_STEP:]
    p = pl.program_id(1)
    rows = N_HEADS_A * PAD_T
    rec = 2 * N_HEADS_A
    tri = _tri(page)
    qh = [q_ref[h].astype(BF16) for h in range(N_HEADS_A)]

    def scores(ref):
        return jnp.concatenate(
            [_dot_nt(qh[h], ref[pl.ds(h, page, stride=rec), :].astype(BF16)) for h in range(N_HEADS_A)],
            axis=0)

    def values(ref, w):
        wb = w.astype(BF16)
        return [_dot(wb[h * PAD_T:(h + 1) * PAD_T],
                     ref[pl.ds(N_HEADS_A + h, page, stride=rec), :].astype(BF16)) for h in range(N_HEADS_A)]

    @pl.when(p == 0)
    def _():
        t_of_row = lax.broadcasted_iota(jnp.int32, (rows, page), 0) % PAD_T
        mask = lax.broadcasted_iota(jnp.int32, (rows, page), 1) < t_of_row
        ls, lr = _sb_logs(scores(new_ref), mask)
        w = _sb_weights(ls, lr, mask, jnp.zeros((rows, 1), F32), tri)
        acc_ref[...] = jnp.concatenate(values(new_ref, w), axis=0)
        c_ref[...] = jnp.sum(lr, axis=1, keepdims=True)

    @pl.when(p > 0)
    def _():
        logs = [_sb_logs(scores(pg), None) for pg in pages]
        c = c_ref[...]
        acc = None
        for pg, (ls, lr) in zip(pages, logs):
            part = values(pg, _sb_weights(ls, lr, None, c, tri))
            acc = part if acc is None else [x + y for x, y in zip(acc, part)]
            c = c + jnp.sum(lr, axis=1, keepdims=True)
        c_ref[...] = c
        acc_ref[...] += jnp.concatenate(acc, axis=0)

    @pl.when(p == pl.num_programs(1) - 1)
    def _():
        o_ref[...] = acc_ref[...]


def _sb_decode(q_heads, new_rows, cache_rows, page_table, page):
    bsz = q_heads.shape[0]
    n_pages = page_table.shape[1]
    pps = SB_PAGES_PER_STEP
    steps = n_pages // pps
    rows = N_HEADS_A * PAD_T
    blk = page * 2 * N_HEADS_A
    page_specs = [
        pl.BlockSpec((blk, HEAD_DIM),
                     lambda b, p, pt, k=k: (pt[b, n_pages - 1 - (jnp.maximum(p - 1, 0) * pps + k)], 0))
        for k in range(pps)]
    grid_spec = pltpu.PrefetchScalarGridSpec(
        num_scalar_prefetch=1,
        grid=(bsz, steps + 1),
        in_specs=[pl.BlockSpec((None, N_HEADS_A, PAD_T, HEAD_DIM), lambda b, p, pt: (b, 0, 0, 0)),
                  pl.BlockSpec((None, blk, HEAD_DIM), lambda b, p, pt: (b, 0, 0))] + page_specs,
        out_specs=pl.BlockSpec((None, rows, HEAD_DIM), lambda b, p, pt: (b, 0, 0)),
        scratch_shapes=[pltpu.VMEM((rows, 1), F32), pltpu.VMEM((rows, HEAD_DIM), F32)],
    )
    return pl.pallas_call(
        functools.partial(_sb_decode_kernel, page=page),
        grid_spec=grid_spec,
        out_shape=jax.ShapeDtypeStruct((bsz, rows, HEAD_DIM), F32),
        compiler_params=_params("parallel", "arbitrary"),
        name="sb_decode",
    )(page_table, q_heads, new_rows, *([cache_rows] * pps))


def _compress_partial(load_rows, pe_ref, w1_ref, nchunk):
    half = CMP_LEN // 2
    a = jnp.zeros((nchunk, HEAD_DIM), F32)
    b = jnp.zeros((nchunk, HEAD_DIM), F32)
    for l in range(half):
        rows = load_rows(l)
        a = a + _dot((rows + pe_ref[l:l + 1, :]).astype(BF16), w1_ref[l])
        b = b + _dot((rows + pe_ref[half + l:half + l + 1, :]).astype(BF16), w1_ref[half + l])
    return a, b


def _compress_finish(a, b, w2):
    n = a.shape[0]
    pre = a + pltpu.roll(b, n - 1, 0)
    return _dot(jax.nn.gelu(pre).astype(BF16), w2)


def _compress_prompt_kernel(r_ref, pe_ref, w1_ref, w2_ref, o_ref, *, nchunk):
    load = lambda l: r_ref[pl.ds(l, nchunk, stride=CMP_STRIDE), :]
    a, b = _compress_partial(load, pe_ref, w1_ref, nchunk)
    o_ref[...] = _compress_finish(a, b, w2_ref[...])


def _compress_prompt(nsa, pe, w1, w2, bsz, t):
    nchunk = t // CMP_STRIDE
    return pl.pallas_call(
        functools.partial(_compress_prompt_kernel, nchunk=nchunk),
        grid=(bsz, 4),
        in_specs=[pl.BlockSpec((t, HEAD_DIM), lambda b, s: (b, s)),
                  pl.BlockSpec((None, CMP_LEN, HEAD_DIM), lambda b, s: (s // 2, 0, 0)),
                  pl.BlockSpec((None, CMP_LEN, HEAD_DIM, HEAD_DIM), lambda b, s: (s // 2, 0, 0, 0)),
                  pl.BlockSpec((None, HEAD_DIM, HEAD_DIM), lambda b, s: (s // 2, 0, 0))],
        out_specs=pl.BlockSpec((None, None, nchunk, HEAD_DIM), lambda b, s: (b, s, 0, 0)),
        out_shape=jax.ShapeDtypeStruct((bsz, 4, nchunk, HEAD_DIM), F32),
        compiler_params=_params("parallel", "parallel"),
        name="compress_prompt",
    )(nsa, pe, w1, w2)


PAGES_PER_STEP = 16


NSA_REC = 4 * N_KV_B


def _compress_pages_kernel(pt_ref, pe_ref, w1_ref, *refs, chunks_per_page):
    pages = refs[:PAGES_PER_STEP]
    a_ref, b_ref = refs[PAGES_PER_STEP:]
    nchunk = PAGES_PER_STEP * chunks_per_page
    for s in range(2 * N_KV_B):
        load = lambda l, s=s: jnp.concatenate(
            [pg[pl.ds(l * NSA_REC + s, chunks_per_page, stride=CMP_STRIDE * NSA_REC), :] for pg in pages],
            axis=0)
        a, b = _compress_partial(load, pe_ref.at[s // N_KV_B], w1_ref.at[s // N_KV_B], nchunk)
        a_ref[s] = a
        b_ref[s] = b


def _compress_pages(cache_rows, page_table, pe, w1, page):
    bsz, n_pages = page_table.shape
    cpp = page // CMP_STRIDE
    nchunk = PAGES_PER_STEP * cpp
    steps = n_pages // PAGES_PER_STEP
    page_specs = [
        pl.BlockSpec((page * NSA_REC, HEAD_DIM),
                     lambda b, s, pt, k=k: (pt[b, s * PAGES_PER_STEP + k], 0))
        for k in range(PAGES_PER_STEP)]
    out_spec = pl.BlockSpec((None, 2 * N_KV_B, nchunk, HEAD_DIM), lambda b, s, pt: (b, 0, s, 0))
    grid_spec = pltpu.PrefetchScalarGridSpec(
        num_scalar_prefetch=1,
        grid=(bsz, steps),
        in_specs=[pl.BlockSpec((2, CMP_LEN, HEAD_DIM), lambda b, s, pt: (0, 0, 0)),
                  pl.BlockSpec((2, CMP_LEN, HEAD_DIM, HEAD_DIM), lambda b, s, pt: (0, 0, 0, 0))]
                 + page_specs,
        out_specs=[out_spec, out_spec],
    )
    shape = jax.ShapeDtypeStruct((bsz, 2 * N_KV_B, n_pages * cpp, HEAD_DIM), F32)
    return pl.pallas_call(
        functools.partial(_compress_pages_kernel, chunks_per_page=cpp),
        grid_spec=grid_spec,
        out_shape=[shape, shape],
        compiler_params=_params("parallel", "parallel"),
        name="compress_pages",
    )(page_table, pe, w1, *([cache_rows] * PAGES_PER_STEP))


def _cmp_branch(qs, kcmp, vcmp, pos_rows):
    n = kcmp.shape[0]
    s = _dot_nt(qs, kcmp.astype(BF16)) * SCALE
    cmp_end = lax.broadcasted_iota(jnp.int32, (1, n), 1) * CMP_STRIDE + (CMP_LEN - 1)
    mask = cmp_end <= pos_rows
    s = jnp.where(mask, s, NEG_BIG)
    m = jnp.max(s, axis=-1, keepdims=True)
    e = jnp.where(mask, jnp.exp(s - m), 0.0)
    p = e / jnp.maximum(jnp.sum(e, axis=-1, keepdims=True), 1e-30)
    return _dot(p.astype(BF16), vcmp.astype(BF16)), p


def _select_blocks(imp, pos_rows, n_sel):
    r, width = imp.shape
    lane = lax.broadcasted_iota(jnp.int32, (r, width), 1)
    qblk = pos_rows // SLC_BLOCK
    forced = (lane == 0) | (lane == qblk) | (lane == qblk - 1)
    valid = lane * SLC_BLOCK <= pos_rows
    imp = jnp.where(forced, FORCE_SCORE, jnp.where(valid, imp, -FORCE_SCORE))
    imp = jnp.where(lane < n_sel, imp, -jnp.inf)
    rank = jnp.zeros((r, width), F32)
    for i in range(n_sel):
        ci = imp[:, i:i + 1]
        better = (ci > imp) | ((ci == imp) & (lane > i))
        rank = rank + jnp.where(better, 1.0, 0.0)
    return jnp.where((rank < float(min(N_SELECT, n_sel))) & (lane < n_sel), 1.0, 0.0)


def _apply_mask(x, mask, fill):
    n = mask.shape[0]
    if x.shape[0] == n:
        return jnp.where(mask, x, fill)
    return jnp.concatenate([jnp.where(mask, x[i:i + n], fill) for i in range(0, x.shape[0], n)], axis=0)


def _softmax_step(s, mask, v, carry, zero_masked=True):
    m, l, acc = carry
    s = _apply_mask(s, mask, NEG_BIG)
    m_new = jnp.maximum(m, jnp.max(s, axis=-1, keepdims=True))
    alpha = jnp.exp(m - m_new)
    e = jnp.exp(s - m_new)
    if zero_masked:
        e = _apply_mask(e, mask, 0.0)
    l = alpha * l + jnp.sum(e, axis=-1, keepdims=True)
    acc = alpha * acc + _dot(e.astype(BF16), v)
    return m_new, l, acc


def _softmax_init(r):
    return (jnp.full((r, 1), NEG_BIG, F32), jnp.zeros((r, 1), F32), jnp.zeros((r, HEAD_DIM), F32))


def _softmax_finish(carry):
    _, l, acc = carry
    return acc / jnp.maximum(l, 1e-30)


def _select_blocks_t(imp, pos, n_sel):
    rows, t = imp.shape
    blk = lax.broadcasted_iota(jnp.int32, (rows, t), 0)
    qblk = pos // SLC_BLOCK
    forced = (blk == 0) | (blk == qblk) | (blk == qblk - 1)
    valid = blk * SLC_BLOCK <= pos
    imp = jnp.where(forced, FORCE_SCORE, jnp.where(valid, imp, -FORCE_SCORE))
    imp = jnp.where(blk < n_sel, imp, -jnp.inf)
    rank = jnp.zeros((rows, t), F32)
    for i in range(n_sel):
        ri = imp[i:i + 1, :]
        better = (ri > imp) | ((ri == imp) & (blk > i))
        rank = rank + jnp.where(better, 1.0, 0.0)
    return jnp.where((rank < float(min(N_SELECT, n_sel))) & (blk < n_sel), 1.0, 0.0)


def _nsa_prompt_kernel(q_ref, g_ref, kc_ref, vc_ref, ks_ref, vs_ref, kw_ref, vw_ref,
                       mt_ref, e_ref, o_ref, *, tq, n_sel):
    i = pl.program_id(2)
    r = HPG * tq
    qs = jnp.concatenate([q_ref[:, h * HEAD_DIM:(h + 1) * HEAD_DIM] for h in range(HPG)],
                         axis=0).astype(BF16)
    pos_t = i * tq + lax.broadcasted_iota(jnp.int32, (tq, 1), 0)
    pos_r = jnp.concatenate([pos_t] * HPG, axis=0)

    o_c, p = _cmp_branch(qs, kc_ref[...], vc_ref[...], pos_r)
    psum = p[0:tq]
    for h in range(1, HPG):
        psum = psum + p[h * tq:(h + 1) * tq]
    hi = psum.astype(BF16)
    lo = (psum - hi.astype(F32)).astype(BF16)
    imp_t = _dot_nt(mt_ref[...], hi) + _dot_nt(mt_ref[...], lo)
    sel_rows = -(-n_sel // 8) * 8
    pos_lane = i * tq + lax.broadcasted_iota(jnp.int32, (1, tq), 1)
    sel_t = _select_blocks_t(imp_t[0:sel_rows], pos_lane, n_sel)
    sel = jnp.concatenate([sel_t, jnp.zeros((imp_t.shape[0] - sel_rows, tq), F32)], axis=0).T.astype(BF16)

    lane = lax.broadcasted_iota(jnp.int32, (tq, tq), 1)

    def slc_chunk(c, carry, diagonal):
        off = pl.multiple_of(c * tq, tq)
        k = ks_ref[pl.ds(off, tq), :].astype(BF16)
        v = vs_ref[pl.ds(off, tq), :].astype(BF16)
        hit = _dot(sel, e_ref[:, pl.ds(off, tq)])
        if diagonal:
            hit = jnp.where(off + lane <= pos_t, hit, 0.0)
        return _softmax_step(_dot_nt(qs, k) * SCALE, hit > 0.5, v, carry, zero_masked=False)

    carry = lax.fori_loop(0, i, lambda c, cr: slc_chunk(c, cr, False), _softmax_init(r))
    o_s = _softmax_finish(slc_chunk(i, carry, True))

    carry = _softmax_init(r)
    for d in range(-(-WINDOW // tq), -1, -1):
        c = i - d
        off = pl.multiple_of(jnp.maximum(c, 0) * tq, tq)
        k = kw_ref[pl.ds(off, tq), :].astype(BF16)
        v = vw_ref[pl.ds(off, tq), :].astype(BF16)
        kpos = c * tq + lane
        dlt = pos_t - kpos
        mask = (dlt >= 0) & (dlt < WINDOW) & (kpos >= 0)
        carry = _softmax_step(_dot_nt(qs, k) * SCALE, mask, v, carry, zero_masked=False)
    o_w = _softmax_finish(carry)

    for h in range(HPG):
        rows = slice(h * tq, (h + 1) * tq)
        gc = g_ref[:, 3 * h:3 * h + 1]
        gs = g_ref[:, 3 * h + 1:3 * h + 2]
        gw = g_ref[:, 3 * h + 2:3 * h + 3]
        o_ref[:, h * HEAD_DIM:(h + 1) * HEAD_DIM] = gc * o_c[rows] + gs * o_s[rows] + gw * o_w[rows]


def _cmp_to_slc(n_cmp_pad, n_cmp, n_sel, width):
    i = np.arange(n_cmp_pad)[:, None]
    j = np.arange(width)[None, :]
    lo = np.maximum(i * CMP_STRIDE, j * SLC_BLOCK)
    hi = np.minimum(i * CMP_STRIDE + CMP_LEN, (j + 1) * SLC_BLOCK)
    m = np.maximum(hi - lo, 0) / CMP_STRIDE
    m = np.where((i < n_cmp) & (j < n_sel), m, 0.0)
    return jnp.asarray(m, dtype=BF16)


def _nsa_prompt(qb, gates, cmp_kv, nsa, win, bsz, t):
    tq = min(512, t)
    nq = t // tq
    n_cmp = (t - CMP_LEN) // CMP_STRIDE + 1
    n_cmp_pad = cmp_kv.shape[2]
    n_sel = -(-t // SLC_BLOCK)
    assert n_sel <= LANES
    m_mat = _cmp_to_slc(n_cmp_pad, n_cmp, n_sel, LANES).T
    e_mat = jnp.asarray(np.arange(LANES)[:, None] == (np.arange(t) // SLC_BLOCK)[None, :], dtype=BF16)
    full = lambda shape: pl.BlockSpec(shape, lambda b, g, i: (0,) * len(shape))
    rows = lambda col: pl.BlockSpec((t, HEAD_DIM), lambda b, g, i, col=col: (b, col + g))
    return pl.pallas_call(
        functools.partial(_nsa_prompt_kernel, tq=tq, n_sel=n_sel),
        grid=(bsz, N_KV_B, nq),
        in_specs=[pl.BlockSpec((tq, HPG * HEAD_DIM), lambda b, g, i: (b * nq + i, g)),
                  pl.BlockSpec((tq, LANES), lambda b, g, i: (b * nq + i, g)),
                  pl.BlockSpec((None, None, n_cmp_pad, HEAD_DIM), lambda b, g, i: (b, g, 0, 0)),
                  pl.BlockSpec((None, None, n_cmp_pad, HEAD_DIM), lambda b, g, i: (b, 2 + g, 0, 0)),
                  rows(4), rows(6), rows(0), rows(2),
                  full(m_mat.shape), full(e_mat.shape)],
        out_specs=pl.BlockSpec((tq, HPG * HEAD_DIM), lambda b, g, i: (b * nq + i, g)),
        out_shape=jax.ShapeDtypeStruct((bsz * t, D_B), F32),
        compiler_params=_params("parallel", "parallel", "parallel"),
        name="nsa_prompt",
    )(qb, gates, cmp_kv, cmp_kv, nsa, nsa, win, win, m_mat, e_mat)


def _nsa_select_kernel(q_ref, ak_ref, av_ref, bk_ref, bv_ref, w2_ref, m_ref, oc_ref, sel_ref,
                       *, n_new, past, n_sel):
    qs = jnp.concatenate([q_ref[:, h * HEAD_DIM:(h + 1) * HEAD_DIM] for h in range(HPG)],
                         axis=0).astype(BF16)
    pos_t = past + lax.broadcasted_iota(jnp.int32, (n_new, 1), 0)
    pos_r = jnp.concatenate([pos_t] * HPG, axis=0)
    kcmp = _compress_finish(ak_ref[...], bk_ref[...], w2_ref[0])
    vcmp = _compress_finish(av_ref[...], bv_ref[...], w2_ref[1])
    o_c, p = _cmp_branch(qs, kcmp, vcmp, pos_r)
    oc_ref[...] = o_c
    psum = p[0:n_new]
    for h in range(1, HPG):
        psum = psum + p[h * n_new:(h + 1) * n_new]
    sel = _select_blocks(_dot_split(psum, m_ref[...]), pos_t, n_sel)
    sel_ref[...] = jnp.concatenate([sel] * HPG, axis=0)


def _nsa_select(qb, part_a, part_b, w2, past):
    bsz, n_new, _ = qb.shape
    n_cmp_pad = part_a.shape[2]
    total = past + n_new
    n_cmp = (total - CMP_LEN) // CMP_STRIDE + 1
    n_sel = -(-total // SLC_BLOCK)
    width = -(-n_sel // LANES) * LANES
    m_mat = _cmp_to_slc(n_cmp_pad, n_cmp, n_sel, width)
    assert n_cmp == n_cmp_pad - 1, "compressed tokens must come from the paged rows only"
    r = HPG * n_new
    k_spec = pl.BlockSpec((None, None, n_cmp_pad, HEAD_DIM), lambda b, g: (b, g, 0, 0))
    v_spec = pl.BlockSpec((None, None, n_cmp_pad, HEAD_DIM), lambda b, g: (b, N_KV_B + g, 0, 0))
    return pl.pallas_call(
        functools.partial(_nsa_select_kernel, n_new=n_new, past=past, n_sel=n_sel),
        grid=(bsz, N_KV_B),
        in_specs=[pl.BlockSpec((None, n_new, HPG * HEAD_DIM), lambda b, g: (b, 0, g)),
                  k_spec, v_spec, k_spec, v_spec,
                  pl.BlockSpec((2, HEAD_DIM, HEAD_DIM), lambda b, g: (0, 0, 0)),
                  pl.BlockSpec(m_mat.shape, lambda b, g: (0, 0))],
        out_specs=[pl.BlockSpec((None, None, r, HEAD_DIM), lambda b, g: (b, g, 0, 0)),
                   pl.BlockSpec((None, None, r, width), lambda b, g: (b, g, 0, 0))],
        out_shape=[jax.ShapeDtypeStruct((bsz, N_KV_B, r, HEAD_DIM), F32),
                   jax.ShapeDtypeStruct((bsz, N_KV_B, r, width), F32)],
        compiler_params=_params("parallel", "parallel"),
        name="nsa_select",
    )(qb, part_a, part_a, part_b, part_b, w2, m_mat)


SLC_PAGES_PER_STEP = 16


def _nsa_slc_decode_kernel(pt_ref, q_ref, sel_ref, new_ref, *refs, n_new, past, page):
    pps = SLC_PAGES_PER_STEP
    pages = refs[:pps]
    o_ref, m_ref, l_ref, acc_ref = refs[pps:]
    p = pl.program_id(1)
    n_steps = pl.num_programs(1) - 1
    r = HPG * n_new
    rows = N_KV_B * r
    t_of_row = lax.broadcasted_iota(jnp.int32, (rows, 1), 0) % n_new
    lane = lax.broadcasted_iota(jnp.int32, (rows, page), 1)
    bpp = page // SLC_BLOCK
    qg = [q_ref[g].astype(BF16) for g in range(N_KV_B)]
    sel = jnp.concatenate([sel_ref[g] for g in range(N_KV_B)], axis=0)
    blk_lane = lax.broadcasted_iota(jnp.int32, sel.shape, 1)

    @pl.when(p == 0)
    def _():
        m_ref[...] = jnp.full(m_ref.shape, NEG_BIG, F32)
        l_ref[...] = jnp.zeros(l_ref.shape, F32)
        acc_ref[...] = jnp.zeros(acc_ref.shape, F32)

    def masked_scores(ref, first_pos):
        first_blk = first_pos // SLC_BLOCK
        hit = jnp.zeros((rows, page), F32)
        for j in range(bpp):
            flag = jnp.sum(jnp.where(blk_lane == first_blk + j, sel, 0.0), axis=-1, keepdims=True)
            hit = jnp.where(lane // SLC_BLOCK == j, flag, hit)
        mask = (hit > 0.5) & (first_pos + lane <= past + t_of_row)
        s = jnp.concatenate(
            [_dot_nt(qg[g], ref[pl.ds(2 * N_KV_B + g, page, stride=NSA_REC), :].astype(BF16))
             for g in range(N_KV_B)], axis=0) * SCALE
        return jnp.where(mask, s, NEG_BIG), mask

    def visit(blocks):
        scored = [masked_scores(ref, pos) for ref, pos in blocks]
        m_old = m_ref[...]
        m_new = m_old
        for s, _ in scored:
            m_new = jnp.maximum(m_new, jnp.max(s, axis=-1, keepdims=True))
        alpha = jnp.exp(m_old - m_new)
        l = alpha * l_ref[...]
        acc = alpha * acc_ref[...]
        for (ref, _), (s, mask) in zip(blocks, scored):
            e = jnp.where(mask, jnp.exp(s - m_new), 0.0)
            l = l + jnp.sum(e, axis=-1, keepdims=True)
            eb = e.astype(BF16)
            acc = acc + jnp.concatenate(
                [_dot(eb[g * r:(g + 1) * r], ref[pl.ds(3 * N_KV_B + g, page, stride=NSA_REC), :].astype(BF16))
                 for g in range(N_KV_B)], axis=0)
        m_ref[...] = m_new
        l_ref[...] = l
        acc_ref[...] = acc

    @pl.when(p < n_steps)
    def _():
        visit([(pg, (p * pps + k) * page) for k, pg in enumerate(pages)])

    @pl.when(p == n_steps)
    def _():
        visit([(new_ref, past)])
        o_ref[...] = acc_ref[...] / jnp.maximum(l_ref[...], 1e-30)


def _nsa_slc_decode(q_groups, sel, new_rows, cache_rows, page_table, past, page, n_new):
    bsz = q_groups.shape[0]
    n_pages = page_table.shape[1]
    pps = SLC_PAGES_PER_STEP
    steps = n_pages // pps
    r = HPG * n_new
    rows = N_KV_B * r
    width = sel.shape[-1]
    blk = page * NSA_REC
    page_specs = [
        pl.BlockSpec((blk, HEAD_DIM),
                     lambda b, p, pt, k=k: (pt[b, jnp.minimum(p, steps - 1) * pps + k], 0))
        for k in range(pps)]
    grid_spec = pltpu.PrefetchScalarGridSpec(
        num_scalar_prefetch=1,
        grid=(bsz, steps + 1),
        in_specs=[pl.BlockSpec((None, N_KV_B, r, HEAD_DIM), lambda b, p, pt: (b, 0, 0, 0)),
                  pl.BlockSpec((None, N_KV_B, r, width), lambda b, p, pt: (b, 0, 0, 0)),
                  pl.BlockSpec((None, blk, HEAD_DIM), lambda b, p, pt: (b, 0, 0))] + page_specs,
        out_specs=pl.BlockSpec((None, rows, HEAD_DIM), lambda b, p, pt: (b, 0, 0)),
        scratch_shapes=[pltpu.VMEM((rows, 1), F32), pltpu.VMEM((rows, 1), F32),
                        pltpu.VMEM((rows, HEAD_DIM), F32)],
    )
    return pl.pallas_call(
        functools.partial(_nsa_slc_decode_kernel, n_new=n_new, past=past, page=page),
        grid_spec=grid_spec,
        out_shape=jax.ShapeDtypeStruct((bsz, rows, HEAD_DIM), F32),
        compiler_params=_params("parallel", "arbitrary"),
        name="nsa_slc_decode",
    )(page_table, q_groups, sel, new_rows, *([cache_rows] * pps))


def _nsa_merge_decode_kernel(q_ref, g_ref, oc_ref, os_ref, st_ref, new_ref, o_ref, nw_ref,
                             *, n_new, past, n_buf):
    r = HPG * n_new
    n_pad = new_ref.shape[0]
    t_of_row = lax.broadcasted_iota(jnp.int32, (r, 1), 0) % n_new
    pos_r = past + t_of_row
    buf_pos = past - n_buf + lax.broadcasted_iota(jnp.int32, (r, n_buf), 1)
    new_pos = past + lax.broadcasted_iota(jnp.int32, (r, n_pad), 1)

    def in_window(kpos):
        dlt = pos_r - kpos
        return (dlt >= 0) & (dlt < WINDOW) & (kpos >= 0)

    for g in range(N_KV_B):
        qs = jnp.concatenate(
            [q_ref[:, (g * HPG + h) * HEAD_DIM:(g * HPG + h + 1) * HEAD_DIM] for h in range(HPG)],
            axis=0).astype(BF16)
        kcol = slice(g * HEAD_DIM, (g + 1) * HEAD_DIM)
        vcol = slice((N_KV_B + g) * HEAD_DIM, (N_KV_B + g + 1) * HEAD_DIM)
        carry = _softmax_init(r)
        carry = _softmax_step(_dot_nt(qs, st_ref[:, kcol].astype(BF16)) * SCALE, in_window(buf_pos),
                              st_ref[:, vcol].astype(BF16), carry)
        carry = _softmax_step(_dot_nt(qs, new_ref[:, kcol].astype(BF16)) * SCALE, in_window(new_pos),
                              new_ref[:, vcol].astype(BF16), carry)
        o_w = _softmax_finish(carry)
        for h in range(HPG):
            rows = slice(h * n_new, (h + 1) * n_new)
            c0 = g * LANES + 3 * h
            out = (g_ref[:, c0:c0 + 1] * oc_ref[g, rows] + g_ref[:, c0 + 1:c0 + 2] * os_ref[g, rows]
                   + g_ref[:, c0 + 2:c0 + 3] * o_w[rows])
            o_ref[:, (g * HPG + h) * HEAD_DIM:(g * HPG + h + 1) * HEAD_DIM] = out
    shifted = pltpu.roll(st_ref[...], n_buf - n_new, 0)
    placed = pltpu.roll(new_ref[...], n_pad - n_new, 0)
    tail_row = lax.broadcasted_iota(jnp.int32, (n_pad, placed.shape[1]), 0)
    nw_ref[0:n_buf - n_pad, :] = shifted[0:n_buf - n_pad]
    nw_ref[n_buf - n_pad:n_buf, :] = jnp.where(tail_row >= n_pad - n_new, placed, shifted[n_buf - n_pad:])


def _nsa_merge_decode(qb, gates, o_cmp, o_slc, state_win, win_new, past):
    bsz, n_new, _ = qb.shape
    n_buf = state_win.shape[1]
    n_pad = win_new.shape[1]
    keep = min(WINDOW, past + n_new)
    assert keep == n_buf and n_buf > n_pad >= n_new
    r = HPG * n_new
    cols = 2 * D_KV_B
    per_b = lambda shape: pl.BlockSpec((None,) + shape, lambda b: (b,) + (0,) * len(shape))
    return pl.pallas_call(
        functools.partial(_nsa_merge_decode_kernel, n_new=n_new, past=past, n_buf=n_buf),
        grid=(bsz,),
        in_specs=[per_b((n_new, D_B)), per_b((n_new, 2 * LANES)),
                  per_b((N_KV_B, r, HEAD_DIM)), per_b((N_KV_B, r, HEAD_DIM)),
                  per_b((n_buf, cols)), per_b((n_pad, cols))],
        out_specs=[per_b((n_new, D_B)), per_b((keep, cols))],
        out_shape=[jax.ShapeDtypeStruct((bsz, n_new, D_B), F32),
                   jax.ShapeDtypeStruct((bsz, keep, cols), F32)],
        compiler_params=_params("parallel"),
        name="nsa_merge_decode",
    )(qb, gates, o_cmp, o_slc, state_win, win_new)


def _outproj_kernel(x_ref, oa_ref, ob_ref, wa_ref, wb_ref, g_ref, b_ref, h_ref):
    y = _dot(oa_ref[...].astype(BF16), wa_ref[...]) + _dot(ob_ref[...].astype(BF16), wb_ref[...])
    h_ref[...] = _layer_norm(ALPHA * x_ref[...] + y, g_ref[...], b_ref[...])


def _outproj(x, o_a, o_b, wa, wb, g, b):
    m, d = x.shape
    tm = min(512, m)
    row = lambda w: pl.BlockSpec((tm, w), lambda i: (i, 0))
    full = lambda a: pl.BlockSpec(a.shape, lambda i: (0, 0))
    return pl.pallas_call(
        _outproj_kernel,
        grid=(m // tm,),
        in_specs=[row(d), row(D_A), row(D_B), full(wa), full(wb), full(g), full(b)],
        out_specs=row(d),
        out_shape=jax.ShapeDtypeStruct((m, d), F32),
        compiler_params=_params("parallel"),
        name="outproj_ln",
    )(x, o_a, o_b, wa, wb, g, b)


def _peer_query_kernel(h_ref, w_ref, q_ref):
    q_ref[...] = _dot(h_ref[...].astype(BF16), w_ref[...])


def _peer_query(h, wq):
    m, d = h.shape
    n = wq.shape[1]
    tm = min(512, m)
    return pl.pallas_call(
        _peer_query_kernel,
        grid=(m // tm,),
        in_specs=[pl.BlockSpec((tm, d), lambda i: (i, 0)), pl.BlockSpec((d, n), lambda i: (0, 0))],
        out_specs=pl.BlockSpec((tm, n), lambda i: (i, 0)),
        out_shape=jax.ShapeDtypeStruct((m, n), F32),
        compiler_params=_params("parallel"),
        name="peer_query",
    )(h, wq)


def _compare_exchange(v, i, j, descending):
    hi, lo = jnp.maximum(v[i], v[j]), jnp.minimum(v[i], v[j])
    v[i], v[j] = (hi, lo) if descending else (lo, hi)


def _bitonic_merge_desc(v):
    j = len(v) // 2
    while j >= 1:
        for i in range(len(v)):
            if i ^ j > i:
                _compare_exchange(v, i, i ^ j, True)
        j //= 2


def _bitonic_sort_desc(v):
    n = len(v)
    k = 2
    while k <= n:
        j = k // 2
        while j >= 1:
            for i in range(n):
                if i ^ j > i:
                    _compare_exchange(v, i, i ^ j, (i & k) == 0)
            j //= 2
        k *= 2


def _top_sorted(x, k):
    v = [x[i:i + SUBLANES] for i in range(0, x.shape[0], SUBLANES)]
    _bitonic_sort_desc(v)
    d = 1
    while d < SUBLANES:
        other = [pltpu.roll(a, d, 0) for a in v]
        if 2 * len(v) <= k:
            v = v + other[::-1]
        else:
            v = [jnp.maximum(v[i], other[k - 1 - i]) for i in range(k)]
        _bitonic_merge_desc(v)
        d *= 2
    return [a[0:1] for a in v]


_PEER_PAIRS = [(i, j) for i in range(PEER_TOPK) for j in range(PEER_TOPK) if (i + 1) * (j + 1) <= PEER_TOPK]
_PEER_CAND_ROWS = SUBLANES * (1 << (-(-len(_PEER_PAIRS) // SUBLANES) - 1).bit_length())


def _peer_route_kernel(q_ref, sk_ref, thr_ref, s2_ref, e1_ref, e2_ref, cand_ref):
    half = sk_ref.shape[2]
    cand_ref[...] = jnp.full(cand_ref.shape, -jnp.inf, F32)
    for h in range(PEER_HEADS):
        s, tops = [], []
        for c in range(2):
            col = (2 * h + c) * half
            st = _dot_nt(sk_ref[c], q_ref[:, col:col + half].astype(BF16))
            s.append(st)
            tops.append(_top_sorted(st, PEER_TOPK))
        for n, (i, j) in enumerate(_PEER_PAIRS):
            cand_ref[n:n + 1, :] = tops[0][i] + tops[1][j]
        best = _top_sorted(cand_ref[...], PEER_TOPK)
        top = best[0]
        z = jnp.ones_like(top)
        for v in best[1:]:
            z = z + jnp.exp(v - top)
        tau = best[PEER_TOPK - 1]
        thr = jnp.full(s[0].shape, jnp.inf, F32)
        for i in reversed(range(PEER_TOPK)):
            thr_i = jnp.full_like(tau, jnp.inf)
            for j in range(PEER_TOPK):
                if (i + 1) * (j + 1) <= PEER_TOPK:
                    thr_i = jnp.where(tops[0][i] + tops[1][j] >= tau, tops[1][j], thr_i)
            thr = jnp.where(s[0] == tops[0][i], thr_i, thr)
        thr_ref[h] = thr
        s2_ref[h] = s[1]
        e1_ref[h] = jnp.exp(s[0] - tops[0][0])
        e2_ref[h] = jnp.exp(s[1] - tops[1][0]) / z


def _peer_route(q, sub_keys):
    n = q.shape[0]
    tm = 128
    nk = sub_keys.shape[1]
    big = jax.ShapeDtypeStruct((PEER_HEADS, nk, n), F32)
    big_spec = pl.BlockSpec((PEER_HEADS, nk, tm), lambda i: (0, 0, i))
    return pl.pallas_call(
        _peer_route_kernel,
        grid=(n // tm,),
        in_specs=[pl.BlockSpec((tm, q.shape[1]), lambda i: (i, 0)),
                  pl.BlockSpec(sub_keys.shape, lambda i: (0, 0, 0))],
        out_specs=[big_spec] * 4,
        out_shape=[big] * 4,
        scratch_shapes=[pltpu.VMEM((_PEER_CAND_ROWS, tm), F32)],
        compiler_params=_params("parallel"),
        name="peer_route",
    )(q, sub_keys)


def _peer_dense_kernel(h_ref, u_ref, v_ref, thr_ref, s2_ref, e1_ref, e2_ref, g_ref, b_ref,
                       o_ref, hb_ref, wt_ref, *, tm, ac, sub_ac):
    c = pl.program_id(1)
    nk = s2_ref.shape[1]
    kb_rows = nk // 4

    @pl.when(c == 0)
    def _():
        o_ref[...] = jnp.zeros(o_ref.shape, F32)
        hb_ref[...] = h_ref[...].astype(BF16)

    for sub in range(ac // sub_ac):
        ex = slice(sub * sub_ac * nk, (sub + 1) * sub_ac * nk)
        act = jax.nn.gelu(_dot_nt(u_ref[ex, :], hb_ref[...]))
        for ts in range(tm // LANES):
            tok = slice(ts * LANES, (ts + 1) * LANES)
            for kb in range(nk // kb_rows):
                keys = slice(kb * kb_rows, (kb + 1) * kb_rows)
                gates = [jnp.zeros((kb_rows, LANES), F32) for _ in range(sub_ac)]
                for h in range(PEER_HEADS):
                    s2 = s2_ref[h, keys, tok]
                    e2 = e2_ref[h, keys, tok]
                    for k in range(sub_ac):
                        aa = sub * sub_ac + k
                        chosen = s2 >= thr_ref[h, aa:aa + 1, tok]
                        gates[k] = gates[k] + jnp.where(chosen, e1_ref[h, aa:aa + 1, tok] * e2, 0.0)
                for k in range(sub_ac):
                    rows = slice(k * nk + kb * kb_rows, k * nk + (kb + 1) * kb_rows)
                    wt_ref[sub * sub_ac * nk + rows.start:sub * sub_ac * nk + rows.stop, tok] = (
                        gates[k] * act[rows, tok]).astype(BF16)
    o_ref[...] += _dot_tn(wt_ref[...], v_ref[...])

    @pl.when(c == pl.num_programs(1) - 1)
    def _():
        o_ref[...] = _layer_norm(ALPHA * h_ref[...] + o_ref[...], g_ref[...], b_ref[...])


def _peer_dense(h, u, v, route, g, b):
    n, d = h.shape
    thr, s2, e1, e2 = route
    nk = s2.shape[1]
    tm = min(256, n)
    ac = 16
    big_spec = pl.BlockSpec((PEER_HEADS, nk, tm), lambda i, c: (0, 0, i))
    row_spec = pl.BlockSpec((PEER_HEADS, ac, tm), lambda i, c: (0, c, i))
    tab_spec = pl.BlockSpec((ac * nk, d), lambda i, c: (c, 0))
    full = lambda a: pl.BlockSpec(a.shape, lambda i, c: (0, 0))
    return pl.pallas_call(
        functools.partial(_peer_dense_kernel, tm=tm, ac=ac, sub_ac=4),
        grid=(n // tm, nk // ac),
        in_specs=[pl.BlockSpec((tm, d), lambda i, c: (i, 0)), tab_spec, tab_spec,
                  row_spec, big_spec, row_spec, big_spec, full(g), full(b)],
        out_specs=pl.BlockSpec((tm, d), lambda i, c: (i, 0)),
        out_shape=jax.ShapeDtypeStruct((n, d), F32),
        scratch_shapes=[pltpu.VMEM((tm, d), BF16), pltpu.VMEM((ac * nk, tm), BF16)],
        compiler_params=_params("parallel", "arbitrary"),
        name="peer_dense",
    )(h, u, v, thr, s2, e1, e2, g, b)


def _post_block(x, o_a, o_b, wts):
    h = _outproj(x, o_a, o_b, wts["wo_a"], wts["wo_b"], wts["ln1_g"], wts["ln1_b"])
    q = _peer_query(h, wts["wq"])
    route = _peer_route(q, wts["sub_keys"])
    return _peer_dense(h, wts["u"], wts["v"], route, wts["ln2_g"], wts["ln2_b"])


def _pad_rows(a, rows):
    return jnp.pad(a, ((0, rows - a.shape[0]),) + ((0, 0),) * (a.ndim - 1))


def kernel(x_prompt, x_sample, cache_a, cache_nsa, state_win, page_table, w_in, cmp_pe_k, cmp_w1_k,
           cmp_w2_k, cmp_pe_v, cmp_w1_v, cmp_w2_v, w_out, ln1_g, ln1_b, peer_w_query, peer_sub_keys,
           peer_u, peer_v, ln2_g, ln2_b):
    assert w_in.shape[0] == DEPTH == 1
    bsz, t, d = x_prompt.shape
    dbsz, n_new, _ = x_sample.shape
    n_pool, page = cache_a.shape[1], cache_a.shape[2]
    past = page_table.shape[1] * page

    w = w_in[0]
    o = np.cumsum((0, D_A, 2 * D_A, D_B, 4 * D_KV_B, 2 * D_KV_B, 3 * N_HEADS_B))
    wg = w[:, o[5]:o[6]]
    per_g = 3 * HPG
    gate_w = jnp.concatenate(
        [jnp.pad(wg[:, g * per_g:(g + 1) * per_g], ((0, 0), (0, LANES - per_g))) for g in range(N_KV_B)],
        axis=1)
    wparts = {"qa": w[:, o[0]:o[1]], "kva": w[:, o[1]:o[2]], "qb": w[:, o[2]:o[3]],
              "nsa": w[:, o[3]:o[4]], "win": w[:, o[4]:o[5]], "gates": gate_w}
    wparts = {k: v.astype(BF16) for k, v in wparts.items()}
    row2 = lambda a: a[0].reshape(1, -1)
    wts = {"wo_a": w_out[0, :D_A].astype(BF16), "wo_b": w_out[0, D_A:].astype(BF16),
           "ln1_g": row2(ln1_g), "ln1_b": row2(ln1_b), "ln2_g": row2(ln2_g), "ln2_b": row2(ln2_b),
           "wq": peer_w_query[0].astype(BF16), "sub_keys": peer_sub_keys[0].astype(BF16),
           "u": peer_u[0].astype(BF16), "v": peer_v[0].astype(BF16)}
    pe = jnp.stack([cmp_pe_k[0], cmp_pe_v[0]])
    w1 = jnp.stack([cmp_w1_k[0], cmp_w1_v[0]]).astype(BF16)
    w2 = jnp.stack([cmp_w2_k[0], cmp_w2_v[0]]).astype(BF16)

    xp = x_prompt.reshape(bsz * t, d)
    tabs_p = _rope_tables(jnp.arange(t, dtype=jnp.int32))
    qa, kva, qb, nsa, win, gates = _project_all(xp.astype(BF16), wparts, tabs_p)
    o_a = _sb_prompt(qa, kva, bsz, t)
    cmp_kv = _compress_prompt(nsa, pe, w1, w2, bsz, t)
    o_b = _nsa_prompt(qb, gates, cmp_kv, nsa, win, bsz, t)
    y_prompt = _post_block(xp, o_a, o_b, wts).reshape(bsz, t, d)
    new_a_p = kva.reshape(1, bsz, t, 2, N_HEADS_A, HEAD_DIM)
    new_nsa_p = nsa.reshape(1, bsz, t, 4, N_KV_B, HEAD_DIM)
    keep_p = min(WINDOW, t)
    new_win_p = win.reshape(bsz, t, 2, N_KV_B, HEAD_DIM)[None, :, t - keep_p:]

    ns = dbsz * n_new
    xs = x_sample.reshape(ns, d)
    pos_s = past + jnp.arange(n_new, dtype=jnp.int32)
    tabs_s = tuple(jnp.tile(tb, (dbsz, 1)) for tb in _rope_tables(pos_s))
    qa_s, kva_s, qb_s, nsa_s, win_s, gates_s = _project_all(xs.astype(BF16), wparts, tabs_s)
    by_b = lambda a: a.reshape(dbsz, n_new, a.shape[-1])
    pad_page = lambda a: jnp.pad(by_b(a), ((0, 0), (0, page - n_new), (0, 0)))
    as_records = lambda a, rec: jnp.pad(a.reshape(dbsz, n_new * rec, HEAD_DIM),
                                        ((0, 0), (0, (page - n_new) * rec), (0, 0)))
    q_heads = jnp.pad(qa_s.reshape(dbsz, n_new, N_HEADS_A, HEAD_DIM).transpose(0, 2, 1, 3),
                      ((0, 0), (0, 0), (0, PAD_T - n_new), (0, 0)))
    o_a_h = _sb_decode(q_heads, as_records(kva_s, 2 * N_HEADS_A),
                       cache_a.reshape(n_pool * page * 2 * N_HEADS_A, HEAD_DIM), page_table, page)
    o_a_s = o_a_h.reshape(dbsz, N_HEADS_A, PAD_T, HEAD_DIM)[:, :, :n_new].transpose(0, 2, 1, 3)
    cache_n = cache_nsa.reshape(n_pool * page * NSA_REC, HEAD_DIM)
    part_a, part_b = _compress_pages(cache_n, page_table, pe, w1, page)
    o_cmp, sel = _nsa_select(by_b(qb_s), part_a, part_b, w2, past)
    q_groups = qb_s.reshape(dbsz, n_new, N_KV_B, HPG, HEAD_DIM).transpose(0, 2, 3, 1, 4).reshape(
        dbsz, N_KV_B, HPG * n_new, HEAD_DIM)
    o_slc = _nsa_slc_decode(q_groups, sel, as_records(nsa_s, NSA_REC), cache_n, page_table, past, page,
                            n_new).reshape(dbsz, N_KV_B, HPG * n_new, HEAD_DIM)
    n_buf = state_win.shape[2]
    o_b_s, new_win = _nsa_merge_decode(by_b(qb_s), by_b(gates_s), o_cmp, o_slc,
                                       state_win[0].reshape(dbsz, n_buf, 2 * D_KV_B), pad_page(win_s), past)
    rows_s = -(-ns // LANES) * LANES
    y_s = _post_block(_pad_rows(xs, rows_s), _pad_rows(o_a_s.reshape(ns, D_A), rows_s),
                      _pad_rows(o_b_s.reshape(ns, D_B), rows_s), wts)
    y_sample = y_s[:ns].reshape(dbsz, n_new, d)
    new_a_s = kva_s.reshape(1, dbsz, n_new, 2, N_HEADS_A, HEAD_DIM)
    new_nsa_s = nsa_s.reshape(1, dbsz, n_new, 4, N_KV_B, HEAD_DIM)
    new_win_s = new_win.reshape(1, dbsz, new_win.shape[1], 2, N_KV_B, HEAD_DIM)
    return (y_prompt, y_sample, new_a_p, new_nsa_p, new_win_p, new_a_s, new_nsa_s, new_win_s)
```
